```python
import jax, jax.numpy as jnp
from jax import lax
import numpy as np

D_MODEL = 1024
BATCH = 32
SEQ = 256
DEPTH = 2
DEC_BATCH = 2
DEC_SEQ = 4096
PAST_LEN = 256

GRID_W = 64
N_MIXERS = 2
N_ATTN_LAYERS = (DEPTH + N_MIXERS - 1) // N_MIXERS
N_CONV_LAYERS = DEPTH // N_MIXERS
HEAD_DIM = 64
N_HEADS = D_MODEL // HEAD_DIM
N_KV_HEADS = 4
GROUP = N_HEADS // N_KV_HEADS
QKV_WIDTH = (N_HEADS + 2 * N_KV_HEADS) * HEAD_DIM
WINDOW = 128
BLOCK = 128
ROPE_THETA = 10000.0
CONV_WIDTH = 31
CONV_PAD = (CONV_WIDTH - 1) // 2
D_FF = 4 * D_MODEL
N_MOD = 6
DN_ALPHA = (2.0 * DEPTH) ** 0.25
DN_BETA = (8.0 * DEPTH) ** -0.25
LN_EPS = 1e-5
NEG_INF = -1e30
ATTN_SCALE = HEAD_DIM ** -0.5

kernel_name = "hybrid_diffusion_swa_conformer_step"


def layer_norm(x, g, b):
    xf = x.astype(jnp.float32)
    mu = jnp.mean(xf, axis=-1, keepdims=True)
    var = jnp.mean(jnp.square(xf - mu), axis=-1, keepdims=True)
    y = (xf - mu) * lax.rsqrt(var + LN_EPS)
    return (y * g + b).astype(x.dtype)


def modulation(cond, w, b):
    m = (jax.nn.silu(cond) @ w + b)[..., None, :]
    return jnp.split(m, N_MOD, axis=-1)


def modulate(x, shift, scale):
    return x * (1 + scale) + shift


def post_norm_update(x, sub_out, gate, g, b):
    return layer_norm(DN_ALPHA * x + gate * sub_out, g, b)


def axial_rope_angles(n_tokens):
    rows = n_tokens // GRID_W
    row = jnp.repeat(jnp.arange(rows, dtype=jnp.float32), GRID_W)
    col = jnp.tile(jnp.arange(GRID_W, dtype=jnp.float32), rows)
    half = HEAD_DIM // 2
    inv = ROPE_THETA ** (-jnp.arange(0, half, 2, dtype=jnp.float32) / half)
    return row[:, None] * inv, col[:, None] * inv


def _rotate(x, ang):
    cos = jnp.cos(ang)[:, None, :]
    sin = jnp.sin(ang)[:, None, :]
    d2 = x.shape[-1] // 2
    x1, x2 = x[..., :d2], x[..., d2:]
    return jnp.concatenate([x1 * cos - x2 * sin, x2 * cos + x1 * sin], axis=-1).astype(x.dtype)


def apply_axial_rope(x, ang_r, ang_c):
    half = HEAD_DIM // 2
    return jnp.concatenate([_rotate(x[..., :half], ang_r), _rotate(x[..., half:], ang_c)], axis=-1)


def qkv_project(h, w_qkv):
    B, L, _ = h.shape
    q, k, v = jnp.split(h @ w_qkv, [N_HEADS * HEAD_DIM, (N_HEADS + N_KV_HEADS) * HEAD_DIM], axis=-1)
    return (q.reshape(B, L, N_HEADS, HEAD_DIM),
            k.reshape(B, L, N_KV_HEADS, HEAD_DIM),
            v.reshape(B, L, N_KV_HEADS, HEAD_DIM))


def _sink_column(sink, shape):
    s = sink.astype(jnp.float32).reshape(N_KV_HEADS, GROUP, 1, 1)
    return jnp.broadcast_to(s, shape[:-1] + (1,))


def context_attention(q, k, v, sink):
    B, L = q.shape[:2]
    qg = q.reshape(B, L, N_KV_HEADS, GROUP, HEAD_DIM)
    s = jnp.einsum('bqhgd,bshd->bhgqs', qg, k, preferred_element_type=jnp.float32) * ATTN_SCALE
    p = jax.nn.softmax(jnp.concatenate([s, _sink_column(sink, s.shape)], axis=-1), axis=-1)[..., :-1]
    o = jnp.einsum('bhgqs,bshd->bqhgd', p.astype(v.dtype), v)
    return o.reshape(B, L, D_MODEL)


def latent_window_attention(q, k, v, k_ctx, v_ctx, sink):
    B, L = q.shape[:2]
    nb = L // BLOCK
    qb = q.reshape(B, nb, BLOCK, N_KV_HEADS, GROUP, HEAD_DIM)
    pad = ((0, 0), (BLOCK, BLOCK), (0, 0), (0, 0))
    kr = jnp.pad(k, pad).reshape(B, nb + 2, BLOCK, N_KV_HEADS, HEAD_DIM)
    vr = jnp.pad(v, pad).reshape(B, nb + 2, BLOCK, N_KV_HEADS, HEAD_DIM)
    kb = jnp.concatenate([kr[:, :-2], kr[:, 1:-1], kr[:, 2:]], axis=2)
    vb = jnp.concatenate([vr[:, :-2], vr[:, 1:-1], vr[:, 2:]], axis=2)
    band = 3 * BLOCK
    qi = jnp.arange(BLOCK)[:, None]
    si = jnp.arange(band)[None, :]
    in_window = jnp.abs(qi - si + BLOCK) <= WINDOW
    kpos = (jnp.arange(nb)[:, None] - 1) * BLOCK + si
    in_range = (kpos >= 0) & (kpos < L)
    mask = in_window[None] & in_range[:, None, :]
    s_loc = jnp.einsum('bnqhgd,bnshd->bnhgqs', qb, kb, preferred_element_type=jnp.float32) * ATTN_SCALE
    s_loc = jnp.where(mask[None, :, None, None], s_loc, NEG_INF)
    s_ctx = jnp.einsum('bnqhgd,bchd->bnhgqc', qb, k_ctx, preferred_element_type=jnp.float32) * ATTN_SCALE
    logits = jnp.concatenate([s_loc, s_ctx, _sink_column(sink, s_loc.shape)], axis=-1)
    p = jax.nn.softmax(logits, axis=-1).astype(v.dtype)
    n_ctx = k_ctx.shape[1]
    o = (jnp.einsum('bnhgqs,bnshd->bnqhgd', p[..., :band], vb)
         + jnp.einsum('bnhgqc,bchd->bnqhgd', p[..., band:band + n_ctx], v_ctx))
    return o.reshape(B, L, D_MODEL)


def conformer_conv(h, pw1_w, pw1_b, dw_w, dw_b, cn_g, cn_b, pw2_w, pw2_b):
    a, gte = jnp.split(h @ pw1_w + pw1_b, 2, axis=-1)
    u = a * jax.nn.sigmoid(gte)
    u = lax.conv_general_dilated(u, dw_w[:, None, :].astype(u.dtype), window_strides=(1,),
                                 padding=[(CONV_PAD, CONV_PAD)],
                                 dimension_numbers=('NWC', 'WIO', 'NWC'),
                                 feature_group_count=D_MODEL) + dw_b
    u = jax.nn.silu(layer_norm(u, cn_g, cn_b))
    return u @ pw2_w + pw2_b


def sq_relu_mlp(h, w1, b1, w2, b2):
    return jnp.square(jax.nn.relu(h @ w1 + b1)) @ w2 + b2


def setup_inputs(seed: int = 0) -> dict:
    key = jax.random.key(seed)
    ks = jax.random.split(key, 32)
    f32 = jnp.float32
    D = D_MODEL
    nrm = lambda k, shape, s: jax.random.normal(k, shape, f32) * s
    w_q_k = nrm(ks[0], (N_ATTN_LAYERS, D, (N_HEADS + N_KV_HEADS) * HEAD_DIM), D ** -0.5)
    w_v = nrm(ks[1], (N_ATTN_LAYERS, D, N_KV_HEADS * HEAD_DIM), DN_BETA * D ** -0.5)
    return {
        "x_prompt": nrm(ks[2], (BATCH, SEQ, D), 1.0),
        "x_sample": nrm(ks[3], (DEC_BATCH, DEC_SEQ, D), 1.0),
        "cache_k": nrm(ks[4], (DEC_BATCH, N_ATTN_LAYERS, PAST_LEN, N_KV_HEADS, HEAD_DIM), 1.0),
        "cache_v": nrm(ks[5], (DEC_BATCH, N_ATTN_LAYERS, PAST_LEN, N_KV_HEADS, HEAD_DIM), DN_BETA),
        "c": nrm(ks[6], (DEC_BATCH, D), 1.0),
        "c_ctx": nrm(ks[7], (D,), 1.0),
        "ada_w": nrm(ks[8], (DEPTH, D, N_MOD * D), 0.5 * D ** -0.5),
        "ada_b": nrm(ks[9], (DEPTH, N_MOD * D), 0.02),
        "attn_w_qkv": jnp.concatenate([w_q_k, w_v], axis=-1),
        "attn_w_o": nrm(ks[10], (N_ATTN_LAYERS, D, D), DN_BETA * D ** -0.5),
        "attn_sink": nrm(ks[11], (N_ATTN_LAYERS, N_HEADS), 0.5),
        "conv_pw1_w": nrm(ks[12], (N_CONV_LAYERS, D, 2 * D), D ** -0.5),
        "conv_pw1_b": nrm(ks[13], (N_CONV_LAYERS, 2 * D), 0.02),
        "conv_dw_w": nrm(ks[14], (N_CONV_LAYERS, CONV_WIDTH, D), CONV_WIDTH ** -0.5),
        "conv_dw_b": nrm(ks[15], (N_CONV_LAYERS, D), 0.02),
        "conv_norm_g": 1.0 + nrm(ks[16], (N_CONV_LAYERS, D), 0.02),
        "conv_norm_b": nrm(ks[17], (N_CONV_LAYERS, D), 0.02),
        "conv_pw2_w": nrm(ks[18], (N_CONV_LAYERS, D, D), DN_BETA * D ** -0.5),
        "conv_pw2_b": nrm(ks[19], (N_CONV_LAYERS, D), 0.02),
        "ln1_g": 1.0 + nrm(ks[20], (DEPTH, D), 0.02),
        "ln1_b": nrm(ks[21], (DEPTH, D), 0.02),
        "mlp_w1": nrm(ks[22], (DEPTH, D, D_FF), D ** -0.5),
        "mlp_b1": nrm(ks[23], (DEPTH, D_FF), 0.02),
        "mlp_w2": nrm(ks[24], (DEPTH, D_FF, D), DN_BETA * D_FF ** -0.5),
        "mlp_b2": nrm(ks[25], (DEPTH, D), 0.02),
        "ln2_g": 1.0 + nrm(ks[26], (DEPTH, D), 0.02),
        "ln2_b": nrm(ks[27], (DEPTH, D), 0.02),
    }


def reference(x_prompt, x_sample, cache_k, cache_v, c, c_ctx, ada_w, ada_b,
              attn_w_qkv, attn_w_o, attn_sink,
              conv_pw1_w, conv_pw1_b, conv_dw_w, conv_dw_b, conv_norm_g, conv_norm_b,
              conv_pw2_w, conv_pw2_b,
              ln1_g, ln1_b, mlp_w1, mlp_b1, mlp_w2, mlp_b2, ln2_g, ln2_b):
    ang_r, ang_c = axial_rope_angles(x_sample.shape[1])
    yp, ys = x_prompt, x_sample
    new_k, new_v = [], []
    for i in range(DEPTH):
        sh1p, sc1p, g1p, sh2p, sc2p, g2p = modulation(c_ctx, ada_w[i], ada_b[i])
        sh1s, sc1s, g1s, sh2s, sc2s, g2s = modulation(c, ada_w[i], ada_b[i])
        hp = modulate(yp, sh1p, sc1p)
        hs = modulate(ys, sh1s, sc1s)
        j = i // N_MIXERS
        if i % N_MIXERS == 0:
            qp, kp, vp = qkv_project(hp, attn_w_qkv[j])
            new_k.append(kp)
            new_v.append(vp)
            op = context_attention(qp, kp, vp, attn_sink[j]) @ attn_w_o[j]
            qs, ks_, vs = qkv_project(hs, attn_w_qkv[j])
            qs = apply_axial_rope(qs, ang_r, ang_c)
            ks_ = apply_axial_rope(ks_, ang_r, ang_c)
            os_ = latent_window_attention(qs, ks_, vs, cache_k[:, j], cache_v[:, j], attn_sink[j]) @ attn_w_o[j]
        else:
            cw = (conv_pw1_w[j], conv_pw1_b[j], conv_dw_w[j], conv_dw_b[j],
                  conv_norm_g[j], conv_norm_b[j], conv_pw2_w[j], conv_pw2_b[j])
            op = conformer_conv(hp, *cw)
            os_ = conformer_conv(hs, *cw)
        yp = post_norm_update(yp, op, g1p, ln1_g[i], ln1_b[i])
        ys = post_norm_update(ys, os_, g1s, ln1_g[i], ln1_b[i])
        mw = (mlp_w1[i], mlp_b1[i], mlp_w2[i], mlp_b2[i])
        yp = post_norm_update(yp, sq_relu_mlp(modulate(yp, sh2p, sc2p), *mw), g2p, ln2_g[i], ln2_b[i])
        ys = post_norm_update(ys, sq_relu_mlp(modulate(ys, sh2s, sc2s), *mw), g2s, ln2_g[i], ln2_b[i])
    new_cache_k = jnp.stack(new_k, axis=1)
    new_cache_v = jnp.stack(new_v, axis=1)
    return (yp, ys, new_cache_k, new_cache_v)
```

```python
import functools

import jax
import jax.numpy as jnp
from jax import lax
from jax.experimental import pallas as pl
from jax.experimental.pallas import tpu as pltpu

D_MODEL = 1024
GRID_W = 64
N_MIXERS = 2
HEAD_DIM = 64
N_HEADS = D_MODEL // HEAD_DIM
N_KV_HEADS = 4
GROUP = N_HEADS // N_KV_HEADS
KV_WIDTH = N_KV_HEADS * HEAD_DIM
QK_WIDTH = D_MODEL + KV_WIDTH
QKV_WIDTH = D_MODEL + 2 * KV_WIDTH
BLOCK = 128
ROPE_THETA = 10000.0
CONV_WIDTH = 31
CONV_PAD = (CONV_WIDTH - 1) // 2
D_FF = 4 * D_MODEL
N_MOD = 6
LN_EPS = 1e-5
NEG_INF = -1e30
ATTN_SCALE = HEAD_DIM ** -0.5

LANES = 128
SUBLANES = 8
VMEM_LIMIT_BYTES = 56 * 1024 * 1024

MOD_ROWS = 8
COND_ROWS = 8
HALO = 16

BF16 = jnp.bfloat16
F32 = jnp.float32


def _const_spec(shape):
    n = len(shape)
    return pl.BlockSpec(shape, lambda *_: (0,) * n, pipeline_mode=pl.Buffered(1))


def _params(n_axes):
    return pltpu.CompilerParams(dimension_semantics=("arbitrary",) * n_axes,
                                vmem_limit_bytes=VMEM_LIMIT_BYTES)


def _layer_norm(z, g, b):
    mu = jnp.mean(z, axis=-1, keepdims=True)
    zc = z - mu
    var = jnp.mean(zc * zc, axis=-1, keepdims=True)
    return zc * lax.rsqrt(var + LN_EPS) * g + b


def _bdot(a, b):
    return jnp.dot(a, b, preferred_element_type=F32)


def _bdot_nt(a, b):
    return lax.dot_general(a, b, (((1,), (1,)), ((), ())), preferred_element_type=F32)


def _mod_kernel(cond_ref, w_ref, b_ref, out_ref):
    c = cond_ref[...]
    s = c * jax.nn.sigmoid(c)
    out_ref[0] = jnp.dot(s, w_ref[0], preferred_element_type=F32,
                         precision=lax.Precision.HIGHEST) + b_ref[0]


def _modulation(cond, ada_w, ada_b):
    depth = ada_w.shape[0]
    tn = 1536
    width = N_MOD * D_MODEL
    return pl.pallas_call(
        _mod_kernel,
        grid=(depth, width // tn),
        in_specs=[
            pl.BlockSpec((COND_ROWS, D_MODEL), lambda l, n: (0, 0)),
            pl.BlockSpec((1, D_MODEL, tn), lambda l, n: (l, 0, n)),
            pl.BlockSpec((1, 1, tn), lambda l, n: (l, 0, n)),
        ],
        out_specs=pl.BlockSpec((1, COND_ROWS, tn), lambda l, n: (l, 0, n)),
        out_shape=jax.ShapeDtypeStruct((depth, COND_ROWS, width), F32),
        compiler_params=_params(2),
        name="modulation",
    )(cond, ada_w, ada_b.reshape(depth, 1, width))


def _mlp_kernel(y_ref, mod_ref, w1_ref, b1_ref, w2_ref, b2_ref, g_ref, b_ref, out_ref, *,
                alpha, ff_chunk):
    y = y_ref[...]
    m = mod_ref[0]
    h = (y * (1.0 + m[4:5]) + m[3:4]).astype(BF16)
    acc = jnp.zeros(y.shape, F32)
    for c in range(D_FF // ff_chunk):
        cols = slice(c * ff_chunk, (c + 1) * ff_chunk)
        a = _bdot(h, w1_ref[:, cols]) + b1_ref[:, cols]
        a = jnp.maximum(a, 0.0)
        acc = acc + _bdot((a * a).astype(BF16), w2_ref[cols, :])
    z = alpha * y + m[5:6] * (acc + b2_ref[...])
    out_ref[...] = _layer_norm(z, g_ref[...], b_ref[...])


def _mlp(y, mod, rows_per_group, w1, b1, w2, b2, g, b, alpha):
    m_rows = y.shape[0]
    tm = 512
    kern = functools.partial(_mlp_kernel, alpha=alpha, ff_chunk=1024)
    return pl.pallas_call(
        kern,
        grid=(m_rows // tm,),
        in_specs=[
            pl.BlockSpec((tm, D_MODEL), lambda i: (i, 0)),
            pl.BlockSpec((1, MOD_ROWS, D_MODEL), lambda i: (i * tm // rows_per_group, 0, 0)),
            _const_spec((D_MODEL, D_FF)),
            _const_spec((1, D_FF)),
            _const_spec((D_FF, D_MODEL)),
            _const_spec((1, D_MODEL)),
            _const_spec((1, D_MODEL)),
            _const_spec((1, D_MODEL)),
        ],
        out_specs=pl.BlockSpec((tm, D_MODEL), lambda i: (i, 0)),
        out_shape=jax.ShapeDtypeStruct((m_rows, D_MODEL), F32),
        compiler_params=_params(1),
        name="mlp",
    )(y, mod, w1, b1, w2, b2, g, b)


def _head_pair_operand(x, kv_head):
    tile = x[:, LANES * (kv_head // 2):LANES * (kv_head // 2 + 1)]
    swapped = pltpu.roll(tile, HEAD_DIM, axis=1)
    lane = lax.broadcasted_iota(jnp.int32, (1, LANES), 1)
    low = lane < HEAD_DIM
    if kv_head % 2 == 0:
        lo_src, hi_src = tile, swapped
    else:
        lo_src, hi_src = swapped, tile
    lo = jnp.where(low, lo_src, 0.0)
    hi = jnp.where(low, 0.0, hi_src)
    return jnp.concatenate([lo, hi], axis=0).astype(BF16)


def _pair_attention(q_pair, k2, v2, sink0, sink1, mask):
    n_keys = k2.shape[0] // 2
    s = _bdot_nt(q_pair, k2)
    outs = []
    ps = []
    rs = []
    for half, sink in ((0, sink0), (1, sink1)):
        sh = s[:, half * n_keys:(half + 1) * n_keys]
        if mask is not None:
            sh = jnp.where(mask, sh, NEG_INF)
        mx = jnp.maximum(jnp.max(sh, axis=-1, keepdims=True), sink)
        p = jnp.exp(sh - mx)
        den = jnp.sum(p, axis=-1, keepdims=True) + jnp.exp(sink - mx)
        ps.append(p.astype(BF16))
        rs.append(1.0 / den)
    o = _bdot(jnp.concatenate(ps, axis=1), v2)
    lane = lax.broadcasted_iota(jnp.int32, (1, LANES), 1)
    return o * jnp.where(lane < HEAD_DIM, rs[0], rs[1])


def _prompt_attn_kernel(sink_ref, x_ref, mod_ref, wqkv_ref, wo_ref, g_ref, b_ref,
                        y_ref, k_ref, v_ref, o_scr, *, alpha):
    x = x_ref[0]
    m = mod_ref[0]
    h = (x * (1.0 + m[1:2]) + m[0:1]).astype(BF16)
    qkv = _bdot(h, wqkv_ref[...])
    k = qkv[:, D_MODEL:QK_WIDTH]
    v = qkv[:, QK_WIDTH:]
    k_ref[0] = k
    v_ref[0] = v
    q = (qkv[:, :D_MODEL] * ATTN_SCALE).astype(BF16)
    for kv_head in range(N_KV_HEADS):
        k2 = _head_pair_operand(k, kv_head)
        v2 = _head_pair_operand(v, kv_head)
        for t in (2 * kv_head, 2 * kv_head + 1):
            o = _pair_attention(q[:, LANES * t:LANES * (t + 1)], k2, v2,
                                sink_ref[2 * t], sink_ref[2 * t + 1], None)
            o_scr[:, LANES * t:LANES * (t + 1)] = o.astype(BF16)
    out = _bdot(o_scr[...], wo_ref[...])
    z = alpha * x + m[2:3] * out
    y_ref[0] = _layer_norm(z, g_ref[...], b_ref[...])


def _prompt_attn_layer(x, mod, sink, wqkv, wo, g, b, alpha):
    bsz, seq, _ = x.shape
    kern = functools.partial(_prompt_attn_kernel, alpha=alpha)
    return pl.pallas_call(
        kern,
        grid=(bsz,),
        in_specs=[
            pl.BlockSpec(memory_space=pltpu.SMEM),
            pl.BlockSpec((1, seq, D_MODEL), lambda i: (i, 0, 0)),
            pl.BlockSpec((1, MOD_ROWS, D_MODEL), lambda i: (0, 0, 0)),
            _const_spec((D_MODEL, QKV_WIDTH)),
            _const_spec((D_MODEL, D_MODEL)),
            _const_spec((1, D_MODEL)),
            _const_spec((1, D_MODEL)),
        ],
        out_specs=[
            pl.BlockSpec((1, seq, D_MODEL), lambda i: (i, 0, 0)),
            pl.BlockSpec((1, seq, KV_WIDTH), lambda i: (i, 0, 0)),
            pl.BlockSpec((1, seq, KV_WIDTH), lambda i: (i, 0, 0)),
        ],
        out_shape=[
            jax.ShapeDtypeStruct((bsz, seq, D_MODEL), F32),
            jax.ShapeDtypeStruct((bsz, seq, KV_WIDTH), F32),
            jax.ShapeDtypeStruct((bsz, seq, KV_WIDTH), F32),
        ],
        scratch_shapes=[pltpu.VMEM((seq, D_MODEL), BF16)],
        compiler_params=_params(1),
        name="prompt_attn",
    )(sink, x, mod, wqkv, wo, g, b)


def _rope_tables(n_tokens):
    rows = n_tokens // GRID_W
    row = jnp.repeat(jnp.arange(rows, dtype=F32), GRID_W)
    col = jnp.tile(jnp.arange(GRID_W, dtype=F32), rows)
    half = HEAD_DIM // 2
    inv = ROPE_THETA ** (-jnp.arange(0, half, 2, dtype=F32) / half)
    ang_r = row[:, None] * inv
    ang_c = col[:, None] * inv
    cos = jnp.concatenate([jnp.cos(ang_r)] * 2 + [jnp.cos(ang_c)] * 2, axis=-1)
    sin = jnp.concatenate([-jnp.sin(ang_r), jnp.sin(ang_r), -jnp.sin(ang_c), jnp.sin(ang_c)], axis=-1)
    return jnp.tile(cos, (1, LANES // HEAD_DIM)), jnp.tile(sin, (1, LANES // HEAD_DIM))


def _sample_qkv_kernel(x_ref, mod_ref, cos_ref, sin_ref, wqkv_ref, q_ref, k_ref, v_ref):
    x = x_ref[...]
    m = mod_ref[0]
    h = (x * (1.0 + m[1:2]) + m[0:1]).astype(BF16)
    qkv = _bdot(h, wqkv_ref[...])
    cos = cos_ref[...]
    sin = sin_ref[...]
    lane = lax.broadcasted_iota(jnp.int32, (1, LANES), 1)
    first = (lane % (HEAD_DIM // 2)) < (HEAD_DIM // 4)
    quarter = HEAD_DIM // 4
    for t in range(QK_WIDTH // LANES):
        tile = qkv[:, LANES * t:LANES * (t + 1)]
        partner = jnp.where(first, pltpu.roll(tile, LANES - quarter, axis=1),
                            pltpu.roll(tile, quarter, axis=1))
        rot = tile * cos + partner * sin
        if t < D_MODEL // LANES:
            q_ref[:, LANES * t:LANES * (t + 1)] = (rot * ATTN_SCALE).astype(BF16)
        else:
            tk = t - D_MODEL // LANES
            k_ref[:, LANES * tk:LANES * (tk + 1)] = rot.astype(BF16)
    v_ref[...] = qkv[:, QK_WIDTH:].astype(BF16)


def _sample_qkv(x, mod, seq, cos, sin, wqkv):
    m_rows = x.shape[0]
    tm = 512
    tiles_per_seq = seq // tm
    return pl.pallas_call(
        _sample_qkv_kernel,
        grid=(m_rows // tm,),
        in_specs=[
            pl.BlockSpec((tm, D_MODEL), lambda i: (i, 0)),
            pl.BlockSpec((1, MOD_ROWS, D_MODEL), lambda i: (i // tiles_per_seq, 0, 0)),
            pl.BlockSpec((tm, LANES), lambda i: (i % tiles_per_seq, 0)),
            pl.BlockSpec((tm, LANES), lambda i: (i % tiles_per_seq, 0)),
            _const_spec((D_MODEL, QKV_WIDTH)),
        ],
        out_specs=[
            pl.BlockSpec((tm, D_MODEL), lambda i: (i, 0)),
            pl.BlockSpec((tm, KV_WIDTH), lambda i: (i, 0)),
            pl.BlockSpec((tm, KV_WIDTH), lambda i: (i, 0)),
        ],
        out_shape=[
            jax.ShapeDtypeStruct((m_rows, D_MODEL), BF16),
            jax.ShapeDtypeStruct((m_rows, KV_WIDTH), BF16),
            jax.ShapeDtypeStruct((m_rows, KV_WIDTH), BF16),
        ],
        compiler_params=_params(1),
        name="sample_qkv",
    )(x, mod, cos, sin, wqkv)


def _sample_attn_kernel(sink_ref, x_ref, mod_ref, q_ref,
                        kp_ref, km_ref, kn_ref, vp_ref, vm_ref, vn_ref, ck_ref, cv_ref,
                        wo_ref, g_ref, b_ref, y_ref, o_scr, *, alpha, tq):
    i = pl.program_id(1)
    n_tiles = pl.num_programs(1)
    x = x_ref[...]
    m = mod_ref[0]
    k_all = jnp.concatenate([kp_ref[...].astype(F32), km_ref[...].astype(F32),
                             kn_ref[...].astype(F32), ck_ref[0]], axis=0)
    v_all = jnp.concatenate([vp_ref[...].astype(F32), vm_ref[...].astype(F32),
                             vn_ref[...].astype(F32), cv_ref[0]], axis=0)
    n_ctx = ck_ref.shape[1]
    n_local = tq + 2 * BLOCK
    qi = lax.broadcasted_iota(jnp.int32, (tq, n_local + n_ctx), 0)
    si = lax.broadcasted_iota(jnp.int32, (tq, n_local + n_ctx), 1)
    kpos = si - BLOCK
    in_window = jnp.abs(qi - kpos) <= BLOCK
    first_pos = jnp.where(i > 0, -BLOCK, 0)
    end_pos = jnp.where(i < n_tiles - 1, tq + BLOCK, tq)
    in_range = (kpos >= first_pos) & (kpos < end_pos)
    mask = (in_window & in_range) | (si >= n_local)
    for kv_head in range(N_KV_HEADS):
        k2 = _head_pair_operand(k_all, kv_head)
        v2 = _head_pair_operand(v_all, kv_head)
        for t in (2 * kv_head, 2 * kv_head + 1):
            o = _pair_attention(q_ref[:, LANES * t:LANES * (t + 1)], k2, v2,
                                sink_ref[2 * t], sink_ref[2 * t + 1], mask)
            o_scr[:, LANES * t:LANES * (t + 1)] = o.astype(BF16)
    out = _bdot(o_scr[...], wo_ref[...])
    z = alpha * x + m[2:3] * out
    y_ref[...] = _layer_norm(z, g_ref[...], b_ref[...])


def _sample_attn_layer(x, mod, seq, q, k, v, ctx_k, ctx_v, sink, wo, g, b, alpha):
    m_rows = x.shape[0]
    bsz = m_rows // seq
    tq = 2 * BLOCK
    tiles = seq // tq
    blocks = seq // BLOCK
    n_ctx = ctx_k.shape[1]
    kern = functools.partial(_sample_attn_kernel, alpha=alpha, tq=tq)
    row_tile = lambda bi, i: (bi * tiles + i, 0)
    prev_blk = lambda bi, i: (bi * blocks + jnp.maximum(2 * i - 1, 0), 0)
    next_blk = lambda bi, i: (bi * blocks + jnp.minimum(2 * i + 2, blocks - 1), 0)
    ctx_blk = lambda bi, i: (bi, 0, 0)
    return pl.pallas_call(
        kern,
        grid=(bsz, tiles),
        in_specs=[
            pl.BlockSpec(memory_space=pltpu.SMEM),
            pl.BlockSpec((tq, D_MODEL), row_tile),
            pl.BlockSpec((1, MOD_ROWS, D_MODEL), ctx_blk),
            pl.BlockSpec((tq, D_MODEL), row_tile),
            pl.BlockSpec((BLOCK, KV_WIDTH), prev_blk),
            pl.BlockSpec((tq, KV_WIDTH), row_tile),
            pl.BlockSpec((BLOCK, KV_WIDTH), next_blk),
            pl.BlockSpec((BLOCK, KV_WIDTH), prev_blk),
            pl.BlockSpec((tq, KV_WIDTH), row_tile),
            pl.BlockSpec((BLOCK, KV_WIDTH), next_blk),
            pl.BlockSpec((1, n_ctx, KV_WIDTH), ctx_blk),
            pl.BlockSpec((1, n_ctx, KV_WIDTH), ctx_blk),
            _const_spec((D_MODEL, D_MODEL)),
            _const_spec((1, D_MODEL)),
            _const_spec((1, D_MODEL)),
        ],
        out_specs=pl.BlockSpec((tq, D_MODEL), row_tile),
        out_shape=jax.ShapeDtypeStruct((m_rows, D_MODEL), F32),
        scratch_shapes=[pltpu.VMEM((tq, D_MODEL), BF16)],
        compiler_params=_params(2),
        name="sample_attn",
    )(sink, x, mod, q, k, k, k, v, v, v, ctx_k, ctx_v, wo, g, b)


def _conv_kernel(xp_ref, x_ref, xn_ref, mod_ref, pw1_ref, pb1_ref, dw_ref, db_ref,
                 cg_ref, cb_ref, pw2_ref, pb2_ref, g_ref, b_ref, y_ref, u_scr, *,
                 alpha, tm, tiles_per_seq):
    i = pl.program_id(0)
    m = mod_ref[0]

    def glu(rows):
        h = (rows * (1.0 + m[1:2]) + m[0:1]).astype(BF16)
        a = _bdot(h, pw1_ref[...]) + pb1_ref[...]
        return a[:, :D_MODEL] * jax.nn.sigmoid(a[:, D_MODEL:])

    x = x_ref[...]
    u_scr[HALO:HALO + tm, :] = glu(x)
    if tiles_per_seq > 1:
        halo = glu(jnp.concatenate([xp_ref[...], xn_ref[...]], axis=0))
        t = i % tiles_per_seq
        u_scr[0:HALO, :] = jnp.where(t > 0, halo[:HALO], 0.0)
        u_scr[HALO + tm:, :] = jnp.where(t < tiles_per_seq - 1, halo[HALO:], 0.0)
    else:
        u_scr[0:HALO, :] = jnp.zeros((HALO, D_MODEL), F32)
        u_scr[HALO + tm:, :] = jnp.zeros((HALO, D_MODEL), F32)
    acc = jnp.zeros((tm, D_MODEL), F32) + db_ref[...]
    for tap in range(CONV_WIDTH):
        off = HALO - CONV_PAD + tap
        acc = acc + dw_ref[tap:tap + 1, :] * u_scr[off:off + tm, :]
    un = _layer_norm(acc, cg_ref[...], cb_ref[...])
    un = un * jax.nn.sigmoid(un)
    out = _bdot(un.astype(BF16), pw2_ref[...]) + pb2_ref[...]
    z = alpha * x + m[2:3] * out
    y_ref[...] = _layer_norm(z, g_ref[...], b_ref[...])


def _conv_layer(x, mod, rows_per_group, seq, pw1, pb1, dw, db, cg, cb, pw2, pb2, g, b, alpha):
    m_rows = x.shape[0]
    tm = min(seq, 512)
    tiles_per_seq = seq // tm
    halo_per_tile = tm // HALO
    n_halo_blocks = m_rows // HALO
    kern = functools.partial(_conv_kernel, alpha=alpha, tm=tm, tiles_per_seq=tiles_per_seq)
    prev_blk = lambda i: (jnp.maximum(i * halo_per_tile - 1, 0), 0)
    next_blk = lambda i: (jnp.minimum((i + 1) * halo_per_tile, n_halo_blocks - 1), 0)
    return pl.pallas_call(
        kern,
        grid=(m_rows // tm,),
        in_specs=[
            pl.BlockSpec((HALO, D_MODEL), prev_blk),
            pl.BlockSpec((tm, D_MODEL), lambda i: (i, 0)),
            pl.BlockSpec((HALO, D_MODEL), next_blk),
            pl.BlockSpec((1, MOD_ROWS, D_MODEL), lambda i: (i * tm // rows_per_group, 0, 0)),
            _const_spec((D_MODEL, 2 * D_MODEL)),
            _const_spec((1, 2 * D_MODEL)),
            _const_spec((CONV_WIDTH, D_MODEL)),
            _const_spec((1, D_MODEL)),
            _const_spec((1, D_MODEL)),
            _const_spec((1, D_MODEL)),
            _const_spec((D_MODEL, D_MODEL)),
            _const_spec((1, D_MODEL)),
            _const_spec((1, D_MODEL)),
            _const_spec((1, D_MODEL)),
        ],
        out_specs=pl.BlockSpec((tm, D_MODEL), lambda i: (i, 0)),
        out_shape=jax.ShapeDtypeStruct((m_rows, D_MODEL), F32),
        scratch_shapes=[pltpu.VMEM((tm + 2 * HALO, D_MODEL), F32)],
        compiler_params=_params(1),
        name="conv",
    )(x, x, x, mod, pw1, pb1, dw, db, cg, cb, pw2, pb2, g, b)


def kernel(x_prompt, x_sample, cache_k, cache_v, c, c_ctx, ada_w, ada_b, attn_w_qkv, attn_w_o, attn_sink, conv_pw1_w, conv_pw1_b, conv_dw_w, conv_dw_b, conv_norm_g, conv_norm_b, conv_pw2_w, conv_pw2_b, ln1_g, ln1_b, mlp_w1, mlp_b1, mlp_w2, mlp_b2, ln2_g, ln2_b):
    depth = ada_w.shape[0]
    bsz, seq, _ = x_prompt.shape
    dec_bsz, dec_seq, _ = x_sample.shape
    n_ctx = cache_k.shape[2]
    alpha = (2.0 * depth) ** 0.25
    assert 1 + dec_bsz <= COND_ROWS

    row = lambda a: a.reshape(1, -1)

    cond = jnp.concatenate([c_ctx[None], c, jnp.zeros((COND_ROWS - 1 - dec_bsz, D_MODEL), F32)], axis=0)
    mods = _modulation(cond, ada_w, ada_b).reshape(depth, COND_ROWS, N_MOD, D_MODEL)
    mods = jnp.pad(mods, ((0, 0), (0, 0), (0, MOD_ROWS - N_MOD), (0, 0)))

    cos, sin = _rope_tables(dec_seq)

    yp = x_prompt
    ys = x_sample.reshape(dec_bsz * dec_seq, D_MODEL)
    new_k, new_v = [], []
    for i in range(depth):
        mod_p = mods[i, 0:1]
        mod_s = mods[i, 1:1 + dec_bsz]
        j = i // N_MIXERS
        if i % N_MIXERS == 0:
            wqkv = attn_w_qkv[j].astype(BF16)
            wo = attn_w_o[j].astype(BF16)
            yp, kp, vp = _prompt_attn_layer(yp, mod_p, attn_sink[j], wqkv, wo,
                                            row(ln1_g[i]), row(ln1_b[i]), alpha)
            new_k.append(kp.reshape(bsz, seq, N_KV_HEADS, HEAD_DIM))
            new_v.append(vp.reshape(bsz, seq, N_KV_HEADS, HEAD_DIM))
            q, k, v = _sample_qkv(ys, mod_s, dec_seq, cos, sin, wqkv)
            ys = _sample_attn_layer(ys, mod_s, dec_seq, q, k, v,
                                    cache_k[:, j].reshape(dec_bsz, n_ctx, KV_WIDTH),
                                    cache_v[:, j].reshape(dec_bsz, n_ctx, KV_WIDTH),
                                    attn_sink[j], wo, row(ln1_g[i]), row(ln1_b[i]), alpha)
        else:
            cw = (conv_pw1_w[j].astype(BF16), row(conv_pw1_b[j]), conv_dw_w[j], row(conv_dw_b[j]),
                  row(conv_norm_g[j]), row(conv_norm_b[j]), conv_pw2_w[j].astype(BF16),
                  row(conv_pw2_b[j]), row(ln1_g[i]), row(ln1_b[i]))
            yp = _conv_layer(yp.reshape(bsz * seq, D_MODEL), mod_p, bsz * seq, seq, *cw,
                             alpha).reshape(bsz, seq, D_MODEL)
            ys = _conv_layer(ys, mod_s, dec_seq, dec_seq, *cw, alpha)
        mw = (mlp_w1[i].astype(BF16), row(mlp_b1[i]), mlp_w2[i].astype(BF16), row(mlp_b2[i]),
              row(ln2_g[i]), row(ln2_b[i]))
        yp = _mlp(yp.reshape(bsz * seq, D_MODEL), mod_p, bsz * seq, *mw, alpha).reshape(bsz, seq, D_MODEL)
        ys = _mlp(ys, mod_s, dec_seq, *mw, alpha)
    new_cache_k = jnp.stack(new_k, axis=1)
    new_cache_v = jnp.stack(new_v, axis=1)
    return (yp, ys.reshape(dec_bsz, dec_seq, D_MODEL), new_cache_k, new_cache_v)
```

```python
import functools

import jax
import jax.numpy as jnp
from jax import lax
from jax.experimental import pallas as pl
from jax.experimental.pallas import tpu as pltpu

D_MODEL = 1024
GRID_W = 64
N_MIXERS = 2
HEAD_DIM = 64
N_HEADS = D_MODEL // HEAD_DIM
N_KV_HEADS = 4
GROUP = N_HEADS // N_KV_HEADS
KV_WIDTH = N_KV_HEADS * HEAD_DIM
QK_WIDTH = D_MODEL + KV_WIDTH
QKV_WIDTH = D_MODEL + 2 * KV_WIDTH
BLOCK = 128
ROPE_THETA = 10000.0
CONV_WIDTH = 31
CONV_PAD = (CONV_WIDTH - 1) // 2
D_FF = 4 * D_MODEL
N_MOD = 6
LN_EPS = 1e-5
NEG_INF = -1e30
ATTN_SCALE = HEAD_DIM ** -0.5

LANES = 128
SUBLANES = 8
VMEM_LIMIT_BYTES = 56 * 1024 * 1024

MOD_ROWS = 8
COND_ROWS = 8
HALO = 16
CONV_ROW_STRIDE = 4
CONV_ROW_BLOCK = 128

BF16 = jnp.bfloat16
F32 = jnp.float32


def _const_spec(shape):
    n = len(shape)
    return pl.BlockSpec(shape, lambda *_: (0,) * n, pipeline_mode=pl.Buffered(1))


def _params(n_axes):
    return pltpu.CompilerParams(dimension_semantics=("arbitrary",) * n_axes,
                                vmem_limit_bytes=VMEM_LIMIT_BYTES)


def _layer_norm(z, g, b):
    mu = jnp.mean(z, axis=-1, keepdims=True)
    zc = z - mu
    var = jnp.mean(zc * zc, axis=-1, keepdims=True)
    return zc * lax.rsqrt(var + LN_EPS) * g + b


def _bdot(a, b):
    return jnp.dot(a, b, preferred_element_type=F32)


def _bdot_nt(a, b):
    return lax.dot_general(a, b, (((1,), (1,)), ((), ())), preferred_element_type=F32)


def _mod_kernel(cond_ref, w_ref, b_ref, out_ref):
    c = cond_ref[...]
    s = c * jax.nn.sigmoid(c)
    out_ref[0] = jnp.dot(s, w_ref[0], preferred_element_type=F32,
                         precision=lax.Precision.HIGHEST) + b_ref[0]


def _modulation(cond, ada_w, ada_b):
    depth = ada_w.shape[0]
    tn = 1536
    width = N_MOD * D_MODEL
    return pl.pallas_call(
        _mod_kernel,
        grid=(depth, width // tn),
        in_specs=[
            pl.BlockSpec((COND_ROWS, D_MODEL), lambda l, n: (0, 0)),
            pl.BlockSpec((1, D_MODEL, tn), lambda l, n: (l, 0, n)),
            pl.BlockSpec((1, 1, tn), lambda l, n: (l, 0, n)),
        ],
        out_specs=pl.BlockSpec((1, COND_ROWS, tn), lambda l, n: (l, 0, n)),
        out_shape=jax.ShapeDtypeStruct((depth, COND_ROWS, width), F32),
        compiler_params=_params(2),
        name="modulation",
    )(cond, ada_w, ada_b.reshape(depth, 1, width))


def _mlp_kernel(y_ref, mod_ref, w1_ref, b1_ref, w2_ref, b2_ref, g_ref, b_ref, out_ref, *,
                alpha, ff_chunk):
    y = y_ref[...]
    m = mod_ref[0]
    h = (y * (1.0 + m[4:5]) + m[3:4]).astype(BF16)
    acc = jnp.zeros(y.shape, F32)
    for c in range(D_FF // ff_chunk):
        cols = slice(c * ff_chunk, (c + 1) * ff_chunk)
        a = _bdot(h, w1_ref[:, cols]) + b1_ref[:, cols]
        a = jnp.maximum(a, 0.0)
        acc = acc + _bdot((a * a).astype(BF16), w2_ref[cols, :])
    z = alpha * y + m[5:6] * (acc + b2_ref[...])
    out_ref[...] = _layer_norm(z, g_ref[...], b_ref[...])


def _mlp(y, mod, rows_per_group, w1, b1, w2, b2, g, b, alpha):
    m_rows = y.shape[0]
    tm = 512
    kern = functools.partial(_mlp_kernel, alpha=alpha, ff_chunk=1024)
    return pl.pallas_call(
        kern,
        grid=(m_rows // tm,),
        in_specs=[
            pl.BlockSpec((tm, D_MODEL), lambda i: (i, 0)),
            pl.BlockSpec((1, MOD_ROWS, D_MODEL), lambda i: (i * tm // rows_per_group, 0, 0)),
            _const_spec((D_MODEL, D_FF)),
            _const_spec((1, D_FF)),
            _const_spec((D_FF, D_MODEL)),
            _const_spec((1, D_MODEL)),
            _const_spec((1, D_MODEL)),
            _const_spec((1, D_MODEL)),
        ],
        out_specs=pl.BlockSpec((tm, D_MODEL), lambda i: (i, 0)),
        out_shape=jax.ShapeDtypeStruct((m_rows, D_MODEL), F32),
        compiler_params=_params(1),
        name="mlp",
    )(y, mod, w1, b1, w2, b2, g, b)


def _head_pair_operand(x, kv_head):
    tile = x[:, LANES * (kv_head // 2):LANES * (kv_head // 2 + 1)]
    swapped = pltpu.roll(tile, HEAD_DIM, axis=1)
    lane = lax.broadcasted_iota(jnp.int32, (1, LANES), 1)
    low = lane < HEAD_DIM
    if kv_head % 2 == 0:
        lo_src, hi_src = tile, swapped
    else:
        lo_src, hi_src = swapped, tile
    lo = jnp.where(low, lo_src, 0.0)
    hi = jnp.where(low, 0.0, hi_src)
    return jnp.concatenate([lo, hi], axis=0).astype(BF16)


def _pair_attention(q_pair, k2, v2, sink0, sink1, mask):
    n_keys = k2.shape[0] // 2
    s = _bdot_nt(q_pair, k2)
    outs = []
    ps = []
    rs = []
    for half, sink in ((0, sink0), (1, sink1)):
        sh = s[:, half * n_keys:(half + 1) * n_keys]
        if mask is not None:
            sh = jnp.where(mask, sh, NEG_INF)
        mx = jnp.maximum(jnp.max(sh, axis=-1, keepdims=True), sink)
        p = jnp.exp(sh - mx)
        den = jnp.sum(p, axis=-1, keepdims=True) + jnp.exp(sink - mx)
        ps.append(p.astype(BF16))
        rs.append(1.0 / den)
    o = _bdot(jnp.concatenate(ps, axis=1), v2)
    lane = lax.broadcasted_iota(jnp.int32, (1, LANES), 1)
    return o * jnp.where(lane < HEAD_DIM, rs[0], rs[1])


def _prompt_attn_kernel(sink_ref, x_ref, mod_ref, wqkv_ref, wo_ref, g_ref, b_ref,
                        y_ref, k_ref, v_ref, o_scr, *, alpha):
    x = x_ref[0]
    m = mod_ref[0]
    h = (x * (1.0 + m[1:2]) + m[0:1]).astype(BF16)
    qkv = _bdot(h, wqkv_ref[...])
    k = qkv[:, D_MODEL:QK_WIDTH]
    v = qkv[:, QK_WIDTH:]
    k_ref[0] = k
    v_ref[0] = v
    q = (qkv[:, :D_MODEL] * ATTN_SCALE).astype(BF16)
    for kv_head in range(N_KV_HEADS):
        k2 = _head_pair_operand(k, kv_head)
        v2 = _head_pair_operand(v, kv_head)
        for t in (2 * kv_head, 2 * kv_head + 1):
            o = _pair_attention(q[:, LANES * t:LANES * (t + 1)], k2, v2,
                                sink_ref[2 * t], sink_ref[2 * t + 1], None)
            o_scr[:, LANES * t:LANES * (t + 1)] = o.astype(BF16)
    out = _bdot(o_scr[...], wo_ref[...])
    z = alpha * x + m[2:3] * out
    y_ref[0] = _layer_norm(z, g_ref[...], b_ref[...])


def _prompt_attn_layer(x, mod, sink, wqkv, wo, g, b, alpha):
    bsz, seq, _ = x.shape
    kern = functools.partial(_prompt_attn_kernel, alpha=alpha)
    return pl.pallas_call(
        kern,
        grid=(bsz,),
        in_specs=[
            pl.BlockSpec(memory_space=pltpu.SMEM),
            pl.BlockSpec((1, seq, D_MODEL), lambda i: (i, 0, 0)),
            pl.BlockSpec((1, MOD_ROWS, D_MODEL), lambda i: (0, 0, 0)),
            _const_spec((D_MODEL, QKV_WIDTH)),
            _const_spec((D_MODEL, D_MODEL)),
            _const_spec((1, D_MODEL)),
            _const_spec((1, D_MODEL)),
        ],
        out_specs=[
            pl.BlockSpec((1, seq, D_MODEL), lambda i: (i, 0, 0)),
            pl.BlockSpec((1, seq, KV_WIDTH), lambda i: (i, 0, 0)),
            pl.BlockSpec((1, seq, KV_WIDTH), lambda i: (i, 0, 0)),
        ],
        out_shape=[
            jax.ShapeDtypeStruct((bsz, seq, D_MODEL), F32),
            jax.ShapeDtypeStruct((bsz, seq, KV_WIDTH), F32),
            jax.ShapeDtypeStruct((bsz, seq, KV_WIDTH), F32),
        ],
        scratch_shapes=[pltpu.VMEM((seq, D_MODEL), BF16)],
        compiler_params=_params(1),
        name="prompt_attn",
    )(sink, x, mod, wqkv, wo, g, b)


def _rope_tables(n_tokens):
    rows = n_tokens // GRID_W
    row = jnp.repeat(jnp.arange(rows, dtype=F32), GRID_W)
    col = jnp.tile(jnp.arange(GRID_W, dtype=F32), rows)
    half = HEAD_DIM // 2
    inv = ROPE_THETA ** (-jnp.arange(0, half, 2, dtype=F32) / half)
    ang_r = row[:, None] * inv
    ang_c = col[:, None] * inv
    cos = jnp.concatenate([jnp.cos(ang_r)] * 2 + [jnp.cos(ang_c)] * 2, axis=-1)
    sin = jnp.concatenate([-jnp.sin(ang_r), jnp.sin(ang_r), -jnp.sin(ang_c), jnp.sin(ang_c)], axis=-1)
    return jnp.tile(cos, (1, LANES // HEAD_DIM)), jnp.tile(sin, (1, LANES // HEAD_DIM))


def _sample_qkv_kernel(x_ref, mod_ref, cos_ref, sin_ref, wqkv_ref, q_ref, k_ref, v_ref):
    x = x_ref[...]
    m = mod_ref[0]
    h = (x * (1.0 + m[1:2]) + m[0:1]).astype(BF16)
    qkv = _bdot(h, wqkv_ref[...])
    cos = cos_ref[...]
    sin = sin_ref[...]
    lane = lax.broadcasted_iota(jnp.int32, (1, LANES), 1)
    first = (lane % (HEAD_DIM // 2)) < (HEAD_DIM // 4)
    quarter = HEAD_DIM // 4
    for t in range(QK_WIDTH // LANES):
        tile = qkv[:, LANES * t:LANES * (t + 1)]
        partner = jnp.where(first, pltpu.roll(tile, LANES - quarter, axis=1),
                            pltpu.roll(tile, quarter, axis=1))
        rot = tile * cos + partner * sin
        if t < D_MODEL // LANES:
            q_ref[:, LANES * t:LANES * (t + 1)] = (rot * ATTN_SCALE).astype(BF16)
        else:
            tk = t - D_MODEL // LANES
            k_ref[:, LANES * tk:LANES * (tk + 1)] = rot.astype(BF16)
    v_ref[...] = qkv[:, QK_WIDTH:].astype(BF16)


def _sample_qkv(x, mod, seq, cos, sin, wqkv):
    m_rows = x.shape[0]
    tm = 512
    tiles_per_seq = seq // tm
    return pl.pallas_call(
        _sample_qkv_kernel,
        grid=(m_rows // tm,),
        in_specs=[
            pl.BlockSpec((tm, D_MODEL), lambda i: (i, 0)),
            pl.BlockSpec((1, MOD_ROWS, D_MODEL), lambda i: (i // tiles_per_seq, 0, 0)),
            pl.BlockSpec((tm, LANES), lambda i: (i % tiles_per_seq, 0)),
            pl.BlockSpec((tm, LANES), lambda i: (i % tiles_per_seq, 0)),
            _const_spec((D_MODEL, QKV_WIDTH)),
        ],
        out_specs=[
            pl.BlockSpec((tm, D_MODEL), lambda i: (i, 0)),
            pl.BlockSpec((tm, KV_WIDTH), lambda i: (i, 0)),
            pl.BlockSpec((tm, KV_WIDTH), lambda i: (i, 0)),
        ],
        out_shape=[
            jax.ShapeDtypeStruct((m_rows, D_MODEL), BF16),
            jax.ShapeDtypeStruct((m_rows, KV_WIDTH), BF16),
            jax.ShapeDtypeStruct((m_rows, KV_WIDTH), BF16),
        ],
        compiler_params=_params(1),
        name="sample_qkv",
    )(x, mod, cos, sin, wqkv)


def _sample_attn_kernel(sink_ref, x_ref, mod_ref, q_ref,
                        kp_ref, km_ref, kn_ref, vp_ref, vm_ref, vn_ref, ck_ref, cv_ref,
                        wo_ref, g_ref, b_ref, y_ref, o_scr, *, alpha, tq):
    i = pl.program_id(1)
    n_tiles = pl.num_programs(1)
    x = x_ref[...]
    m = mod_ref[0]
    k_all = jnp.concatenate([kp_ref[...].astype(F32), km_ref[...].astype(F32),
                             kn_ref[...].astype(F32), ck_ref[0]], axis=0)
    v_all = jnp.concatenate([vp_ref[...].astype(F32), vm_ref[...].astype(F32),
                             vn_ref[...].astype(F32), cv_ref[0]], axis=0)
    n_ctx = ck_ref.shape[1]
    n_local = tq + 2 * BLOCK
    qi = lax.broadcasted_iota(jnp.int32, (tq, n_local + n_ctx), 0)
    si = lax.broadcasted_iota(jnp.int32, (tq, n_local + n_ctx), 1)
    kpos = si - BLOCK
    in_window = jnp.abs(qi - kpos) <= BLOCK
    first_pos = jnp.where(i > 0, -BLOCK, 0)
    end_pos = jnp.where(i < n_tiles - 1, tq + BLOCK, tq)
    in_range = (kpos >= first_pos) & (kpos < end_pos)
    mask = (in_window & in_range) | (si >= n_local)
    for kv_head in range(N_KV_HEADS):
        k2 = _head_pair_operand(k_all, kv_head)
        v2 = _head_pair_operand(v_all, kv_head)
        for t in (2 * kv_head, 2 * kv_head + 1):
            o = _pair_attention(q_ref[:, LANES * t:LANES * (t + 1)], k2, v2,
                                sink_ref[2 * t], sink_ref[2 * t + 1], mask)
            o_scr[:, LANES * t:LANES * (t + 1)] = o.astype(BF16)
    out = _bdot(o_scr[...], wo_ref[...])
    z = alpha * x + m[2:3] * out
    y_ref[...] = _layer_norm(z, g_ref[...], b_ref[...])


def _sample_attn_layer(x, mod, seq, q, k, v, ctx_k, ctx_v, sink, wo, g, b, alpha):
    m_rows = x.shape[0]
    bsz = m_rows // seq
    tq = 2 * BLOCK
    tiles = seq // tq
    blocks = seq // BLOCK
    n_ctx = ctx_k.shape[1]
    kern = functools.partial(_sample_attn_kernel, alpha=alpha, tq=tq)
    row_tile = lambda bi, i: (bi * tiles + i, 0)
    prev_blk = lambda bi, i: (bi * blocks + jnp.maximum(2 * i - 1, 0), 0)
    next_blk = lambda bi, i: (bi * blocks + jnp.minimum(2 * i + 2, blocks - 1), 0)
    ctx_blk = lambda bi, i: (bi, 0, 0)
    return pl.pallas_call(
        kern,
        grid=(bsz, tiles),
        in_specs=[
            pl.BlockSpec(memory_space=pltpu.SMEM),
            pl.BlockSpec((tq, D_MODEL), row_tile),
            pl.BlockSpec((1, MOD_ROWS, D_MODEL), ctx_blk),
            pl.BlockSpec((tq, D_MODEL), row_tile),
            pl.BlockSpec((BLOCK, KV_WIDTH), prev_blk),
            pl.BlockSpec((tq, KV_WIDTH), row_tile),
            pl.BlockSpec((BLOCK, KV_WIDTH), next_blk),
            pl.BlockSpec((BLOCK, KV_WIDTH), prev_blk),
            pl.BlockSpec((tq, KV_WIDTH), row_tile),
            pl.BlockSpec((BLOCK, KV_WIDTH), next_blk),
            pl.BlockSpec((1, n_ctx, KV_WIDTH), ctx_blk),
            pl.BlockSpec((1, n_ctx, KV_WIDTH), ctx_blk),
            _const_spec((D_MODEL, D_MODEL)),
            _const_spec((1, D_MODEL)),
            _const_spec((1, D_MODEL)),
        ],
        out_specs=pl.BlockSpec((tq, D_MODEL), row_tile),
        out_shape=jax.ShapeDtypeStruct((m_rows, D_MODEL), F32),
        scratch_shapes=[pltpu.VMEM((tq, D_MODEL), BF16)],
        compiler_params=_params(2),
        name="sample_attn",
    )(sink, x, mod, q, k, k, k, v, v, v, ctx_k, ctx_v, wo, g, b)


def _conv_kernel(xp_ref, x_ref, xn_ref, mod_ref, pw1_ref, pb1_ref, dw_ref, db_ref,
                 cg_ref, cb_ref, pw2_ref, pb2_ref, g_ref, b_ref, y_ref, u_scr, c_scr, *,
                 alpha, tm, tiles_per_seq):
    i = pl.program_id(0)
    m = mod_ref[0]

    def glu(rows):
        h = (rows * (1.0 + m[1:2]) + m[0:1]).astype(BF16)
        a = _bdot(h, pw1_ref[...]) + pb1_ref[...]
        return a[:, :D_MODEL] * jax.nn.sigmoid(a[:, D_MODEL:])

    lane_tiles = D_MODEL // LANES

    def put_rows(rows, vals):
        for c in range(lane_tiles):
            u_scr[c, rows, :] = vals[:, c * LANES:(c + 1) * LANES]

    x = x_ref[...]
    put_rows(slice(HALO, HALO + tm), glu(x))
    if tiles_per_seq > 1:
        halo = glu(jnp.concatenate([xp_ref[...], xn_ref[...]], axis=0))
        t = i % tiles_per_seq
        put_rows(slice(0, HALO), jnp.where(t > 0, halo[:HALO], 0.0))
        put_rows(slice(HALO + tm, 2 * HALO + tm), jnp.where(t < tiles_per_seq - 1, halo[HALO:], 0.0))
    else:
        put_rows(slice(0, HALO), jnp.zeros((HALO, D_MODEL), F32))
        put_rows(slice(HALO + tm, 2 * HALO + tm), jnp.zeros((HALO, D_MODEL), F32))
    row_blocks = tm // CONV_ROW_BLOCK
    spans = CONV_ROW_BLOCK // (SUBLANES * CONV_ROW_STRIDE)

    def conv_block(idx, carry):
        c = idx // row_blocks
        base = (idx % row_blocks) * CONV_ROW_BLOCK
        starts = [base + s * SUBLANES * CONV_ROW_STRIDE + j
                  for s in range(spans) for j in range(CONV_ROW_STRIDE)]
        bias = db_ref[c]
        accs = [jnp.zeros((SUBLANES, LANES), F32) + bias for _ in starts]
        for tap in range(CONV_WIDTH):
            w = dw_ref[c, tap:tap + 1, :]
            off = HALO - CONV_PAD + tap
            for n, start in enumerate(starts):
                rows = pl.ds(start + off, SUBLANES, stride=CONV_ROW_STRIDE)
                accs[n] = accs[n] + w * u_scr[c, rows, :]
        for n, start in enumerate(starts):
            c_scr[c, pl.ds(start, SUBLANES, stride=CONV_ROW_STRIDE), :] = accs[n]
        return carry

    lax.fori_loop(0, lane_tiles * row_blocks, conv_block, 0)
    conv = jnp.concatenate([c_scr[c] for c in range(lane_tiles)], axis=1)
    un = _layer_norm(conv, cg_ref[...], cb_ref[...])
    un = un * jax.nn.sigmoid(un)
    out = _bdot(un.astype(BF16), pw2_ref[...]) + pb2_ref[...]
    z = alpha * x + m[2:3] * out
    y_ref[...] = _layer_norm(z, g_ref[...], b_ref[...])


def _conv_layer(x, mod, rows_per_group, seq, pw1, pb1, dw, db, cg, cb, pw2, pb2, g, b, alpha):
    m_rows = x.shape[0]
    tm = min(seq, 512)
    tiles_per_seq = seq // tm
    halo_per_tile = tm // HALO
    n_halo_blocks = m_rows // HALO
    kern = functools.partial(_conv_kernel, alpha=alpha, tm=tm, tiles_per_seq=tiles_per_seq)
    dw = dw.reshape(CONV_WIDTH, D_MODEL // LANES, LANES).transpose(1, 0, 2)
    db = db.reshape(D_MODEL // LANES, 1, LANES)
    prev_blk = lambda i: (jnp.maximum(i * halo_per_tile - 1, 0), 0)
    next_blk = lambda i: (jnp.minimum((i + 1) * halo_per_tile, n_halo_blocks - 1), 0)
    return pl.pallas_call(
        kern,
        grid=(m_rows // tm,),
        in_specs=[
            pl.BlockSpec((HALO, D_MODEL), prev_blk),
            pl.BlockSpec((tm, D_MODEL), lambda i: (i, 0)),
            pl.BlockSpec((HALO, D_MODEL), next_blk),
            pl.BlockSpec((1, MOD_ROWS, D_MODEL), lambda i: (i * tm // rows_per_group, 0, 0)),
            _const_spec((D_MODEL, 2 * D_MODEL)),
            _const_spec((1, 2 * D_MODEL)),
            _const_spec((D_MODEL // LANES, CONV_WIDTH, LANES)),
            _const_spec((D_MODEL // LANES, 1, LANES)),
            _const_spec((1, D_MODEL)),
            _const_spec((1, D_MODEL)),
            _const_spec((D_MODEL, D_MODEL)),
            _const_spec((1, D_MODEL)),
            _const_spec((1, D_MODEL)),
            _const_spec((1, D_MODEL)),
        ],
        out_specs=pl.BlockSpec((tm, D_MODEL), lambda i: (i, 0)),
        out_shape=jax.ShapeDtypeStruct((m_rows, D_MODEL), F32),
        scratch_shapes=[pltpu.VMEM((D_MODEL // LANES, tm + 2 * HALO, LANES), F32),
                        pltpu.VMEM((D_MODEL // LANES, tm, LANES), F32)],
        compiler_params=_params(1),
        name="conv",
    )(x, x, x, mod, pw1, pb1, dw, db, cg, cb, pw2, pb2, g, b)


def kernel(x_prompt, x_sample, cache_k, cache_v, c, c_ctx, ada_w, ada_b, attn_w_qkv, attn_w_o, attn_sink, conv_pw1_w, conv_pw1_b, conv_dw_w, conv_dw_b, conv_norm_g, conv_norm_b, conv_pw2_w, conv_pw2_b, ln1_g, ln1_b, mlp_w1, mlp_b1, mlp_w2, mlp_b2, ln2_g, ln2_b):
    depth = ada_w.shape[0]
    bsz, seq, _ = x_prompt.shape
    dec_bsz, dec_seq, _ = x_sample.shape
    n_ctx = cache_k.shape[2]
    alpha = (2.0 * depth) ** 0.25
    assert 1 + dec_bsz <= COND_ROWS

    row = lambda a: a.reshape(1, -1)

    cond = jnp.concatenate([c_ctx[None], c, jnp.zeros((COND_ROWS - 1 - dec_bsz, D_MODEL), F32)], axis=0)
    mods = _modulation(cond, ada_w, ada_b).reshape(depth, COND_ROWS, N_MOD, D_MODEL)
    mods = jnp.pad(mods, ((0, 0), (0, 0), (0, MOD_ROWS - N_MOD), (0, 0)))

    cos, sin = _rope_tables(dec_seq)

    yp = x_prompt
    ys = x_sample.reshape(dec_bsz * dec_seq, D_MODEL)
    new_k, new_v = [], []
    for i in range(depth):
        mod_p = mods[i, 0:1]
        mod_s = mods[i, 1:1 + dec_bsz]
        j = i // N_MIXERS
        if i % N_MIXERS == 0:
            wqkv = attn_w_qkv[j].astype(BF16)
            wo = attn_w_o[j].astype(BF16)
            yp, kp, vp = _prompt_attn_layer(yp, mod_p, attn_sink[j], wqkv, wo,
                                            row(ln1_g[i]), row(ln1_b[i]), alpha)
            new_k.append(kp.reshape(bsz, seq, N_KV_HEADS, HEAD_DIM))
            new_v.append(vp.reshape(bsz, seq, N_KV_HEADS, HEAD_DIM))
            q, k, v = _sample_qkv(ys, mod_s, dec_seq, cos, sin, wqkv)
            ys = _sample_attn_layer(ys, mod_s, dec_seq, q, k, v,
                                    cache_k[:, j].reshape(dec_bsz, n_ctx, KV_WIDTH),
                                    cache_v[:, j].reshape(dec_bsz, n_ctx, KV_WIDTH),
                                    attn_sink[j], wo, row(ln1_g[i]), row(ln1_b[i]), alpha)
        else:
            cw = (conv_pw1_w[j].astype(BF16), row(conv_pw1_b[j]), conv_dw_w[j], row(conv_dw_b[j]),
                  row(conv_norm_g[j]), row(conv_norm_b[j]), conv_pw2_w[j].astype(BF16),
                  row(conv_pw2_b[j]), row(ln1_g[i]), row(ln1_b[i]))
            yp = _conv_layer(yp.reshape(bsz * seq, D_MODEL), mod_p, bsz * seq, seq, *cw,
                             alpha).reshape(bsz, seq, D_MODEL)
            ys = _conv_layer(ys, mod_s, dec_seq, dec_seq, *cw, alpha)
        mw = (mlp_w1[i].astype(BF16), row(mlp_b1[i]), mlp_w2[i].astype(BF16), row(mlp_b2[i]),
              row(ln2_g[i]), row(ln2_b[i]))
        yp = _mlp(yp.reshape(bsz * seq, D_MODEL), mod_p, bsz * seq, *mw, alpha).reshape(bsz, seq, D_MODEL)
        ys = _mlp(ys, mod_s, dec_seq, *mw, alpha)
    new_cache_k = jnp.stack(new_k, axis=1)
    new_cache_v = jnp.stack(new_v, axis=1)
    return (yp, ys.reshape(dec_bsz, dec_seq, D_MODEL), new_cache_k, new_cache_v)
```

```python
import functools
import math

import jax
import jax.numpy as jnp
from jax import lax
from jax.experimental import pallas as pl
from jax.experimental.pallas import tpu as pltpu

D_MODEL = 1024
GRID_W = 64
N_MIXERS = 2
HEAD_DIM = 64
N_HEADS = D_MODEL // HEAD_DIM
N_KV_HEADS = 4
GROUP = N_HEADS // N_KV_HEADS
KV_WIDTH = N_KV_HEADS * HEAD_DIM
QK_WIDTH = D_MODEL + KV_WIDTH
QKV_WIDTH = D_MODEL + 2 * KV_WIDTH
BLOCK = 128
ROPE_THETA = 10000.0
CONV_WIDTH = 31
CONV_PAD = (CONV_WIDTH - 1) // 2
D_FF = 4 * D_MODEL
N_MOD = 6
LN_EPS = 1e-5
NEG_INF = -1e30
ATTN_SCALE = HEAD_DIM ** -0.5
LOG2E = math.log2(math.e)
Q_SCALE = ATTN_SCALE * LOG2E

LANES = 128
SUBLANES = 8
VMEM_LIMIT_BYTES = 56 * 1024 * 1024

MOD_ROWS = 8
COND_ROWS = 8
HALO = 16
CONV_ROW_STRIDE = 4
CONV_ROW_BLOCK = 128
CONV_TILE_ROWS = 512
CONV_SUB_ROWS = 256
SOFTMAX_ROWS = 64
MLP_SUB_ROWS = 256

BF16 = jnp.bfloat16
F32 = jnp.float32


def _const_spec(shape):
    n = len(shape)
    return pl.BlockSpec(shape, lambda *_: (0,) * n, pipeline_mode=pl.Buffered(1))


def _layer_spec(shape, layer):
    n = len(shape)
    return pl.BlockSpec((None,) + tuple(shape[1:]), lambda *_: (layer,) + (0,) * (n - 1),
                        pipeline_mode=pl.Buffered(1))


def _params(n_axes):
    return pltpu.CompilerParams(dimension_semantics=("arbitrary",) * n_axes,
                                vmem_limit_bytes=VMEM_LIMIT_BYTES)


def _layer_norm(z, g, b):
    mu = jnp.mean(z, axis=-1, keepdims=True)
    zc = z - mu
    var = jnp.mean(zc * zc, axis=-1, keepdims=True)
    return zc * lax.rsqrt(var + LN_EPS) * g + b


def _bdot(a, b):
    return jnp.dot(a, b, preferred_element_type=F32)


def _bdot_nt(a, b):
    return lax.dot_general(a, b, (((1,), (1,)), ((), ())), preferred_element_type=F32)


def _mod_kernel(cond_ref, w_ref, b_ref, out_ref):
    c = cond_ref[...]
    s = c * jax.nn.sigmoid(c)
    out_ref[0] = jnp.dot(s, w_ref[0], preferred_element_type=F32,
                         precision=lax.Precision.HIGHEST) + b_ref[0]


def _modulation(cond, ada_w, ada_b):
    depth = ada_w.shape[0]
    tn = 1536
    width = N_MOD * D_MODEL
    return pl.pallas_call(
        _mod_kernel,
        grid=(depth, width // tn),
        in_specs=[
            pl.BlockSpec((COND_ROWS, D_MODEL), lambda l, n: (0, 0)),
            pl.BlockSpec((1, D_MODEL, tn), lambda l, n: (l, 0, n)),
            pl.BlockSpec((1, 1, tn), lambda l, n: (l, 0, n)),
        ],
        out_specs=pl.BlockSpec((1, COND_ROWS, tn), lambda l, n: (l, 0, n)),
        out_shape=jax.ShapeDtypeStruct((depth, COND_ROWS, width), F32),
        compiler_params=_params(2),
        name="modulation",
    )(cond, ada_w, ada_b.reshape(depth, 1, width))


def _mlp_kernel(y_ref, mod_ref, w1_ref, b1_ref, w2_ref, b2_ref, g_ref, b_ref, out_ref, *,
                alpha, ff_chunk):
    m = mod_ref[0]
    for r0 in range(0, y_ref.shape[0], MLP_SUB_ROWS):
        rows = slice(r0, r0 + MLP_SUB_ROWS)
        y = y_ref[rows, :]
        h = (y * (1.0 + m[4:5]) + m[3:4]).astype(BF16)
        acc = jnp.zeros(y.shape, F32)
        for c in range(D_FF // ff_chunk):
            cols = slice(c * ff_chunk, (c + 1) * ff_chunk)
            a = _bdot(h, w1_ref[:, cols]) + b1_ref[:, cols]
            a = jnp.maximum(a, 0.0)
            acc = acc + _bdot((a * a).astype(BF16), w2_ref[cols, :])
        z = alpha * y + m[5:6] * (acc + b2_ref[...])
        out_ref[rows, :] = _layer_norm(z, g_ref[...], b_ref[...])


def _mlp(y, mod, rows_per_group, layer, w1, b1, w2, b2, g, b, alpha):
    m_rows = y.shape[0]
    tm = 512
    kern = functools.partial(_mlp_kernel, alpha=alpha, ff_chunk=1024)
    return pl.pallas_call(
        kern,
        grid=(m_rows // tm,),
        in_specs=[
            pl.BlockSpec((tm, D_MODEL), lambda i: (i, 0)),
            pl.BlockSpec((1, MOD_ROWS, D_MODEL), lambda i: (i * tm // rows_per_group, 0, 0)),
            _layer_spec(w1.shape, layer),
            _layer_spec(b1.shape, layer),
            _layer_spec(w2.shape, layer),
            _layer_spec(b2.shape, layer),
            _layer_spec(g.shape, layer),
            _layer_spec(b.shape, layer),
        ],
        out_specs=pl.BlockSpec((tm, D_MODEL), lambda i: (i, 0)),
        out_shape=jax.ShapeDtypeStruct((m_rows, D_MODEL), F32),
        compiler_params=_params(1),
        name="mlp",
    )(y, mod, w1, b1, w2, b2, g, b)


def _head_pair_operand(x, kv_head):
    tile = x[:, LANES * (kv_head // 2):LANES * (kv_head // 2 + 1)]
    swapped = pltpu.roll(tile, HEAD_DIM, axis=1)
    lane = lax.broadcasted_iota(jnp.int32, (1, LANES), 1)
    low = lane < HEAD_DIM
    if kv_head % 2 == 0:
        lo_src, hi_src = tile, swapped
    else:
        lo_src, hi_src = swapped, tile
    lo = jnp.where(low, lo_src, 0.0).astype(BF16)
    hi = jnp.where(low, 0.0, hi_src).astype(BF16)
    pieces = []
    for blk in range(x.shape[0] // BLOCK):
        pieces += [lo[blk * BLOCK:(blk + 1) * BLOCK], hi[blk * BLOCK:(blk + 1) * BLOCK]]
    return jnp.concatenate(pieces, axis=0)


def _pair_softmax(s_ref, p_ref, stat_ref, sinks, row_plan):
    def blocks(first_block, n_blocks):
        return range(first_block, first_block + n_blocks)

    def load(rows, block, head, plan_row, biases):
        tile = 2 * block + head
        v = s_ref[rows, tile * LANES:(tile + 1) * LANES]
        if block in biases:
            v = v + biases[block][plan_row:plan_row + SOFTMAX_ROWS, :]
        return v

    for row0, n_rows, first_block, n_blocks, biases in row_plan:
        for r in range(0, n_rows, SOFTMAX_ROWS):
            rows = slice(row0 + r, row0 + r + SOFTMAX_ROWS)
            for head in range(2):
                acc = None
                for block in blocks(first_block, n_blocks):
                    v = load(rows, block, head, r, biases)
                    acc = v if acc is None else jnp.maximum(acc, v)
                stat_ref[0, rows, head * LANES:(head + 1) * LANES] = acc
    mx = [jnp.maximum(jnp.max(stat_ref[0, :, head * LANES:(head + 1) * LANES], axis=-1, keepdims=True),
                      sinks[head]) for head in range(2)]
    for row0, n_rows, first_block, n_blocks, biases in row_plan:
        for r in range(0, n_rows, SOFTMAX_ROWS):
            rows = slice(row0 + r, row0 + r + SOFTMAX_ROWS)
            for head in range(2):
                m_rows = mx[head][row0 + r:row0 + r + SOFTMAX_ROWS]
                acc = None
                for block in blocks(first_block, n_blocks):
                    p = jnp.exp2(load(rows, block, head, r, biases) - m_rows)
                    acc = p if acc is None else acc + p
                    tile = 2 * block + head
                    p_ref[rows, tile * LANES:(tile + 1) * LANES] = p.astype(BF16)
                stat_ref[1, rows, head * LANES:(head + 1) * LANES] = acc
    recip = [1.0 / (jnp.sum(stat_ref[1, :, head * LANES:(head + 1) * LANES], axis=-1, keepdims=True)
                    + jnp.exp2(sinks[head] - mx[head])) for head in range(2)]
    lane = lax.broadcasted_iota(jnp.int32, (1, LANES), 1)
    return jnp.where(lane < HEAD_DIM, recip[0], recip[1])


def _attend_all(q_tile, k_src, v_src, sink_ref, o_scr, s_scr, p_scr, stat_scr, row_plan):
    n_tiles = D_MODEL // LANES
    operands = {}

    def kv_operands(t):
        kv_head = t // (GROUP // 2)
        if kv_head not in operands:
            operands[kv_head] = (_head_pair_operand(k_src, kv_head), _head_pair_operand(v_src, kv_head))
        return operands[kv_head]

    def scores(t):
        s_scr[t % 2] = _bdot_nt(q_tile(t), kv_operands(t)[0])

    def values(t, recip):
        o = _bdot(p_scr[t % 2], kv_operands(t)[1]) * recip
        o_scr[:, LANES * t:LANES * (t + 1)] = o.astype(BF16)

    scores(0)
    recip_prev = None
    for t in range(n_tiles):
        if t + 1 < n_tiles:
            scores(t + 1)
        if t > 0:
            values(t - 1, recip_prev)
        sinks = (sink_ref[2 * t] * LOG2E, sink_ref[2 * t + 1] * LOG2E)
        recip_prev = _pair_softmax(s_scr.at[t % 2], p_scr.at[t % 2], stat_scr.at[t % 2], sinks, row_plan)
    values(n_tiles - 1, recip_prev)


def _prompt_attn_kernel(sink_ref, x_ref, mod_ref, wqkv_ref, wo_ref, g_ref, b_ref,
                        y_ref, k_ref, v_ref, o_scr, s_scr, p_scr, stat_scr, *, alpha):
    x = x_ref[0]
    seq = x.shape[0]
    m = mod_ref[0]
    h = (x * (1.0 + m[1:2]) + m[0:1]).astype(BF16)
    qkv = _bdot(h, wqkv_ref[...])
    k = qkv[:, D_MODEL:QK_WIDTH]
    v = qkv[:, QK_WIDTH:]
    k_ref[0] = k
    v_ref[0] = v
    q = (qkv[:, :D_MODEL] * Q_SCALE).astype(BF16)
    row_plan = [(0, seq, 0, seq // BLOCK, {})]
    _attend_all(lambda t: q[:, LANES * t:LANES * (t + 1)], k, v, sink_ref,
                o_scr, s_scr, p_scr, stat_scr, row_plan)
    out = _bdot(o_scr[...], wo_ref[...])
    z = alpha * x + m[2:3] * out
    y_ref[0] = _layer_norm(z, g_ref[...], b_ref[...])


def _attn_scratch(rows, n_key_blocks):
    width = 2 * n_key_blocks * LANES
    return [pltpu.VMEM((rows, D_MODEL), BF16),
            pltpu.VMEM((2, rows, width), F32),
            pltpu.VMEM((2, rows, width), BF16),
            pltpu.VMEM((2, 2, rows, 2 * LANES), F32)]


def _prompt_attn_layer(x, mod, sink, layer, attn_layer, wqkv, wo, g, b, alpha):
    bsz, seq, _ = x.shape
    kern = functools.partial(_prompt_attn_kernel, alpha=alpha)
    return pl.pallas_call(
        kern,
        grid=(bsz,),
        in_specs=[
            pl.BlockSpec(memory_space=pltpu.SMEM),
            pl.BlockSpec((1, seq, D_MODEL), lambda i: (i, 0, 0)),
            pl.BlockSpec((1, MOD_ROWS, D_MODEL), lambda i: (0, 0, 0)),
            _layer_spec(wqkv.shape, attn_layer),
            _layer_spec(wo.shape, attn_layer),
            _layer_spec(g.shape, layer),
            _layer_spec(b.shape, layer),
        ],
        out_specs=[
            pl.BlockSpec((1, seq, D_MODEL), lambda i: (i, 0, 0)),
            pl.BlockSpec((1, seq, KV_WIDTH), lambda i: (i, 0, 0)),
            pl.BlockSpec((1, seq, KV_WIDTH), lambda i: (i, 0, 0)),
        ],
        out_shape=[
            jax.ShapeDtypeStruct((bsz, seq, D_MODEL), F32),
            jax.ShapeDtypeStruct((bsz, seq, KV_WIDTH), F32),
            jax.ShapeDtypeStruct((bsz, seq, KV_WIDTH), F32),
        ],
        scratch_shapes=_attn_scratch(seq, seq // BLOCK),
        compiler_params=_params(1),
        name="prompt_attn",
    )(sink, x, mod, wqkv, wo, g, b)


def _rope_tables(n_tokens):
    rows = n_tokens // GRID_W
    row = jnp.repeat(jnp.arange(rows, dtype=F32), GRID_W)
    col = jnp.tile(jnp.arange(GRID_W, dtype=F32), rows)
    half = HEAD_DIM // 2
    inv = ROPE_THETA ** (-jnp.arange(0, half, 2, dtype=F32) / half)
    ang_r = row[:, None] * inv
    ang_c = col[:, None] * inv
    cos = jnp.concatenate([jnp.cos(ang_r)] * 2 + [jnp.cos(ang_c)] * 2, axis=-1)
    sin = jnp.concatenate([-jnp.sin(ang_r), jnp.sin(ang_r), -jnp.sin(ang_c), jnp.sin(ang_c)], axis=-1)
    return jnp.tile(cos, (1, LANES // HEAD_DIM)), jnp.tile(sin, (1, LANES // HEAD_DIM))


def _sample_qkv_kernel(x_ref, mod_ref, cos_ref, sin_ref, wqkv_ref, q_ref, k_ref, v_ref):
    x = x_ref[...]
    m = mod_ref[0]
    h = (x * (1.0 + m[1:2]) + m[0:1]).astype(BF16)
    qkv = _bdot(h, wqkv_ref[...])
    cos = cos_ref[...]
    sin = sin_ref[...]
    lane = lax.broadcasted_iota(jnp.int32, (1, LANES), 1)
    first = (lane % (HEAD_DIM // 2)) < (HEAD_DIM // 4)
    quarter = HEAD_DIM // 4
    for t in range(QK_WIDTH // LANES):
        tile = qkv[:, LANES * t:LANES * (t + 1)]
        partner = jnp.where(first, pltpu.roll(tile, LANES - quarter, axis=1),
                            pltpu.roll(tile, quarter, axis=1))
        rot = tile * cos + partner * sin
        if t < D_MODEL // LANES:
            q_ref[:, LANES * t:LANES * (t + 1)] = (rot * Q_SCALE).astype(BF16)
        else:
            tk = t - D_MODEL // LANES
            k_ref[:, LANES * tk:LANES * (tk + 1)] = rot.astype(BF16)
    v_ref[...] = qkv[:, QK_WIDTH:].astype(BF16)


def _sample_qkv(x, mod, seq, cos, sin, attn_layer, wqkv):
    m_rows = x.shape[0]
    tm = 512
    tiles_per_seq = seq // tm
    return pl.pallas_call(
        _sample_qkv_kernel,
        grid=(m_rows // tm,),
        in_specs=[
            pl.BlockSpec((tm, D_MODEL), lambda i: (i, 0)),
            pl.BlockSpec((1, MOD_ROWS, D_MODEL), lambda i: (i // tiles_per_seq, 0, 0)),
            pl.BlockSpec((tm, LANES), lambda i: (i % tiles_per_seq, 0)),
            pl.BlockSpec((tm, LANES), lambda i: (i % tiles_per_seq, 0)),
            _layer_spec(wqkv.shape, attn_layer),
        ],
        out_specs=[
            pl.BlockSpec((tm, D_MODEL), lambda i: (i, 0)),
            pl.BlockSpec((tm, KV_WIDTH), lambda i: (i, 0)),
            pl.BlockSpec((tm, KV_WIDTH), lambda i: (i, 0)),
        ],
        out_shape=[
            jax.ShapeDtypeStruct((m_rows, D_MODEL), BF16),
            jax.ShapeDtypeStruct((m_rows, KV_WIDTH), BF16),
            jax.ShapeDtypeStruct((m_rows, KV_WIDTH), BF16),
        ],
        compiler_params=_params(1),
        name="sample_qkv",
    )(x, mod, cos, sin, wqkv)


def _sample_attn_kernel(sink_ref, x_ref, mod_ref, q_ref,
                        kp_ref, km_ref, kn_ref, vp_ref, vm_ref, vn_ref, ck_ref, cv_ref,
                        wo_ref, g_ref, b_ref, y_ref, o_scr, s_scr, p_scr, stat_scr, bias_scr, *,
                        alpha):
    i = pl.program_id(1)
    n_tiles = pl.num_programs(1)
    x = x_ref[...]
    m = mod_ref[0]
    ctx_blocks = ck_ref.shape[1] // BLOCK
    k_all = jnp.concatenate([kp_ref[...].astype(F32), km_ref[...].astype(F32), ck_ref[0],
                             kn_ref[...].astype(F32)], axis=0)
    v_all = jnp.concatenate([vp_ref[...].astype(F32), vm_ref[...].astype(F32), cv_ref[0],
                             vn_ref[...].astype(F32)], axis=0)
    next_block = 3 + ctx_blocks
    qi = lax.broadcasted_iota(jnp.int32, (BLOCK, BLOCK), 0)
    ki = lax.broadcasted_iota(jnp.int32, (BLOCK, BLOCK), 1)
    bias_scr[0] = jnp.where(ki >= qi, jnp.where(i > 0, 0.0, NEG_INF), NEG_INF)
    bias_scr[1] = jnp.where(ki <= qi, 0.0, NEG_INF)
    bias_scr[2] = jnp.where(ki >= qi, 0.0, NEG_INF)
    bias_scr[3] = jnp.where(ki <= qi, jnp.where(i < n_tiles - 1, 0.0, NEG_INF), NEG_INF)
    row_plan = [(0, BLOCK, 0, next_block, {0: bias_scr.at[0], 2: bias_scr.at[1]}),
                (BLOCK, BLOCK, 1, next_block, {1: bias_scr.at[2], next_block: bias_scr.at[3]})]
    for slot in range(2):
        p_scr[slot, 0:BLOCK, 2 * next_block * LANES:] = jnp.zeros((BLOCK, 2 * LANES), BF16)
        p_scr[slot, BLOCK:, 0:2 * LANES] = jnp.zeros((BLOCK, 2 * LANES), BF16)
    _attend_all(lambda t: q_ref[:, LANES * t:LANES * (t + 1)], k_all, v_all, sink_ref,
                o_scr, s_scr, p_scr, stat_scr, row_plan)
    out = _bdot(o_scr[...], wo_ref[...])
    z = alpha * x + m[2:3] * out
    y_ref[...] = _layer_norm(z, g_ref[...], b_ref[...])


def _sample_attn_layer(x, mod, seq, q, k, v, ctx_k, ctx_v, sink, layer, attn_layer, wo, g, b, alpha):
    m_rows = x.shape[0]
    bsz = m_rows // seq
    tq = 2 * BLOCK
    tiles = seq // tq
    blocks = seq // BLOCK
    n_ctx = ctx_k.shape[1]
    assert n_ctx % BLOCK == 0
    kern = functools.partial(_sample_attn_kernel, alpha=alpha)
    row_tile = lambda bi, i: (bi * tiles + i, 0)
    prev_blk = lambda bi, i: (bi * blocks + jnp.maximum(2 * i - 1, 0), 0)
    next_blk = lambda bi, i: (bi * blocks + jnp.minimum(2 * i + 2, blocks - 1), 0)
    ctx_blk = lambda bi, i: (bi, 0, 0)
    return pl.pallas_call(
        kern,
        grid=(bsz, tiles),
        in_specs=[
            pl.BlockSpec(memory_space=pltpu.SMEM),
            pl.BlockSpec((tq, D_MODEL), row_tile),
            pl.BlockSpec((1, MOD_ROWS, D_MODEL), ctx_blk),
            pl.BlockSpec((tq, D_MODEL), row_tile),
            pl.BlockSpec((BLOCK, KV_WIDTH), prev_blk),
            pl.BlockSpec((tq, KV_WIDTH), row_tile),
            pl.BlockSpec((BLOCK, KV_WIDTH), next_blk),
            pl.BlockSpec((BLOCK, KV_WIDTH), prev_blk),
            pl.BlockSpec((tq, KV_WIDTH), row_tile),
            pl.BlockSpec((BLOCK, KV_WIDTH), next_blk),
            pl.BlockSpec((1, n_ctx, KV_WIDTH), ctx_blk),
            pl.BlockSpec((1, n_ctx, KV_WIDTH), ctx_blk),
            _layer_spec(wo.shape, attn_layer),
            _layer_spec(g.shape, layer),
            _layer_spec(b.shape, layer),
        ],
        out_specs=pl.BlockSpec((tq, D_MODEL), row_tile),
        out_shape=jax.ShapeDtypeStruct((m_rows, D_MODEL), F32),
        scratch_shapes=_attn_scratch(tq, 4 + n_ctx // BLOCK) + [pltpu.VMEM((4, BLOCK, BLOCK), F32)],
        compiler_params=_params(2),
        name="sample_attn",
    )(sink, x, mod, q, k, k, k, v, v, v, ctx_k, ctx_v, wo, g, b)


def _conv_kernel(xp_ref, x_ref, xn_ref, mod_ref, pw1_ref, pb1_ref, dw_ref, db_ref,
                 cg_ref, cb_ref, pw2_ref, pb2_ref, g_ref, b_ref, y_ref, *scratch,
                 alpha, tm, sub, tiles_per_seq, seq_per_sub):
    i = pl.program_id(0)
    t = i % tiles_per_seq
    m = mod_ref[0]
    lane_tiles = D_MODEL // LANES
    n_sub = tm // sub
    h_scrs, u_scrs, c_scrs = scratch[:n_sub], scratch[n_sub:2 * n_sub], scratch[2 * n_sub:]
    recompute_halo = not seq_per_sub

    def modulated_input(s):
        r0 = s * sub
        rows = x_ref[r0:r0 + sub, :]
        if recompute_halo:
            top = xp_ref[...] if s == 0 else x_ref[r0 - HALO:r0, :]
            bot = xn_ref[...] if s == n_sub - 1 else x_ref[r0 + sub:r0 + sub + HALO, :]
            rows = jnp.concatenate([top, rows, bot], axis=0)
        h_scrs[s][...] = (rows * (1.0 + m[1:2]) + m[0:1]).astype(BF16)

    def glu_chunk(s, c):
        a = _bdot(h_scrs[s][...], pw1_ref[c]) + pb1_ref[c]
        u = a[:, :LANES] * jax.nn.sigmoid(a[:, LANES:])
        zeros = jnp.zeros((HALO, LANES), F32)
        if recompute_halo:
            u_scrs[s][c, :, :] = u
            if s == 0:
                u_scrs[s][c, 0:HALO, :] = jnp.where(t > 0, u[:HALO], 0.0)
            if s == n_sub - 1:
                u_scrs[s][c, HALO + sub:, :] = jnp.where(t < tiles_per_seq - 1, u[HALO + sub:], 0.0)
        else:
            u_scrs[s][c, 0:HALO, :] = zeros
            u_scrs[s][c, HALO:HALO + sub, :] = u
            u_scrs[s][c, HALO + sub:, :] = zeros

    def conv_taps(s, c):
        span_rows = SUBLANES * CONV_ROW_STRIDE
        for base in range(0, sub, CONV_ROW_BLOCK):
            starts = [base + sp + j for sp in range(0, CONV_ROW_BLOCK, span_rows)
                      for j in range(CONV_ROW_STRIDE)]
            bias = db_ref[c]
            accs = [jnp.zeros((SUBLANES, LANES), F32) + bias for _ in starts]
            for tap in range(CONV_WIDTH):
                w = dw_ref[c, tap:tap + 1, :]
                off = HALO - CONV_PAD + tap
                for n, start in enumerate(starts):
                    rows = pl.ds(start + off, SUBLANES, stride=CONV_ROW_STRIDE)
                    accs[n] = accs[n] + w * u_scrs[s][c, rows, :]
            for n, start in enumerate(starts):
                c_scrs[s][c, pl.ds(start, SUBLANES, stride=CONV_ROW_STRIDE), :] = accs[n]

    def finish(s):
        r0 = s * sub
        x = x_ref[r0:r0 + sub, :]
        conv = jnp.concatenate([c_scrs[s][c] for c in range(lane_tiles)], axis=1)
        un = _layer_norm(conv, cg_ref[...], cb_ref[...])
        un = un * jax.nn.sigmoid(un)
        out = _bdot(un.astype(BF16), pw2_ref[...]) + pb2_ref[...]
        z = alpha * x + m[2:3] * out
        y_ref[r0:r0 + sub, :] = _layer_norm(z, g_ref[...], b_ref[...])

    for s in range(n_sub):
        modulated_input(s)
    for c in range(lane_tiles):
        glu_chunk(0, c)
    for s in range(n_sub):
        for c in range(lane_tiles):
            conv_taps(s, c)
            if s + 1 < n_sub:
                glu_chunk(s + 1, c)
        finish(s)


def _conv_layer(x, mod, rows_per_group, seq, layer, conv_layer, pw1, pb1, dw, db, cg, cb, pw2, pb2,
                g, b, alpha):
    m_rows = x.shape[0]
    tm = CONV_TILE_ROWS
    sub = CONV_SUB_ROWS
    seq_per_sub = seq == sub
    assert seq_per_sub or seq % tm == 0
    tiles_per_seq = max(seq // tm, 1)
    in_rows = sub if seq_per_sub else sub + 2 * HALO
    halo_per_tile = tm // HALO
    n_halo_blocks = m_rows // HALO
    kern = functools.partial(_conv_kernel, alpha=alpha, tm=tm, sub=sub, tiles_per_seq=tiles_per_seq,
                             seq_per_sub=seq_per_sub)
    prev_blk = lambda i: (jnp.maximum(i * halo_per_tile - 1, 0), 0)
    next_blk = lambda i: (jnp.minimum((i + 1) * halo_per_tile, n_halo_blocks - 1), 0)
    return pl.pallas_call(
        kern,
        grid=(m_rows // tm,),
        in_specs=[
            pl.BlockSpec((HALO, D_MODEL), prev_blk),
            pl.BlockSpec((tm, D_MODEL), lambda i: (i, 0)),
            pl.BlockSpec((HALO, D_MODEL), next_blk),
            pl.BlockSpec((1, MOD_ROWS, D_MODEL), lambda i: (i * tm // rows_per_group, 0, 0)),
            _layer_spec(pw1.shape, conv_layer),
            _layer_spec(pb1.shape, conv_layer),
            _layer_spec(dw.shape, conv_layer),
            _layer_spec(db.shape, conv_layer),
            _layer_spec(cg.shape, conv_layer),
            _layer_spec(cb.shape, conv_layer),
            _layer_spec(pw2.shape, conv_layer),
            _layer_spec(pb2.shape, conv_layer),
            _layer_spec(g.shape, layer),
            _layer_spec(b.shape, layer),
        ],
        out_specs=pl.BlockSpec((tm, D_MODEL), lambda i: (i, 0)),
        out_shape=jax.ShapeDtypeStruct((m_rows, D_MODEL), F32),
        scratch_shapes=([pltpu.VMEM((in_rows, D_MODEL), BF16)] * (tm // sub)
                        + [pltpu.VMEM((D_MODEL // LANES, sub + 2 * HALO, LANES), F32)] * (tm // sub)
                        + [pltpu.VMEM((D_MODEL // LANES, sub, LANES), F32)] * (tm // sub)),
        compiler_params=_params(1),
        name="conv",
    )(x, x, x, mod, pw1, pb1, dw, db, cg, cb, pw2, pb2, g, b)


def kernel(x_prompt, x_sample, cache_k, cache_v, c, c_ctx, ada_w, ada_b, attn_w_qkv, attn_w_o, attn_sink, conv_pw1_w, conv_pw1_b, conv_dw_w, conv_dw_b, conv_norm_g, conv_norm_b, conv_pw2_w, conv_pw2_b, ln1_g, ln1_b, mlp_w1, mlp_b1, mlp_w2, mlp_b2, ln2_g, ln2_b):
    depth = ada_w.shape[0]
    bsz, seq, _ = x_prompt.shape
    dec_bsz, dec_seq, _ = x_sample.shape
    n_ctx = cache_k.shape[2]
    alpha = (2.0 * depth) ** 0.25
    assert 1 + dec_bsz <= COND_ROWS

    rows = lambda a: a.reshape(a.shape[0], 1, a.shape[-1])
    lane_tiles = D_MODEL // LANES

    cond = jnp.concatenate([c_ctx[None], c, jnp.zeros((COND_ROWS - 1 - dec_bsz, D_MODEL), F32)], axis=0)
    mods = _modulation(cond, ada_w, ada_b).reshape(depth, COND_ROWS, N_MOD, D_MODEL)
    mods = jnp.pad(mods, ((0, 0), (0, 0), (0, MOD_ROWS - N_MOD), (0, 0)))

    cos, sin = _rope_tables(dec_seq)

    wqkv, wo = attn_w_qkv.astype(BF16), attn_w_o.astype(BF16)
    pw1, pw2 = conv_pw1_w.astype(BF16), conv_pw2_w.astype(BF16)
    w1, w2 = mlp_w1.astype(BF16), mlp_w2.astype(BF16)
    ln1 = (rows(ln1_g), rows(ln1_b))
    n_conv = conv_dw_w.shape[0]
    dw = conv_dw_w.reshape(n_conv, CONV_WIDTH, lane_tiles, LANES).transpose(0, 2, 1, 3)
    db = conv_dw_b.reshape(n_conv, lane_tiles, 1, LANES)
    pw1 = pw1.reshape(n_conv, D_MODEL, 2, lane_tiles, LANES).transpose(0, 3, 1, 2, 4)
    pw1 = pw1.reshape(n_conv, lane_tiles, D_MODEL, 2 * LANES)
    pb1 = conv_pw1_b.reshape(n_conv, 2, lane_tiles, 1, LANES).transpose(0, 2, 3, 1, 4)
    pb1 = pb1.reshape(n_conv, lane_tiles, 1, 2 * LANES)
    conv_w = (pw1, pb1, dw, db, rows(conv_norm_g), rows(conv_norm_b), pw2, rows(conv_pw2_b))
    mlp_w = (w1, rows(mlp_b1), w2, rows(mlp_b2), rows(ln2_g), rows(ln2_b))

    yp = x_prompt
    ys = x_sample.reshape(dec_bsz * dec_seq, D_MODEL)
    new_k, new_v = [], []
    for i in range(depth):
        mod_p = mods[i, 0:1]
        mod_s = mods[i, 1:1 + dec_bsz]
        j = i // N_MIXERS
        if i % N_MIXERS == 0:
            yp, kp, vp = _prompt_attn_layer(yp, mod_p, attn_sink[j], i, j, wqkv, wo, *ln1, alpha)
            new_k.append(kp.reshape(bsz, seq, N_KV_HEADS, HEAD_DIM))
            new_v.append(vp.reshape(bsz, seq, N_KV_HEADS, HEAD_DIM))
            q, k, v = _sample_qkv(ys, mod_s, dec_seq, cos, sin, j, wqkv)
            ys = _sample_attn_layer(ys, mod_s, dec_seq, q, k, v,
                                    cache_k[:, j].reshape(dec_bsz, n_ctx, KV_WIDTH),
                                    cache_v[:, j].reshape(dec_bsz, n_ctx, KV_WIDTH),
                                    attn_sink[j], i, j, wo, *ln1, alpha)
        else:
            yp = _conv_layer(yp.reshape(bsz * seq, D_MODEL), mod_p, bsz * seq, seq, i, j, *conv_w, *ln1,
                             alpha).reshape(bsz, seq, D_MODEL)
            ys = _conv_layer(ys, mod_s, dec_seq, dec_seq, i, j, *conv_w, *ln1, alpha)
        yp = _mlp(yp.reshape(bsz * seq, D_MODEL), mod_p, bsz * seq, i, *mlp_w, alpha).reshape(bsz, seq, D_MODEL)
        ys = _mlp(ys, mod_s, dec_seq, i, *mlp_w, alpha)
    new_cache_k = jnp.stack(new_k, axis=1)
    new_cache_v = jnp.stack(new_v, axis=1)
    return (yp, ys.reshape(dec_bsz, dec_seq, D_MODEL), new_cache_k, new_cache_v)
```

```python
import functools
import math

import jax
import jax.numpy as jnp
from jax import lax
from jax.experimental import pallas as pl
from jax.experimental.pallas import tpu as pltpu

D_MODEL = 1024
GRID_W = 64
N_MIXERS = 2
HEAD_DIM = 64
N_HEADS = D_MODEL // HEAD_DIM
N_KV_HEADS = 4
GROUP = N_HEADS // N_KV_HEADS
KV_WIDTH = N_KV_HEADS * HEAD_DIM
QK_WIDTH = D_MODEL + KV_WIDTH
QKV_WIDTH = D_MODEL + 2 * KV_WIDTH
BLOCK = 128
ROPE_THETA = 10000.0
CONV_WIDTH = 31
CONV_PAD = (CONV_WIDTH - 1) // 2
D_FF = 4 * D_MODEL
N_MOD = 6
LN_EPS = 1e-5
NEG_INF = -1e30
ATTN_SCALE = HEAD_DIM ** -0.5
LOG2E = math.log2(math.e)
Q_SCALE = ATTN_SCALE * LOG2E

LANES = 128
SUBLANES = 8
VMEM_LIMIT_BYTES = 56 * 1024 * 1024

MOD_ROWS = 8
COND_ROWS = 8
HALO = 16
CONV_ROW_STRIDE = 4
CONV_ROW_BLOCK = 128
CONV_TILE_ROWS = 512
CONV_SUB_ROWS = 256
SOFTMAX_ROWS = 64
MLP_TILE_ROWS = 1024
MLP_SUB_ROWS = 256

BF16 = jnp.bfloat16
F32 = jnp.float32


def _const_spec(shape):
    n = len(shape)
    return pl.BlockSpec(shape, lambda *_: (0,) * n, pipeline_mode=pl.Buffered(1))


def _layer_spec(shape, layer):
    n = len(shape)
    return pl.BlockSpec((None,) + tuple(shape[1:]), lambda *_: (layer,) + (0,) * (n - 1),
                        pipeline_mode=pl.Buffered(1))


def _params(n_axes):
    return pltpu.CompilerParams(dimension_semantics=("arbitrary",) * n_axes,
                                vmem_limit_bytes=VMEM_LIMIT_BYTES)


def _layer_norm(z, g, b):
    mu = jnp.mean(z, axis=-1, keepdims=True)
    zc = z - mu
    var = jnp.mean(zc * zc, axis=-1, keepdims=True)
    return zc * lax.rsqrt(var + LN_EPS) * g + b


def _bdot(a, b):
    return jnp.dot(a, b, preferred_element_type=F32)


def _bdot_nt(a, b):
    return lax.dot_general(a, b, (((1,), (1,)), ((), ())), preferred_element_type=F32)


def _mod_kernel(cond_ref, w_ref, b_ref, out_ref):
    c = cond_ref[...]
    s = c * jax.nn.sigmoid(c)
    out_ref[0] = jnp.dot(s, w_ref[0], preferred_element_type=F32,
                         precision=lax.Precision.HIGHEST) + b_ref[0]


def _modulation(cond, ada_w, ada_b):
    depth = ada_w.shape[0]
    tn = 1536
    width = N_MOD * D_MODEL
    return pl.pallas_call(
        _mod_kernel,
        grid=(depth, width // tn),
        in_specs=[
            pl.BlockSpec((COND_ROWS, D_MODEL), lambda l, n: (0, 0)),
            pl.BlockSpec((1, D_MODEL, tn), lambda l, n: (l, 0, n)),
            pl.BlockSpec((1, 1, tn), lambda l, n: (l, 0, n)),
        ],
        out_specs=pl.BlockSpec((1, COND_ROWS, tn), lambda l, n: (l, 0, n)),
        out_shape=jax.ShapeDtypeStruct((depth, COND_ROWS, width), F32),
        compiler_params=_params(2),
        name="modulation",
    )(cond, ada_w, ada_b.reshape(depth, 1, width))


def _mlp_kernel(y_ref, mod_ref, w1_ref, b1_ref, w2_ref, b2_ref, g_ref, b_ref, out_ref, *,
                alpha, ff_chunk):
    m = mod_ref[0]
    for r0 in range(0, y_ref.shape[0], MLP_SUB_ROWS):
        rows = slice(r0, r0 + MLP_SUB_ROWS)
        y = y_ref[rows, :]
        h = (y * (1.0 + m[4:5]) + m[3:4]).astype(BF16)
        acc = jnp.zeros(y.shape, F32)
        for c in range(D_FF // ff_chunk):
            cols = slice(c * ff_chunk, (c + 1) * ff_chunk)
            a = _bdot(h, w1_ref[:, cols]) + b1_ref[:, cols]
            a = jnp.maximum(a, 0.0)
            acc = acc + _bdot((a * a).astype(BF16), w2_ref[cols, :])
        z = alpha * y + m[5:6] * (acc + b2_ref[...])
        out_ref[rows, :] = _layer_norm(z, g_ref[...], b_ref[...])


def _mlp(y, mod, rows_per_group, layer, w1, b1, w2, b2, g, b, alpha):
    m_rows = y.shape[0]
    tm = MLP_TILE_ROWS
    kern = functools.partial(_mlp_kernel, alpha=alpha, ff_chunk=1024)
    return pl.pallas_call(
        kern,
        grid=(m_rows // tm,),
        in_specs=[
            pl.BlockSpec((tm, D_MODEL), lambda i: (i, 0)),
            pl.BlockSpec((1, MOD_ROWS, D_MODEL), lambda i: (i * tm // rows_per_group, 0, 0)),
            _layer_spec(w1.shape, layer),
            _layer_spec(b1.shape, layer),
            _layer_spec(w2.shape, layer),
            _layer_spec(b2.shape, layer),
            _layer_spec(g.shape, layer),
            _layer_spec(b.shape, layer),
        ],
        out_specs=pl.BlockSpec((tm, D_MODEL), lambda i: (i, 0)),
        out_shape=jax.ShapeDtypeStruct((m_rows, D_MODEL), F32),
        compiler_params=_params(1),
        name="mlp",
    )(y, mod, w1, b1, w2, b2, g, b)


def _head_pair_operand(x, kv_head):
    tile = x[:, LANES * (kv_head // 2):LANES * (kv_head // 2 + 1)]
    swapped = pltpu.roll(tile, HEAD_DIM, axis=1)
    lane = lax.broadcasted_iota(jnp.int32, (1, LANES), 1)
    low = lane < HEAD_DIM
    if kv_head % 2 == 0:
        lo_src, hi_src = tile, swapped
    else:
        lo_src, hi_src = swapped, tile
    lo = jnp.where(low, lo_src, 0.0).astype(BF16)
    hi = jnp.where(low, 0.0, hi_src).astype(BF16)
    pieces = []
    for blk in range(x.shape[0] // BLOCK):
        pieces += [lo[blk * BLOCK:(blk + 1) * BLOCK], hi[blk * BLOCK:(blk + 1) * BLOCK]]
    return jnp.concatenate(pieces, axis=0)


def _pair_softmax(s_ref, p_ref, stat_ref, sinks, row_plan):
    def blocks(first_block, n_blocks):
        return range(first_block, first_block + n_blocks)

    def load(rows, block, head, plan_row, biases):
        tile = 2 * block + head
        v = s_ref[rows, tile * LANES:(tile + 1) * LANES]
        if block in biases:
            v = v + biases[block][plan_row:plan_row + SOFTMAX_ROWS, :]
        return v

    for row0, n_rows, first_block, n_blocks, biases in row_plan:
        for r in range(0, n_rows, SOFTMAX_ROWS):
            rows = slice(row0 + r, row0 + r + SOFTMAX_ROWS)
            for head in range(2):
                acc = None
                for block in blocks(first_block, n_blocks):
                    v = load(rows, block, head, r, biases)
                    acc = v if acc is None else jnp.maximum(acc, v)
                stat_ref[0, rows, head * LANES:(head + 1) * LANES] = acc
    mx = [jnp.maximum(jnp.max(stat_ref[0, :, head * LANES:(head + 1) * LANES], axis=-1, keepdims=True),
                      sinks[head]) for head in range(2)]
    for row0, n_rows, first_block, n_blocks, biases in row_plan:
        for r in range(0, n_rows, SOFTMAX_ROWS):
            rows = slice(row0 + r, row0 + r + SOFTMAX_ROWS)
            for head in range(2):
                m_rows = mx[head][row0 + r:row0 + r + SOFTMAX_ROWS]
                acc = None
                for block in blocks(first_block, n_blocks):
                    p = jnp.exp2(load(rows, block, head, r, biases) - m_rows)
                    acc = p if acc is None else acc + p
                    tile = 2 * block + head
                    p_ref[rows, tile * LANES:(tile + 1) * LANES] = p.astype(BF16)
                stat_ref[1, rows, head * LANES:(head + 1) * LANES] = acc
    recip = [1.0 / (jnp.sum(stat_ref[1, :, head * LANES:(head + 1) * LANES], axis=-1, keepdims=True)
                    + jnp.exp2(sinks[head] - mx[head])) for head in range(2)]
    lane = lax.broadcasted_iota(jnp.int32, (1, LANES), 1)
    return jnp.where(lane < HEAD_DIM, recip[0], recip[1])


def _attend_all(q_tile, k_src, v_src, sink_ref, o_scr, s_scr, p_scr, stat_scr, row_plan):
    n_tiles = D_MODEL // LANES
    operands = {}

    def kv_operands(t):
        kv_head = t // (GROUP // 2)
        if kv_head not in operands:
            operands[kv_head] = (_head_pair_operand(k_src, kv_head), _head_pair_operand(v_src, kv_head))
        return operands[kv_head]

    def scores(t):
        s_scr[t % 2] = _bdot_nt(q_tile(t), kv_operands(t)[0])

    def values(t, recip):
        o = _bdot(p_scr[t % 2], kv_operands(t)[1]) * recip
        o_scr[:, LANES * t:LANES * (t + 1)] = o.astype(BF16)

    scores(0)
    recip_prev = None
    for t in range(n_tiles):
        if t + 1 < n_tiles:
            scores(t + 1)
        if t > 0:
            values(t - 1, recip_prev)
        sinks = (sink_ref[2 * t] * LOG2E, sink_ref[2 * t + 1] * LOG2E)
        recip_prev = _pair_softmax(s_scr.at[t % 2], p_scr.at[t % 2], stat_scr.at[t % 2], sinks, row_plan)
    values(n_tiles - 1, recip_prev)


def _prompt_attn_kernel(sink_ref, x_ref, mod_ref, wqkv_ref, wo_ref, g_ref, b_ref,
                        y_ref, k_ref, v_ref, o_scr, s_scr, p_scr, stat_scr, *, alpha):
    x = x_ref[0]
    seq = x.shape[0]
    m = mod_ref[0]
    h = (x * (1.0 + m[1:2]) + m[0:1]).astype(BF16)
    qkv = _bdot(h, wqkv_ref[...])
    k = qkv[:, D_MODEL:QK_WIDTH]
    v = qkv[:, QK_WIDTH:]
    k_ref[0] = k
    v_ref[0] = v
    q = (qkv[:, :D_MODEL] * Q_SCALE).astype(BF16)
    row_plan = [(0, seq, 0, seq // BLOCK, {})]
    _attend_all(lambda t: q[:, LANES * t:LANES * (t + 1)], k, v, sink_ref,
                o_scr, s_scr, p_scr, stat_scr, row_plan)
    out = _bdot(o_scr[...], wo_ref[...])
    z = alpha * x + m[2:3] * out
    y_ref[0] = _layer_norm(z, g_ref[...], b_ref[...])


def _attn_scratch(rows, n_key_blocks):
    width = 2 * n_key_blocks * LANES
    return [pltpu.VMEM((rows, D_MODEL), BF16),
            pltpu.VMEM((2, rows, width), F32),
            pltpu.VMEM((2, rows, width), BF16),
            pltpu.VMEM((2, 2, rows, 2 * LANES), F32)]


def _prompt_attn_layer(x, mod, sink, layer, attn_layer, wqkv, wo, g, b, alpha):
    bsz, seq, _ = x.shape
    kern = functools.partial(_prompt_attn_kernel, alpha=alpha)
    return pl.pallas_call(
        kern,
        grid=(bsz,),
        in_specs=[
            pl.BlockSpec(memory_space=pltpu.SMEM),
            pl.BlockSpec((1, seq, D_MODEL), lambda i: (i, 0, 0)),
            pl.BlockSpec((1, MOD_ROWS, D_MODEL), lambda i: (0, 0, 0)),
            _layer_spec(wqkv.shape, attn_layer),
            _layer_spec(wo.shape, attn_layer),
            _layer_spec(g.shape, layer),
            _layer_spec(b.shape, layer),
        ],
        out_specs=[
            pl.BlockSpec((1, seq, D_MODEL), lambda i: (i, 0, 0)),
            pl.BlockSpec((1, seq, KV_WIDTH), lambda i: (i, 0, 0)),
            pl.BlockSpec((1, seq, KV_WIDTH), lambda i: (i, 0, 0)),
        ],
        out_shape=[
            jax.ShapeDtypeStruct((bsz, seq, D_MODEL), F32),
            jax.ShapeDtypeStruct((bsz, seq, KV_WIDTH), F32),
            jax.ShapeDtypeStruct((bsz, seq, KV_WIDTH), F32),
        ],
        scratch_shapes=_attn_scratch(seq, seq // BLOCK),
        compiler_params=_params(1),
        name="prompt_attn",
    )(sink, x, mod, wqkv, wo, g, b)


def _rope_tables(n_tokens):
    rows = n_tokens // GRID_W
    row = jnp.repeat(jnp.arange(rows, dtype=F32), GRID_W)
    col = jnp.tile(jnp.arange(GRID_W, dtype=F32), rows)
    half = HEAD_DIM // 2
    inv = ROPE_THETA ** (-jnp.arange(0, half, 2, dtype=F32) / half)
    ang_r = row[:, None] * inv
    ang_c = col[:, None] * inv
    cos = jnp.concatenate([jnp.cos(ang_r)] * 2 + [jnp.cos(ang_c)] * 2, axis=-1)
    sin = jnp.concatenate([-jnp.sin(ang_r), jnp.sin(ang_r), -jnp.sin(ang_c), jnp.sin(ang_c)], axis=-1)
    return jnp.tile(cos, (1, LANES // HEAD_DIM)), jnp.tile(sin, (1, LANES // HEAD_DIM))


def _sample_qkv_kernel(x_ref, mod_ref, cos_ref, sin_ref, wqkv_ref, q_ref, k_ref, v_ref):
    x = x_ref[...]
    m = mod_ref[0]
    h = (x * (1.0 + m[1:2]) + m[0:1]).astype(BF16)
    qkv = _bdot(h, wqkv_ref[...])
    cos = cos_ref[...]
    sin = sin_ref[...]
    lane = lax.broadcasted_iota(jnp.int32, (1, LANES), 1)
    first = (lane % (HEAD_DIM // 2)) < (HEAD_DIM // 4)
    quarter = HEAD_DIM // 4
    for t in range(QK_WIDTH // LANES):
        tile = qkv[:, LANES * t:LANES * (t + 1)]
        partner = jnp.where(first, pltpu.roll(tile, LANES - quarter, axis=1),
                            pltpu.roll(tile, quarter, axis=1))
        rot = tile * cos + partner * sin
        if t < D_MODEL // LANES:
            q_ref[:, LANES * t:LANES * (t + 1)] = (rot * Q_SCALE).astype(BF16)
        else:
            tk = t - D_MODEL // LANES
            k_ref[:, LANES * tk:LANES * (tk + 1)] = rot.astype(BF16)
    v_ref[...] = qkv[:, QK_WIDTH:].astype(BF16)


def _sample_qkv(x, mod, seq, cos, sin, attn_layer, wqkv):
    m_rows = x.shape[0]
    tm = 512
    tiles_per_seq = seq // tm
    return pl.pallas_call(
        _sample_qkv_kernel,
        grid=(m_rows // tm,),
        in_specs=[
            pl.BlockSpec((tm, D_MODEL), lambda i: (i, 0)),
            pl.BlockSpec((1, MOD_ROWS, D_MODEL), lambda i: (i // tiles_per_seq, 0, 0)),
            pl.BlockSpec((tm, LANES), lambda i: (i % tiles_per_seq, 0)),
            pl.BlockSpec((tm, LANES), lambda i: (i % tiles_per_seq, 0)),
            _layer_spec(wqkv.shape, attn_layer),
        ],
        out_specs=[
            pl.BlockSpec((tm, D_MODEL), lambda i: (i, 0)),
            pl.BlockSpec((tm, KV_WIDTH), lambda i: (i, 0)),
            pl.BlockSpec((tm, KV_WIDTH), lambda i: (i, 0)),
        ],
        out_shape=[
            jax.ShapeDtypeStruct((m_rows, D_MODEL), BF16),
            jax.ShapeDtypeStruct((m_rows, KV_WIDTH), BF16),
            jax.ShapeDtypeStruct((m_rows, KV_WIDTH), BF16),
        ],
        compiler_params=_params(1),
        name="sample_qkv",
    )(x, mod, cos, sin, wqkv)


def _sample_attn_kernel(sink_ref, x_ref, mod_ref, q_ref,
                        kp_ref, km_ref, kn_ref, vp_ref, vm_ref, vn_ref, ck_ref, cv_ref,
                        wo_ref, g_ref, b_ref, y_ref, o_scr, s_scr, p_scr, stat_scr, bias_scr, *,
                        alpha):
    i = pl.program_id(1)
    n_tiles = pl.num_programs(1)
    x = x_ref[...]
    m = mod_ref[0]
    ctx_blocks = ck_ref.shape[1] // BLOCK
    k_all = jnp.concatenate([kp_ref[...].astype(F32), km_ref[...].astype(F32), ck_ref[0],
                             kn_ref[...].astype(F32)], axis=0)
    v_all = jnp.concatenate([vp_ref[...].astype(F32), vm_ref[...].astype(F32), cv_ref[0],
                             vn_ref[...].astype(F32)], axis=0)
    next_block = 3 + ctx_blocks
    qi = lax.broadcasted_iota(jnp.int32, (BLOCK, BLOCK), 0)
    ki = lax.broadcasted_iota(jnp.int32, (BLOCK, BLOCK), 1)
    bias_scr[0] = jnp.where(ki >= qi, jnp.where(i > 0, 0.0, NEG_INF), NEG_INF)
    bias_scr[1] = jnp.where(ki <= qi, 0.0, NEG_INF)
    bias_scr[2] = jnp.where(ki >= qi, 0.0, NEG_INF)
    bias_scr[3] = jnp.where(ki <= qi, jnp.where(i < n_tiles - 1, 0.0, NEG_INF), NEG_INF)
    row_plan = [(0, BLOCK, 0, next_block, {0: bias_scr.at[0], 2: bias_scr.at[1]}),
                (BLOCK, BLOCK, 1, next_block, {1: bias_scr.at[2], next_block: bias_scr.at[3]})]
    for slot in range(2):
        p_scr[slot, 0:BLOCK, 2 * next_block * LANES:] = jnp.zeros((BLOCK, 2 * LANES), BF16)
        p_scr[slot, BLOCK:, 0:2 * LANES] = jnp.zeros((BLOCK, 2 * LANES), BF16)
    _attend_all(lambda t: q_ref[:, LANES * t:LANES * (t + 1)], k_all, v_all, sink_ref,
                o_scr, s_scr, p_scr, stat_scr, row_plan)
    out = _bdot(o_scr[...], wo_ref[...])
    z = alpha * x + m[2:3] * out
    y_ref[...] = _layer_norm(z, g_ref[...], b_ref[...])


def _sample_attn_layer(x, mod, seq, q, k, v, ctx_k, ctx_v, sink, layer, attn_layer, wo, g, b, alpha):
    m_rows = x.shape[0]
    bsz = m_rows // seq
    tq = 2 * BLOCK
    tiles = seq // tq
    blocks = seq // BLOCK
    n_ctx = ctx_k.shape[1]
    assert n_ctx % BLOCK == 0
    kern = functools.partial(_sample_attn_kernel, alpha=alpha)
    row_tile = lambda bi, i: (bi * tiles + i, 0)
    prev_blk = lambda bi, i: (bi * blocks + jnp.maximum(2 * i - 1, 0), 0)
    next_blk = lambda bi, i: (bi * blocks + jnp.minimum(2 * i + 2, blocks - 1), 0)
    ctx_blk = lambda bi, i: (bi, 0, 0)
    return pl.pallas_call(
        kern,
        grid=(bsz, tiles),
        in_specs=[
            pl.BlockSpec(memory_space=pltpu.SMEM),
            pl.BlockSpec((tq, D_MODEL), row_tile),
            pl.BlockSpec((1, MOD_ROWS, D_MODEL), ctx_blk),
            pl.BlockSpec((tq, D_MODEL), row_tile),
            pl.BlockSpec((BLOCK, KV_WIDTH), prev_blk),
            pl.BlockSpec((tq, KV_WIDTH), row_tile),
            pl.BlockSpec((BLOCK, KV_WIDTH), next_blk),
            pl.BlockSpec((BLOCK, KV_WIDTH), prev_blk),
            pl.BlockSpec((tq, KV_WIDTH), row_tile),
            pl.BlockSpec((BLOCK, KV_WIDTH), next_blk),
            pl.BlockSpec((1, n_ctx, KV_WIDTH), ctx_blk),
            pl.BlockSpec((1, n_ctx, KV_WIDTH), ctx_blk),
            _layer_spec(wo.shape, attn_layer),
            _layer_spec(g.shape, layer),
            _layer_spec(b.shape, layer),
        ],
        out_specs=pl.BlockSpec((tq, D_MODEL), row_tile),
        out_shape=jax.ShapeDtypeStruct((m_rows, D_MODEL), F32),
        scratch_shapes=_attn_scratch(tq, 4 + n_ctx // BLOCK) + [pltpu.VMEM((4, BLOCK, BLOCK), F32)],
        compiler_params=_params(2),
        name="sample_attn",
    )(sink, x, mod, q, k, k, k, v, v, v, ctx_k, ctx_v, wo, g, b)


def _conv_kernel(xp_ref, x_ref, xn_ref, mod_ref, pw1_ref, pb1_ref, dw_ref, db_ref,
                 cg_ref, cb_ref, pw2_ref, pb2_ref, g_ref, b_ref, y_ref, *scratch,
                 alpha, tm, sub, tiles_per_seq, seq_per_sub):
    i = pl.program_id(0)
    t = i % tiles_per_seq
    m = mod_ref[0]
    lane_tiles = D_MODEL // LANES
    n_sub = tm // sub
    h_scrs, u_scrs, c_scrs = scratch[:n_sub], scratch[n_sub:2 * n_sub], scratch[2 * n_sub:]
    recompute_halo = not seq_per_sub

    def modulated_input(s):
        r0 = s * sub
        rows = x_ref[r0:r0 + sub, :]
        if recompute_halo:
            top = xp_ref[...] if s == 0 else x_ref[r0 - HALO:r0, :]
            bot = xn_ref[...] if s == n_sub - 1 else x_ref[r0 + sub:r0 + sub + HALO, :]
            rows = jnp.concatenate([top, rows, bot], axis=0)
        h_scrs[s][...] = (rows * (1.0 + m[1:2]) + m[0:1]).astype(BF16)

    def glu_chunk(s, c):
        value_cols = slice(c * LANES, (c + 1) * LANES)
        gate_cols = slice(D_MODEL + c * LANES, D_MODEL + (c + 1) * LANES)
        w = jnp.concatenate([pw1_ref[:, value_cols], pw1_ref[:, gate_cols]], axis=1)
        a = _bdot(h_scrs[s][...], w)
        u = (a[:, :LANES] + pb1_ref[:, value_cols]) * jax.nn.sigmoid(a[:, LANES:] + pb1_ref[:, gate_cols])
        zeros = jnp.zeros((HALO, LANES), F32)
        if recompute_halo:
            u_scrs[s][c, :, :] = u
            if s == 0:
                u_scrs[s][c, 0:HALO, :] = jnp.where(t > 0, u[:HALO], 0.0)
            if s == n_sub - 1:
                u_scrs[s][c, HALO + sub:, :] = jnp.where(t < tiles_per_seq - 1, u[HALO + sub:], 0.0)
        else:
            u_scrs[s][c, 0:HALO, :] = zeros
            u_scrs[s][c, HALO:HALO + sub, :] = u
            u_scrs[s][c, HALO + sub:, :] = zeros

    def conv_taps(s, c):
        span_rows = SUBLANES * CONV_ROW_STRIDE
        for base in range(0, sub, CONV_ROW_BLOCK):
            starts = [base + sp + j for sp in range(0, CONV_ROW_BLOCK, span_rows)
                      for j in range(CONV_ROW_STRIDE)]
            bias = db_ref[c]
            accs = [jnp.zeros((SUBLANES, LANES), F32) + bias for _ in starts]
            for tap in range(CONV_WIDTH):
                w = dw_ref[c, tap:tap + 1, :]
                off = HALO - CONV_PAD + tap
                for n, start in enumerate(starts):
                    rows = pl.ds(start + off, SUBLANES, stride=CONV_ROW_STRIDE)
                    accs[n] = accs[n] + w * u_scrs[s][c, rows, :]
            for n, start in enumerate(starts):
                c_scrs[s][c, pl.ds(start, SUBLANES, stride=CONV_ROW_STRIDE), :] = accs[n]

    def finish(s):
        r0 = s * sub
        x = x_ref[r0:r0 + sub, :]
        conv = jnp.concatenate([c_scrs[s][c] for c in range(lane_tiles)], axis=1)
        un = _layer_norm(conv, cg_ref[...], cb_ref[...])
        un = un * jax.nn.sigmoid(un)
        out = _bdot(un.astype(BF16), pw2_ref[...]) + pb2_ref[...]
        z = alpha * x + m[2:3] * out
        y_ref[r0:r0 + sub, :] = _layer_norm(z, g_ref[...], b_ref[...])

    for s in range(n_sub):
        modulated_input(s)
    for c in range(lane_tiles):
        glu_chunk(0, c)
    for s in range(n_sub):
        for c in range(lane_tiles):
            conv_taps(s, c)
            if s + 1 < n_sub:
                glu_chunk(s + 1, c)
        finish(s)


def _conv_layer(x, mod, rows_per_group, seq, layer, conv_layer, pw1, pb1, dw, db, cg, cb, pw2, pb2,
                g, b, alpha):
    m_rows = x.shape[0]
    tm = CONV_TILE_ROWS
    sub = CONV_SUB_ROWS
    seq_per_sub = seq == sub
    assert seq_per_sub or seq % tm == 0
    tiles_per_seq = max(seq // tm, 1)
    in_rows = sub if seq_per_sub else sub + 2 * HALO
    halo_per_tile = tm // HALO
    n_halo_blocks = m_rows // HALO
    kern = functools.partial(_conv_kernel, alpha=alpha, tm=tm, sub=sub, tiles_per_seq=tiles_per_seq,
                             seq_per_sub=seq_per_sub)
    prev_blk = lambda i: (jnp.maximum(i * halo_per_tile - 1, 0), 0)
    next_blk = lambda i: (jnp.minimum((i + 1) * halo_per_tile, n_halo_blocks - 1), 0)
    return pl.pallas_call(
        kern,
        grid=(m_rows // tm,),
        in_specs=[
            pl.BlockSpec((HALO, D_MODEL), prev_blk),
            pl.BlockSpec((tm, D_MODEL), lambda i: (i, 0)),
            pl.BlockSpec((HALO, D_MODEL), next_blk),
            pl.BlockSpec((1, MOD_ROWS, D_MODEL), lambda i: (i * tm // rows_per_group, 0, 0)),
            _layer_spec(pw1.shape, conv_layer),
            _layer_spec(pb1.shape, conv_layer),
            _layer_spec(dw.shape, conv_layer),
            _layer_spec(db.shape, conv_layer),
            _layer_spec(cg.shape, conv_layer),
            _layer_spec(cb.shape, conv_layer),
            _layer_spec(pw2.shape, conv_layer),
            _layer_spec(pb2.shape, conv_layer),
            _layer_spec(g.shape, layer),
            _layer_spec(b.shape, layer),
        ],
        out_specs=pl.BlockSpec((tm, D_MODEL), lambda i: (i, 0)),
        out_shape=jax.ShapeDtypeStruct((m_rows, D_MODEL), F32),
        scratch_shapes=([pltpu.VMEM((in_rows, D_MODEL), BF16)] * (tm // sub)
                        + [pltpu.VMEM((D_MODEL // LANES, sub + 2 * HALO, LANES), F32)] * (tm // sub)
                        + [pltpu.VMEM((D_MODEL // LANES, sub, LANES), F32)] * (tm // sub)),
        compiler_params=_params(1),
        name="conv",
    )(x, x, x, mod, pw1, pb1, dw, db, cg, cb, pw2, pb2, g, b)


def kernel(x_prompt, x_sample, cache_k, cache_v, c, c_ctx, ada_w, ada_b, attn_w_qkv, attn_w_o, attn_sink, conv_pw1_w, conv_pw1_b, conv_dw_w, conv_dw_b, conv_norm_g, conv_norm_b, conv_pw2_w, conv_pw2_b, ln1_g, ln1_b, mlp_w1, mlp_b1, mlp_w2, mlp_b2, ln2_g, ln2_b):
    depth = ada_w.shape[0]
    bsz, seq, _ = x_prompt.shape
    dec_bsz, dec_seq, _ = x_sample.shape
    n_ctx = cache_k.shape[2]
    alpha = (2.0 * depth) ** 0.25
    assert 1 + dec_bsz <= COND_ROWS

    rows = lambda a: a.reshape(a.shape[0], 1, a.shape[-1])
    lane_tiles = D_MODEL // LANES

    cond = jnp.concatenate([c_ctx[None], c, jnp.zeros((COND_ROWS - 1 - dec_bsz, D_MODEL), F32)], axis=0)
    mods = _modulation(cond, ada_w, ada_b).reshape(depth, COND_ROWS, N_MOD, D_MODEL)
    mods = jnp.pad(mods, ((0, 0), (0, 0), (0, MOD_ROWS - N_MOD), (0, 0)))

    cos, sin = _rope_tables(dec_seq)

    wqkv, wo = attn_w_qkv.astype(BF16), attn_w_o.astype(BF16)
    pw1, pw2 = conv_pw1_w.astype(BF16), conv_pw2_w.astype(BF16)
    w1, w2 = mlp_w1.astype(BF16), mlp_w2.astype(BF16)
    ln1 = (rows(ln1_g), rows(ln1_b))
    n_conv = conv_dw_w.shape[0]
    dw = conv_dw_w.reshape(n_conv, CONV_WIDTH, lane_tiles, LANES).transpose(0, 2, 1, 3)
    db = conv_dw_b.reshape(n_conv, lane_tiles, 1, LANES)
    conv_w = (pw1, rows(conv_pw1_b), dw, db, rows(conv_norm_g), rows(conv_norm_b), pw2, rows(conv_pw2_b))
    mlp_w = (w1, rows(mlp_b1), w2, rows(mlp_b2), rows(ln2_g), rows(ln2_b))

    yp = x_prompt
    ys = x_sample.reshape(dec_bsz * dec_seq, D_MODEL)
    new_k, new_v = [], []
    for i in range(depth):
        mod_p = mods[i, 0:1]
        mod_s = mods[i, 1:1 + dec_bsz]
        j = i // N_MIXERS
        if i % N_MIXERS == 0:
            yp, kp, vp = _prompt_attn_layer(yp, mod_p, attn_sink[j], i, j, wqkv, wo, *ln1, alpha)
            new_k.append(kp.reshape(bsz, seq, N_KV_HEADS, HEAD_DIM))
            new_v.append(vp.reshape(bsz, seq, N_KV_HEADS, HEAD_DIM))
            q, k, v = _sample_qkv(ys, mod_s, dec_seq, cos, sin, j, wqkv)
            ys = _sample_attn_layer(ys, mod_s, dec_seq, q, k, v,
                                    cache_k[:, j].reshape(dec_bsz, n_ctx, KV_WIDTH),
                                    cache_v[:, j].reshape(dec_bsz, n_ctx, KV_WIDTH),
                                    attn_sink[j], i, j, wo, *ln1, alpha)
        else:
            yp = _conv_layer(yp.reshape(bsz * seq, D_MODEL), mod_p, bsz * seq, seq, i, j, *conv_w, *ln1,
                             alpha).reshape(bsz, seq, D_MODEL)
            ys = _conv_layer(ys, mod_s, dec_seq, dec_seq, i, j, *conv_w, *ln1, alpha)
        yp = _mlp(yp.reshape(bsz * seq, D_MODEL), mod_p, bsz * seq, i, *mlp_w, alpha).reshape(bsz, seq, D_MODEL)
        ys = _mlp(ys, mod_s, dec_seq, i, *mlp_w, alpha)
    new_cache_k = jnp.stack(new_k, axis=1)
    new_cache_v = jnp.stack(new_v, axis=1)
    return (yp, ys.reshape(dec_bsz, dec_seq, D_MODEL), new_cache_k, new_cache_v)
```

```python
import functools
import math

import jax
import jax.numpy as jnp
from jax import lax
from jax.experimental import pallas as pl
from jax.experimental.pallas import tpu as pltpu

D_MODEL = 1024
GRID_W = 64
N_MIXERS = 2
HEAD_DIM = 64
N_HEADS = D_MODEL // HEAD_DIM
N_KV_HEADS = 4
GROUP = N_HEADS // N_KV_HEADS
KV_WIDTH = N_KV_HEADS * HEAD_DIM
QK_WIDTH = D_MODEL + KV_WIDTH
QKV_WIDTH = D_MODEL + 2 * KV_WIDTH
BLOCK = 128
ROPE_THETA = 10000.0
CONV_WIDTH = 31
CONV_PAD = (CONV_WIDTH - 1) // 2
D_FF = 4 * D_MODEL
N_MOD = 6
LN_EPS = 1e-5
NEG_INF = -1e30
ATTN_SCALE = HEAD_DIM ** -0.5
LOG2E = math.log2(math.e)
Q_SCALE = ATTN_SCALE * LOG2E

LANES = 128
SUBLANES = 8
VMEM_LIMIT_BYTES = 56 * 1024 * 1024

MOD_ROWS = 8
COND_ROWS = 8
HALO = 16
CONV_ROW_STRIDE = 4
CONV_ROW_BLOCK = 128
CONV_TILE_ROWS = 512
CONV_SUB_ROWS = 256
CONV_CHUNK_TILES = 2
SOFTMAX_ROWS = 64
MOD_TILE_COLS = 3072
MLP_TILE_ROWS = 512
MLP_SUB_ROWS = 256

BF16 = jnp.bfloat16
F32 = jnp.float32


def _const_spec(shape):
    n = len(shape)
    return pl.BlockSpec(shape, lambda *_: (0,) * n, pipeline_mode=pl.Buffered(1))


def _layer_spec(shape, layer):
    n = len(shape)
    return pl.BlockSpec((None,) + tuple(shape[1:]), lambda *_: (layer,) + (0,) * (n - 1),
                        pipeline_mode=pl.Buffered(1))


def _params(n_axes):
    return pltpu.CompilerParams(dimension_semantics=("arbitrary",) * n_axes,
                                vmem_limit_bytes=VMEM_LIMIT_BYTES)


def _layer_norm(z, g, b):
    mu = jnp.mean(z, axis=-1, keepdims=True)
    zc = z - mu
    var = jnp.mean(zc * zc, axis=-1, keepdims=True)
    return zc * lax.rsqrt(var + LN_EPS) * g + b


def _bdot(a, b):
    return jnp.dot(a, b, preferred_element_type=F32)


def _bdot_nt(a, b):
    return lax.dot_general(a, b, (((1,), (1,)), ((), ())), preferred_element_type=F32)


def _mod_kernel(cond_ref, w_ref, b_ref, out_ref, s_scr, *, n_cond):
    @pl.when((pl.program_id(0) == 0) & (pl.program_id(1) == 0))
    def _():
        c = cond_ref[...]
        s_t = (c * jax.nn.sigmoid(c)).T
        for m in range(n_cond):
            s_scr[m] = jnp.broadcast_to(s_t[:, m:m + 1], (D_MODEL, LANES))

    zero_rows = [jnp.zeros((1, LANES), F32)] * (COND_ROWS - n_cond)
    for n in range(w_ref.shape[2] // LANES):
        cols = slice(n * LANES, (n + 1) * LANES)
        accs = [jnp.zeros((SUBLANES, LANES), F32) for _ in range(n_cond)]
        for k in range(0, D_MODEL, SUBLANES):
            w = w_ref[0, k:k + SUBLANES, cols]
            for m in range(n_cond):
                accs[m] = accs[m] + w * s_scr[m, k:k + SUBLANES, :]
        out_rows = [jnp.sum(a, axis=0, keepdims=True) for a in accs] + zero_rows
        out_ref[0, :, cols] = jnp.concatenate(out_rows, axis=0) + b_ref[0, :, cols]


def _modulation(cond, n_cond, ada_w, ada_b):
    depth = ada_w.shape[0]
    tn = MOD_TILE_COLS
    width = N_MOD * D_MODEL
    return pl.pallas_call(
        functools.partial(_mod_kernel, n_cond=n_cond),
        grid=(depth, width // tn),
        in_specs=[
            pl.BlockSpec((COND_ROWS, D_MODEL), lambda l, n: (0, 0)),
            pl.BlockSpec((1, D_MODEL, tn), lambda l, n: (l, 0, n)),
            pl.BlockSpec((1, 1, tn), lambda l, n: (l, 0, n)),
        ],
        out_specs=pl.BlockSpec((1, COND_ROWS, tn), lambda l, n: (l, 0, n)),
        out_shape=jax.ShapeDtypeStruct((depth, COND_ROWS, width), F32),
        scratch_shapes=[pltpu.VMEM((n_cond, D_MODEL, LANES), F32)],
        compiler_params=_params(2),
        name="modulation",
    )(cond, ada_w, ada_b.reshape(depth, 1, width))


def _mlp_kernel(y_ref, mod_ref, w1_ref, b1_ref, w2_ref, b2_ref, g_ref, b_ref, out_ref, *,
                alpha, ff_chunk):
    m = mod_ref[0]
    for r0 in range(0, y_ref.shape[0], MLP_SUB_ROWS):
        rows = slice(r0, r0 + MLP_SUB_ROWS)
        y = y_ref[rows, :]
        h = (y * (1.0 + m[4:5]) + m[3:4]).astype(BF16)
        acc = jnp.zeros(y.shape, F32)
        for c in range(D_FF // ff_chunk):
            cols = slice(c * ff_chunk, (c + 1) * ff_chunk)
            a = _bdot(h, w1_ref[:, cols]) + b1_ref[:, cols]
            a = jnp.maximum(a, 0.0)
            acc = acc + _bdot((a * a).astype(BF16), w2_ref[cols, :])
        z = alpha * y + m[5:6] * (acc + b2_ref[...])
        out_ref[rows, :] = _layer_norm(z, g_ref[...], b_ref[...])


def _mlp(y, mod, rows_per_group, layer, w1, b1, w2, b2, g, b, alpha):
    m_rows = y.shape[0]
    tm = MLP_TILE_ROWS
    kern = functools.partial(_mlp_kernel, alpha=alpha, ff_chunk=1024)
    return pl.pallas_call(
        kern,
        grid=(m_rows // tm,),
        in_specs=[
            pl.BlockSpec((tm, D_MODEL), lambda i: (i, 0)),
            pl.BlockSpec((1, MOD_ROWS, D_MODEL), lambda i: (i * tm // rows_per_group, 0, 0)),
            _layer_spec(w1.shape, layer),
            _layer_spec(b1.shape, layer),
            _layer_spec(w2.shape, layer),
            _layer_spec(b2.shape, layer),
            _layer_spec(g.shape, layer),
            _layer_spec(b.shape, layer),
        ],
        out_specs=pl.BlockSpec((tm, D_MODEL), lambda i: (i, 0)),
        out_shape=jax.ShapeDtypeStruct((m_rows, D_MODEL), F32),
        compiler_params=_params(1),
        name="mlp",
    )(y, mod, w1, b1, w2, b2, g, b)


def _head_pair_operand(x, kv_head):
    tile = x[:, LANES * (kv_head // 2):LANES * (kv_head // 2 + 1)]
    swapped = pltpu.roll(tile, HEAD_DIM, axis=1)
    lane = lax.broadcasted_iota(jnp.int32, (1, LANES), 1)
    low = lane < HEAD_DIM
    if kv_head % 2 == 0:
        lo_src, hi_src = tile, swapped
    else:
        lo_src, hi_src = swapped, tile
    lo = jnp.where(low, lo_src, 0.0).astype(BF16)
    hi = jnp.where(low, 0.0, hi_src).astype(BF16)
    pieces = []
    for blk in range(x.shape[0] // BLOCK):
        pieces += [lo[blk * BLOCK:(blk + 1) * BLOCK], hi[blk * BLOCK:(blk + 1) * BLOCK]]
    return jnp.concatenate(pieces, axis=0)


def _pair_softmax(s_ref, p_ref, stat_ref, sinks, row_plan):
    def blocks(first_block, n_blocks):
        return range(first_block, first_block + n_blocks)

    def load(rows, block, head, plan_row, biases):
        tile = 2 * block + head
        v = s_ref[rows, tile * LANES:(tile + 1) * LANES]
        if block in biases:
            v = v + biases[block][plan_row:plan_row + SOFTMAX_ROWS, :]
        return v

    for row0, n_rows, first_block, n_blocks, biases in row_plan:
        for r in range(0, n_rows, SOFTMAX_ROWS):
            rows = slice(row0 + r, row0 + r + SOFTMAX_ROWS)
            for head in range(2):
                acc = None
                for block in blocks(first_block, n_blocks):
                    v = load(rows, block, head, r, biases)
                    acc = v if acc is None else jnp.maximum(acc, v)
                stat_ref[0, rows, head * LANES:(head + 1) * LANES] = acc
    mx = [jnp.maximum(jnp.max(stat_ref[0, :, head * LANES:(head + 1) * LANES], axis=-1, keepdims=True),
                      sinks[head]) for head in range(2)]
    for row0, n_rows, first_block, n_blocks, biases in row_plan:
        for r in range(0, n_rows, SOFTMAX_ROWS):
            rows = slice(row0 + r, row0 + r + SOFTMAX_ROWS)
            for head in range(2):
                m_rows = mx[head][row0 + r:row0 + r + SOFTMAX_ROWS]
                acc = None
                for block in blocks(first_block, n_blocks):
                    p = jnp.exp2(load(rows, block, head, r, biases) - m_rows)
                    acc = p if acc is None else acc + p
                    tile = 2 * block + head
                    p_ref[rows, tile * LANES:(tile + 1) * LANES] = p.astype(BF16)
                stat_ref[1, rows, head * LANES:(head + 1) * LANES] = acc
    recip = [1.0 / (jnp.sum(stat_ref[1, :, head * LANES:(head + 1) * LANES], axis=-1, keepdims=True)
                    + jnp.exp2(sinks[head] - mx[head])) for head in range(2)]
    lane = lax.broadcasted_iota(jnp.int32, (1, LANES), 1)
    return jnp.where(lane < HEAD_DIM, recip[0], recip[1])


def _attend_all(q_tile, k_src, v_src, sink_ref, o_scr, s_scr, p_scr, stat_scr, row_plan):
    n_tiles = D_MODEL // LANES
    operands = {}

    def kv_operands(t):
        kv_head = t // (GROUP // 2)
        if kv_head not in operands:
            operands[kv_head] = (_head_pair_operand(k_src, kv_head), _head_pair_operand(v_src, kv_head))
        return operands[kv_head]

    def scores(t):
        s_scr[t % 2] = _bdot_nt(q_tile(t), kv_operands(t)[0])

    def values(t, recip):
        o = _bdot(p_scr[t % 2], kv_operands(t)[1]) * recip
        o_scr[:, LANES * t:LANES * (t + 1)] = o.astype(BF16)

    scores(0)
    recip_prev = None
    for t in range(n_tiles):
        if t + 1 < n_tiles:
            scores(t + 1)
        if t > 0:
            values(t - 1, recip_prev)
        sinks = (sink_ref[2 * t] * LOG2E, sink_ref[2 * t + 1] * LOG2E)
        recip_prev = _pair_softmax(s_scr.at[t % 2], p_scr.at[t % 2], stat_scr.at[t % 2], sinks, row_plan)
    values(n_tiles - 1, recip_prev)


def _prompt_attn_kernel(sink_ref, x_ref, mod_ref, wqkv_ref, wo_ref, g_ref, b_ref,
                        y_ref, k_ref, v_ref, o_scr, s_scr, p_scr, stat_scr, *, alpha):
    x = x_ref[0]
    seq = x.shape[0]
    m = mod_ref[0]
    h = (x * (1.0 + m[1:2]) + m[0:1]).astype(BF16)
    qkv = _bdot(h, wqkv_ref[...])
    k = qkv[:, D_MODEL:QK_WIDTH]
    v = qkv[:, QK_WIDTH:]
    k_ref[0] = k
    v_ref[0] = v
    q = (qkv[:, :D_MODEL] * Q_SCALE).astype(BF16)
    row_plan = [(0, seq, 0, seq // BLOCK, {})]
    _attend_all(lambda t: q[:, LANES * t:LANES * (t + 1)], k, v, sink_ref,
                o_scr, s_scr, p_scr, stat_scr, row_plan)
    out = _bdot(o_scr[...], wo_ref[...])
    z = alpha * x + m[2:3] * out
    y_ref[0] = _layer_norm(z, g_ref[...], b_ref[...])


def _attn_scratch(rows, n_key_blocks):
    width = 2 * n_key_blocks * LANES
    return [pltpu.VMEM((rows, D_MODEL), BF16),
            pltpu.VMEM((2, rows, width), F32),
            pltpu.VMEM((2, rows, width), BF16),
            pltpu.VMEM((2, 2, rows, 2 * LANES), F32)]


def _prompt_attn_layer(x, mod, sink, layer, attn_layer, wqkv, wo, g, b, alpha):
    bsz, seq, _ = x.shape
    kern = functools.partial(_prompt_attn_kernel, alpha=alpha)
    return pl.pallas_call(
        kern,
        grid=(bsz,),
        in_specs=[
            pl.BlockSpec(memory_space=pltpu.SMEM),
            pl.BlockSpec((1, seq, D_MODEL), lambda i: (i, 0, 0)),
            pl.BlockSpec((1, MOD_ROWS, D_MODEL), lambda i: (0, 0, 0)),
            _layer_spec(wqkv.shape, attn_layer),
            _layer_spec(wo.shape, attn_layer),
            _layer_spec(g.shape, layer),
            _layer_spec(b.shape, layer),
        ],
        out_specs=[
            pl.BlockSpec((1, seq, D_MODEL), lambda i: (i, 0, 0)),
            pl.BlockSpec((1, seq, KV_WIDTH), lambda i: (i, 0, 0)),
            pl.BlockSpec((1, seq, KV_WIDTH), lambda i: (i, 0, 0)),
        ],
        out_shape=[
            jax.ShapeDtypeStruct((bsz, seq, D_MODEL), F32),
            jax.ShapeDtypeStruct((bsz, seq, KV_WIDTH), F32),
            jax.ShapeDtypeStruct((bsz, seq, KV_WIDTH), F32),
        ],
        scratch_shapes=_attn_scratch(seq, seq // BLOCK),
        compiler_params=_params(1),
        name="prompt_attn",
    )(sink, x, mod, wqkv, wo, g, b)


def _rope_tables(n_tokens):
    rows = n_tokens // GRID_W
    row = jnp.repeat(jnp.arange(rows, dtype=F32), GRID_W)
    col = jnp.tile(jnp.arange(GRID_W, dtype=F32), rows)
    half = HEAD_DIM // 2
    inv = ROPE_THETA ** (-jnp.arange(0, half, 2, dtype=F32) / half)
    ang_r = row[:, None] * inv
    ang_c = col[:, None] * inv
    cos = jnp.concatenate([jnp.cos(ang_r)] * 2 + [jnp.cos(ang_c)] * 2, axis=-1)
    sin = jnp.concatenate([-jnp.sin(ang_r), jnp.sin(ang_r), -jnp.sin(ang_c), jnp.sin(ang_c)], axis=-1)
    return jnp.tile(cos, (1, LANES // HEAD_DIM)), jnp.tile(sin, (1, LANES // HEAD_DIM))


def _sample_qkv_kernel(x_ref, mod_ref, cos_ref, sin_ref, wqkv_ref, q_ref, k_ref, v_ref):
    x = x_ref[...]
    m = mod_ref[0]
    h = (x * (1.0 + m[1:2]) + m[0:1]).astype(BF16)
    qkv = _bdot(h, wqkv_ref[...])
    cos = cos_ref[...]
    sin = sin_ref[...]
    lane = lax.broadcasted_iota(jnp.int32, (1, LANES), 1)
    first = (lane % (HEAD_DIM // 2)) < (HEAD_DIM // 4)
    quarter = HEAD_DIM // 4
    for t in range(QK_WIDTH // LANES):
        tile = qkv[:, LANES * t:LANES * (t + 1)]
        partner = jnp.where(first, pltpu.roll(tile, LANES - quarter, axis=1),
                            pltpu.roll(tile, quarter, axis=1))
        rot = tile * cos + partner * sin
        if t < D_MODEL // LANES:
            q_ref[:, LANES * t:LANES * (t + 1)] = (rot * Q_SCALE).astype(BF16)
        else:
            tk = t - D_MODEL // LANES
            k_ref[:, LANES * tk:LANES * (tk + 1)] = rot.astype(BF16)
    v_ref[...] = qkv[:, QK_WIDTH:].astype(BF16)


def _sample_qkv(x, mod, seq, cos, sin, attn_layer, wqkv):
    m_rows = x.shape[0]
    tm = 512
    tiles_per_seq = seq // tm
    return pl.pallas_call(
        _sample_qkv_kernel,
        grid=(m_rows // tm,),
        in_specs=[
            pl.BlockSpec((tm, D_MODEL), lambda i: (i, 0)),
            pl.BlockSpec((1, MOD_ROWS, D_MODEL), lambda i: (i // tiles_per_seq, 0, 0)),
            pl.BlockSpec((tm, LANES), lambda i: (i % tiles_per_seq, 0)),
            pl.BlockSpec((tm, LANES), lambda i: (i % tiles_per_seq, 0)),
            _layer_spec(wqkv.shape, attn_layer),
        ],
        out_specs=[
            pl.BlockSpec((tm, D_MODEL), lambda i: (i, 0)),
            pl.BlockSpec((tm, KV_WIDTH), lambda i: (i, 0)),
            pl.BlockSpec((tm, KV_WIDTH), lambda i: (i, 0)),
        ],
        out_shape=[
            jax.ShapeDtypeStruct((m_rows, D_MODEL), BF16),
            jax.ShapeDtypeStruct((m_rows, KV_WIDTH), BF16),
            jax.ShapeDtypeStruct((m_rows, KV_WIDTH), BF16),
        ],
        compiler_params=_params(1),
        name="sample_qkv",
    )(x, mod, cos, sin, wqkv)


def _sample_attn_kernel(sink_ref, x_ref, mod_ref, q_ref,
                        kp_ref, km_ref, kn_ref, vp_ref, vm_ref, vn_ref, ck_ref, cv_ref,
                        wo_ref, g_ref, b_ref, y_ref, o_scr, s_scr, p_scr, stat_scr, bias_scr, *,
                        alpha):
    i = pl.program_id(1)
    n_tiles = pl.num_programs(1)
    x = x_ref[...]
    m = mod_ref[0]
    ctx_blocks = ck_ref.shape[1] // BLOCK
    k_all = jnp.concatenate([kp_ref[...].astype(F32), km_ref[...].astype(F32), ck_ref[0],
                             kn_ref[...].astype(F32)], axis=0)
    v_all = jnp.concatenate([vp_ref[...].astype(F32), vm_ref[...].astype(F32), cv_ref[0],
                             vn_ref[...].astype(F32)], axis=0)
    next_block = 3 + ctx_blocks
    qi = lax.broadcasted_iota(jnp.int32, (BLOCK, BLOCK), 0)
    ki = lax.broadcasted_iota(jnp.int32, (BLOCK, BLOCK), 1)
    bias_scr[0] = jnp.where(ki >= qi, jnp.where(i > 0, 0.0, NEG_INF), NEG_INF)
    bias_scr[1] = jnp.where(ki <= qi, 0.0, NEG_INF)
    bias_scr[2] = jnp.where(ki >= qi, 0.0, NEG_INF)
    bias_scr[3] = jnp.where(ki <= qi, jnp.where(i < n_tiles - 1, 0.0, NEG_INF), NEG_INF)
    row_plan = [(0, BLOCK, 0, next_block, {0: bias_scr.at[0], 2: bias_scr.at[1]}),
                (BLOCK, BLOCK, 1, next_block, {1: bias_scr.at[2], next_block: bias_scr.at[3]})]
    for slot in range(2):
        p_scr[slot, 0:BLOCK, 2 * next_block * LANES:] = jnp.zeros((BLOCK, 2 * LANES), BF16)
        p_scr[slot, BLOCK:, 0:2 * LANES] = jnp.zeros((BLOCK, 2 * LANES), BF16)
    _attend_all(lambda t: q_ref[:, LANES * t:LANES * (t + 1)], k_all, v_all, sink_ref,
                o_scr, s_scr, p_scr, stat_scr, row_plan)
    out = _bdot(o_scr[...], wo_ref[...])
    z = alpha * x + m[2:3] * out
    y_ref[...] = _layer_norm(z, g_ref[...], b_ref[...])


def _sample_attn_layer(x, mod, seq, q, k, v, ctx_k, ctx_v, sink, layer, attn_layer, wo, g, b, alpha):
    m_rows = x.shape[0]
    bsz = m_rows // seq
    tq = 2 * BLOCK
    tiles = seq // tq
    blocks = seq // BLOCK
    n_ctx = ctx_k.shape[1]
    assert n_ctx % BLOCK == 0
    kern = functools.partial(_sample_attn_kernel, alpha=alpha)
    row_tile = lambda bi, i: (bi * tiles + i, 0)
    prev_blk = lambda bi, i: (bi * blocks + jnp.maximum(2 * i - 1, 0), 0)
    next_blk = lambda bi, i: (bi * blocks + jnp.minimum(2 * i + 2, blocks - 1), 0)
    ctx_blk = lambda bi, i: (bi, 0, 0)
    return pl.pallas_call(
        kern,
        grid=(bsz, tiles),
        in_specs=[
            pl.BlockSpec(memory_space=pltpu.SMEM),
            pl.BlockSpec((tq, D_MODEL), row_tile),
            pl.BlockSpec((1, MOD_ROWS, D_MODEL), ctx_blk),
            pl.BlockSpec((tq, D_MODEL), row_tile),
            pl.BlockSpec((BLOCK, KV_WIDTH), prev_blk),
            pl.BlockSpec((tq, KV_WIDTH), row_tile),
            pl.BlockSpec((BLOCK, KV_WIDTH), next_blk),
            pl.BlockSpec((BLOCK, KV_WIDTH), prev_blk),
            pl.BlockSpec((tq, KV_WIDTH), row_tile),
            pl.BlockSpec((BLOCK, KV_WIDTH), next_blk),
            pl.BlockSpec((1, n_ctx, KV_WIDTH), ctx_blk),
            pl.BlockSpec((1, n_ctx, KV_WIDTH), ctx_blk),
            _layer_spec(wo.shape, attn_layer),
            _layer_spec(g.shape, layer),
            _layer_spec(b.shape, layer),
        ],
        out_specs=pl.BlockSpec((tq, D_MODEL), row_tile),
        out_shape=jax.ShapeDtypeStruct((m_rows, D_MODEL), F32),
        scratch_shapes=_attn_scratch(tq, 4 + n_ctx // BLOCK) + [pltpu.VMEM((4, BLOCK, BLOCK), F32)],
        compiler_params=_params(2),
        name="sample_attn",
    )(sink, x, mod, q, k, k, k, v, v, v, ctx_k, ctx_v, wo, g, b)


def _conv_kernel(xp_ref, x_ref, xn_ref, mod_ref, pw1_ref, pb1_ref, dw_ref, db_ref,
                 cg_ref, cb_ref, pw2_ref, pb2_ref, g_ref, b_ref, y_ref, *scratch,
                 alpha, tm, sub, tiles_per_seq, seq_per_sub):
    i = pl.program_id(0)
    t = i % tiles_per_seq
    m = mod_ref[0]
    lane_tiles = D_MODEL // LANES
    n_sub = tm // sub
    h_scrs, u_scrs, c_scrs = scratch[:n_sub], scratch[n_sub:2 * n_sub], scratch[2 * n_sub:]
    recompute_halo = not seq_per_sub

    def modulated_input(s):
        r0 = s * sub
        rows = x_ref[r0:r0 + sub, :]
        if recompute_halo:
            top = xp_ref[...] if s == 0 else x_ref[r0 - HALO:r0, :]
            bot = xn_ref[...] if s == n_sub - 1 else x_ref[r0 + sub:r0 + sub + HALO, :]
            rows = jnp.concatenate([top, rows, bot], axis=0)
        h_scrs[s][...] = (rows * (1.0 + m[1:2]) + m[0:1]).astype(BF16)

    def glu_chunk(s, chunk):
        width = CONV_CHUNK_TILES * LANES
        value_cols = slice(chunk * width, (chunk + 1) * width)
        gate_cols = slice(D_MODEL + chunk * width, D_MODEL + (chunk + 1) * width)
        h = h_scrs[s][...]
        value = _bdot(h, pw1_ref[:, value_cols]) + pb1_ref[:, value_cols]
        gate = _bdot(h, pw1_ref[:, gate_cols]) + pb1_ref[:, gate_cols]
        u_chunk = value * jax.nn.sigmoid(gate)
        zeros = jnp.zeros((HALO, LANES), F32)
        for j in range(CONV_CHUNK_TILES):
            c = chunk * CONV_CHUNK_TILES + j
            u = u_chunk[:, j * LANES:(j + 1) * LANES]
            if recompute_halo:
                u_scrs[s][c, :, :] = u
                if s == 0:
                    u_scrs[s][c, 0:HALO, :] = jnp.where(t > 0, u[:HALO], 0.0)
                if s == n_sub - 1:
                    u_scrs[s][c, HALO + sub:, :] = jnp.where(t < tiles_per_seq - 1, u[HALO + sub:], 0.0)
            else:
                u_scrs[s][c, 0:HALO, :] = zeros
                u_scrs[s][c, HALO:HALO + sub, :] = u
                u_scrs[s][c, HALO + sub:, :] = zeros

    def conv_taps(s, c):
        span_rows = SUBLANES * CONV_ROW_STRIDE
        for base in range(0, sub, CONV_ROW_BLOCK):
            starts = [base + sp + j for sp in range(0, CONV_ROW_BLOCK, span_rows)
                      for j in range(CONV_ROW_STRIDE)]
            bias = db_ref[c]
            accs = [jnp.zeros((SUBLANES, LANES), F32) + bias for _ in starts]
            for tap in range(CONV_WIDTH):
                w = dw_ref[c, tap:tap + 1, :]
                off = HALO - CONV_PAD + tap
                for n, start in enumerate(starts):
                    rows = pl.ds(start + off, SUBLANES, stride=CONV_ROW_STRIDE)
                    accs[n] = accs[n] + w * u_scrs[s][c, rows, :]
            for n, start in enumerate(starts):
                c_scrs[s][c, pl.ds(start, SUBLANES, stride=CONV_ROW_STRIDE), :] = accs[n]

    def finish(s):
        r0 = s * sub
        x = x_ref[r0:r0 + sub, :]
        conv = jnp.concatenate([c_scrs[s][c] for c in range(lane_tiles)], axis=1)
        un = _layer_norm(conv, cg_ref[...], cb_ref[...])
        un = un * jax.nn.sigmoid(un)
        out = _bdot(un.astype(BF16), pw2_ref[...]) + pb2_ref[...]
        z = alpha * x + m[2:3] * out
        y_ref[r0:r0 + sub, :] = _layer_norm(z, g_ref[...], b_ref[...])

    for s in range(n_sub):
        modulated_input(s)
    n_chunks = lane_tiles // CONV_CHUNK_TILES
    units = [(s, chunk) for s in range(n_sub) for chunk in range(n_chunks)]
    glu_chunk(*units[0])
    for n, (s, chunk) in enumerate(units):
        if n + 1 < len(units):
            glu_chunk(*units[n + 1])
        for j in range(CONV_CHUNK_TILES):
            conv_taps(s, chunk * CONV_CHUNK_TILES + j)
        if chunk == n_chunks - 1:
            finish(s)


def _conv_layer(x, mod, rows_per_group, seq, layer, conv_layer, pw1, pb1, dw, db, cg, cb, pw2, pb2,
                g, b, alpha):
    m_rows = x.shape[0]
    tm = CONV_TILE_ROWS
    sub = CONV_SUB_ROWS
    seq_per_sub = seq == sub
    assert seq_per_sub or seq % tm == 0
    tiles_per_seq = max(seq // tm, 1)
    in_rows = sub if seq_per_sub else sub + 2 * HALO
    halo_per_tile = tm // HALO
    n_halo_blocks = m_rows // HALO
    kern = functools.partial(_conv_kernel, alpha=alpha, tm=tm, sub=sub, tiles_per_seq=tiles_per_seq,
                             seq_per_sub=seq_per_sub)
    prev_blk = lambda i: (jnp.maximum(i * halo_per_tile - 1, 0), 0)
    next_blk = lambda i: (jnp.minimum((i + 1) * halo_per_tile, n_halo_blocks - 1), 0)
    return pl.pallas_call(
        kern,
        grid=(m_rows // tm,),
        in_specs=[
            pl.BlockSpec((HALO, D_MODEL), prev_blk),
            pl.BlockSpec((tm, D_MODEL), lambda i: (i, 0)),
            pl.BlockSpec((HALO, D_MODEL), next_blk),
            pl.BlockSpec((1, MOD_ROWS, D_MODEL), lambda i: (i * tm // rows_per_group, 0, 0)),
            _layer_spec(pw1.shape, conv_layer),
            _layer_spec(pb1.shape, conv_layer),
            _layer_spec(dw.shape, conv_layer),
            _layer_spec(db.shape, conv_layer),
            _layer_spec(cg.shape, conv_layer),
            _layer_spec(cb.shape, conv_layer),
            _layer_spec(pw2.shape, conv_layer),
            _layer_spec(pb2.shape, conv_layer),
            _layer_spec(g.shape, layer),
            _layer_spec(b.shape, layer),
        ],
        out_specs=pl.BlockSpec((tm, D_MODEL), lambda i: (i, 0)),
        out_shape=jax.ShapeDtypeStruct((m_rows, D_MODEL), F32),
        scratch_shapes=([pltpu.VMEM((in_rows, D_MODEL), BF16)] * (tm // sub)
                        + [pltpu.VMEM((D_MODEL // LANES, sub + 2 * HALO, LANES), F32)] * (tm // sub)
                        + [pltpu.VMEM((D_MODEL // LANES, sub, LANES), F32)] * (tm // sub)),
        compiler_params=_params(1),
        name="conv",
    )(x, x, x, mod, pw1, pb1, dw, db, cg, cb, pw2, pb2, g, b)


def kernel(x_prompt, x_sample, cache_k, cache_v, c, c_ctx, ada_w, ada_b, attn_w_qkv, attn_w_o, attn_sink, conv_pw1_w, conv_pw1_b, conv_dw_w, conv_dw_b, conv_norm_g, conv_norm_b, conv_pw2_w, conv_pw2_b, ln1_g, ln1_b, mlp_w1, mlp_b1, mlp_w2, mlp_b2, ln2_g, ln2_b):
    depth = ada_w.shape[0]
    bsz, seq, _ = x_prompt.shape
    dec_bsz, dec_seq, _ = x_sample.shape
    n_ctx = cache_k.shape[2]
    alpha = (2.0 * depth) ** 0.25
    assert 1 + dec_bsz <= COND_ROWS

    rows = lambda a: a.reshape(a.shape[0], 1, a.shape[-1])
    lane_tiles = D_MODEL // LANES

    cond = jnp.concatenate([c_ctx[None], c, jnp.zeros((COND_ROWS - 1 - dec_bsz, D_MODEL), F32)], axis=0)
    mods = _modulation(cond, 1 + dec_bsz, ada_w, ada_b).reshape(depth, COND_ROWS, N_MOD, D_MODEL)
    mods = jnp.pad(mods, ((0, 0), (0, 0), (0, MOD_ROWS - N_MOD), (0, 0)))

    cos, sin = _rope_tables(dec_seq)

    wqkv, wo = attn_w_qkv.astype(BF16), attn_w_o.astype(BF16)
    pw1, pw2 = conv_pw1_w.astype(BF16), conv_pw2_w.astype(BF16)
    w1, w2 = mlp_w1.astype(BF16), mlp_w2.astype(BF16)
    ln1 = (rows(ln1_g), rows(ln1_b))
    n_conv = conv_dw_w.shape[0]
    dw = conv_dw_w.reshape(n_conv, CONV_WIDTH, lane_tiles, LANES).transpose(0, 2, 1, 3)
    db = conv_dw_b.reshape(n_conv, lane_tiles, 1, LANES)
    conv_w = (pw1, rows(conv_pw1_b), dw, db, rows(conv_norm_g), rows(conv_norm_b), pw2, rows(conv_pw2_b))
    mlp_w = (w1, rows(mlp_b1), w2, rows(mlp_b2), rows(ln2_g), rows(ln2_b))

    yp = x_prompt
    ys = x_sample.reshape(dec_bsz * dec_seq, D_MODEL)
    new_k, new_v = [], []
    for i in range(depth):
        mod_p = mods[i, 0:1]
        mod_s = mods[i, 1:1 + dec_bsz]
        j = i // N_MIXERS
        if i % N_MIXERS == 0:
            yp, kp, vp = _prompt_attn_layer(yp, mod_p, attn_sink[j], i, j, wqkv, wo, *ln1, alpha)
            new_k.append(kp.reshape(bsz, seq, N_KV_HEADS, HEAD_DIM))
            new_v.append(vp.reshape(bsz, seq, N_KV_HEADS, HEAD_DIM))
            q, k, v = _sample_qkv(ys, mod_s, dec_seq, cos, sin, j, wqkv)
            ys = _sample_attn_layer(ys, mod_s, dec_seq, q, k, v,
                                    cache_k[:, j].reshape(dec_bsz, n_ctx, KV_WIDTH),
                                    cache_v[:, j].reshape(dec_bsz, n_ctx, KV_WIDTH),
                                    attn_sink[j], i, j, wo, *ln1, alpha)
        else:
            yp = _conv_layer(yp.reshape(bsz * seq, D_MODEL), mod_p, bsz * seq, seq, i, j, *conv_w, *ln1,
                             alpha).reshape(bsz, seq, D_MODEL)
            ys = _conv_layer(ys, mod_s, dec_seq, dec_seq, i, j, *conv_w, *ln1, alpha)
        yp = _mlp(yp.reshape(bsz * seq, D_MODEL), mod_p, bsz * seq, i, *mlp_w, alpha).reshape(bsz, seq, D_MODEL)
        ys = _mlp(ys, mod_s, dec_seq, i, *mlp_w, alpha)
    new_cache_k = jnp.stack(new_k, axis=1)
    new_cache_v = jnp.stack(new_v, axis=1)
    return (yp, ys.reshape(dec_bsz, dec_seq, D_MODEL), new_cache_k, new_cache_v)
```

```python
import functools
import math

import jax
import jax.numpy as jnp
from jax import lax
from jax.experimental import pallas as pl
from jax.experimental.pallas import tpu as pltpu

D_MODEL = 1024
GRID_W = 64
N_MIXERS = 2
HEAD_DIM = 64
N_HEADS = D_MODEL // HEAD_DIM
N_KV_HEADS = 4
GROUP = N_HEADS // N_KV_HEADS
KV_WIDTH = N_KV_HEADS * HEAD_DIM
QK_WIDTH = D_MODEL + KV_WIDTH
QKV_WIDTH = D_MODEL + 2 * KV_WIDTH
BLOCK = 128
ROPE_THETA = 10000.0
CONV_WIDTH = 31
CONV_PAD = (CONV_WIDTH - 1) // 2
D_FF = 4 * D_MODEL
N_MOD = 6
LN_EPS = 1e-5
NEG_INF = -1e30
ATTN_SCALE = HEAD_DIM ** -0.5
LOG2E = math.log2(math.e)
Q_SCALE = ATTN_SCALE * LOG2E

LANES = 128
SUBLANES = 8
VMEM_LIMIT_BYTES = 56 * 1024 * 1024

MOD_ROWS = 8
COND_ROWS = 8
HALO = 16
CONV_ROW_STRIDE = 4
CONV_ROW_BLOCK = 128
CONV_TILE_ROWS = 512
CONV_SUB_ROWS = 256
CONV_CHUNK_TILES = 2
SOFTMAX_ROWS = 64
MOD_TILE_COLS = 3072
MLP_TILE_ROWS = 512
MLP_SUB_ROWS = 256

BF16 = jnp.bfloat16
F32 = jnp.float32


def _const_spec(shape):
    n = len(shape)
    return pl.BlockSpec(shape, lambda *_: (0,) * n, pipeline_mode=pl.Buffered(1))


def _layer_spec(shape, layer):
    n = len(shape)
    return pl.BlockSpec((None,) + tuple(shape[1:]), lambda *_: (layer,) + (0,) * (n - 1),
                        pipeline_mode=pl.Buffered(1))


def _params(n_axes):
    return pltpu.CompilerParams(dimension_semantics=("arbitrary",) * n_axes,
                                vmem_limit_bytes=VMEM_LIMIT_BYTES)


def _layer_norm(z, g, b):
    mu = jnp.mean(z, axis=-1, keepdims=True)
    zc = z - mu
    var = jnp.mean(zc * zc, axis=-1, keepdims=True)
    return zc * lax.rsqrt(var + LN_EPS) * g + b


def _bdot(a, b):
    return jnp.dot(a, b, preferred_element_type=F32)


def _bdot_nt(a, b):
    return lax.dot_general(a, b, (((1,), (1,)), ((), ())), preferred_element_type=F32)


def _mod_kernel(cond_ref, w_ref, b_ref, out_ref, s_scr, *, n_cond):
    @pl.when((pl.program_id(0) == 0) & (pl.program_id(1) == 0))
    def _():
        c = cond_ref[...]
        s_t = (c * jax.nn.sigmoid(c)).T
        for m in range(n_cond):
            s_scr[m] = jnp.broadcast_to(s_t[:, m:m + 1], (D_MODEL, LANES))

    zero_rows = [jnp.zeros((1, LANES), F32)] * (COND_ROWS - n_cond)
    for n in range(w_ref.shape[2] // LANES):
        cols = slice(n * LANES, (n + 1) * LANES)
        accs = [jnp.zeros((SUBLANES, LANES), F32) for _ in range(n_cond)]
        for k in range(0, D_MODEL, SUBLANES):
            w = w_ref[0, k:k + SUBLANES, cols]
            for m in range(n_cond):
                accs[m] = accs[m] + w * s_scr[m, k:k + SUBLANES, :]
        out_rows = [jnp.sum(a, axis=0, keepdims=True) for a in accs] + zero_rows
        out_ref[0, :, cols] = jnp.concatenate(out_rows, axis=0) + b_ref[0, :, cols]


def _modulation(cond, n_cond, ada_w, ada_b):
    depth = ada_w.shape[0]
    tn = MOD_TILE_COLS
    width = N_MOD * D_MODEL
    return pl.pallas_call(
        functools.partial(_mod_kernel, n_cond=n_cond),
        grid=(depth, width // tn),
        in_specs=[
            pl.BlockSpec((COND_ROWS, D_MODEL), lambda l, n: (0, 0)),
            pl.BlockSpec((1, D_MODEL, tn), lambda l, n: (l, 0, n)),
            pl.BlockSpec((1, 1, tn), lambda l, n: (l, 0, n)),
        ],
        out_specs=pl.BlockSpec((1, COND_ROWS, tn), lambda l, n: (l, 0, n)),
        out_shape=jax.ShapeDtypeStruct((depth, COND_ROWS, width), F32),
        scratch_shapes=[pltpu.VMEM((n_cond, D_MODEL, LANES), F32)],
        compiler_params=_params(2),
        name="modulation",
    )(cond, ada_w, ada_b.reshape(depth, 1, width))


def _mlp_kernel(y_ref, mod_ref, w1_ref, b1_ref, w2_ref, b2_ref, g_ref, b_ref, out_ref, *,
                alpha, ff_chunk):
    m = mod_ref[0]
    for r0 in range(0, y_ref.shape[0], MLP_SUB_ROWS):
        rows = slice(r0, r0 + MLP_SUB_ROWS)
        y = y_ref[rows, :]
        h = (y * (1.0 + m[4:5]) + m[3:4]).astype(BF16)
        acc = jnp.zeros(y.shape, F32)
        for c in range(D_FF // ff_chunk):
            cols = slice(c * ff_chunk, (c + 1) * ff_chunk)
            a = _bdot(h, w1_ref[:, cols]) + b1_ref[:, cols]
            a = jnp.maximum(a, 0.0)
            acc = acc + _bdot((a * a).astype(BF16), w2_ref[cols, :])
        z = alpha * y + m[5:6] * (acc + b2_ref[...])
        out_ref[rows, :] = _layer_norm(z, g_ref[...], b_ref[...])


def _mlp(y, mod, rows_per_group, layer, w1, b1, w2, b2, g, b, alpha):
    m_rows = y.shape[0]
    tm = MLP_TILE_ROWS
    kern = functools.partial(_mlp_kernel, alpha=alpha, ff_chunk=1024)
    return pl.pallas_call(
        kern,
        grid=(m_rows // tm,),
        in_specs=[
            pl.BlockSpec((tm, D_MODEL), lambda i: (i, 0)),
            pl.BlockSpec((1, MOD_ROWS, D_MODEL), lambda i: (i * tm // rows_per_group, 0, 0)),
            _layer_spec(w1.shape, layer),
            _layer_spec(b1.shape, layer),
            _layer_spec(w2.shape, layer),
            _layer_spec(b2.shape, layer),
            _layer_spec(g.shape, layer),
            _layer_spec(b.shape, layer),
        ],
        out_specs=pl.BlockSpec((tm, D_MODEL), lambda i: (i, 0)),
        out_shape=jax.ShapeDtypeStruct((m_rows, D_MODEL), F32),
        compiler_params=_params(1),
        name="mlp",
    )(y, mod, w1, b1, w2, b2, g, b)


def _head_pair_operand(x, kv_head):
    tile = x[:, LANES * (kv_head // 2):LANES * (kv_head // 2 + 1)]
    swapped = pltpu.roll(tile, HEAD_DIM, axis=1)
    lane = lax.broadcasted_iota(jnp.int32, (1, LANES), 1)
    low = lane < HEAD_DIM
    if kv_head % 2 == 0:
        lo_src, hi_src = tile, swapped
    else:
        lo_src, hi_src = swapped, tile
    lo = jnp.where(low, lo_src, 0.0).astype(BF16)
    hi = jnp.where(low, 0.0, hi_src).astype(BF16)
    pieces = []
    for blk in range(x.shape[0] // BLOCK):
        pieces += [lo[blk * BLOCK:(blk + 1) * BLOCK], hi[blk * BLOCK:(blk + 1) * BLOCK]]
    return jnp.concatenate(pieces, axis=0)


def _pair_softmax(s_ref, p_ref, stat_ref, sinks, row_plan):
    def load(rows, block, head, plan_row, biases):
        tile = 2 * block + head
        v = s_ref[rows, tile * LANES:(tile + 1) * LANES]
        if block in biases:
            v = v + biases[block][plan_row:plan_row + SOFTMAX_ROWS, :]
        return v

    for row0, n_rows, blocks, biases in row_plan:
        for r in range(0, n_rows, SOFTMAX_ROWS):
            rows = slice(row0 + r, row0 + r + SOFTMAX_ROWS)
            for head in range(2):
                acc = None
                for block in blocks:
                    v = load(rows, block, head, r, biases)
                    acc = v if acc is None else jnp.maximum(acc, v)
                stat_ref[0, rows, head * LANES:(head + 1) * LANES] = acc
    mx = [jnp.maximum(jnp.max(stat_ref[0, :, head * LANES:(head + 1) * LANES], axis=-1, keepdims=True),
                      sinks[head]) for head in range(2)]
    for row0, n_rows, blocks, biases in row_plan:
        for r in range(0, n_rows, SOFTMAX_ROWS):
            rows = slice(row0 + r, row0 + r + SOFTMAX_ROWS)
            for head in range(2):
                m_rows = mx[head][row0 + r:row0 + r + SOFTMAX_ROWS]
                acc = None
                for block in blocks:
                    p = jnp.exp2(load(rows, block, head, r, biases) - m_rows)
                    acc = p if acc is None else acc + p
                    tile = 2 * block + head
                    p_ref[rows, tile * LANES:(tile + 1) * LANES] = p.astype(BF16)
                stat_ref[1, rows, head * LANES:(head + 1) * LANES] = acc
    recip = [1.0 / (jnp.sum(stat_ref[1, :, head * LANES:(head + 1) * LANES], axis=-1, keepdims=True)
                    + jnp.exp2(sinks[head] - mx[head])) for head in range(2)]
    lane = lax.broadcasted_iota(jnp.int32, (1, LANES), 1)
    return jnp.where(lane < HEAD_DIM, recip[0], recip[1])


def _attention_pipeline(units, sink_ref, o_scr, stat_scr):
    def scores(u):
        u["s_ref"][...] = _bdot_nt(u["q"](), u["kk"]())

    def values(u, recip):
        o = _bdot(u["p_ref"][...], u["vv"]()) * recip
        o_scr[u["rows"], LANES * u["t"]:LANES * (u["t"] + 1)] = o.astype(BF16)

    scores(units[0])
    recip_prev = None
    for n, u in enumerate(units):
        if n + 1 < len(units):
            scores(units[n + 1])
        if n > 0:
            values(units[n - 1], recip_prev)
        sinks = (sink_ref[2 * u["t"]] * LOG2E, sink_ref[2 * u["t"] + 1] * LOG2E)
        recip_prev = _pair_softmax(u["s_ref"], u["p_ref"], stat_scr.at[n % 2], sinks, u["plan"])
    values(units[-1], recip_prev)


def _cached(fn):
    memo = {}

    def get(*key):
        if key not in memo:
            memo[key] = fn(*key)
        return memo[key]
    return get


def _prompt_attn_kernel(sink_ref, x_ref, mod_ref, wqkv_ref, wo_ref, g_ref, b_ref,
                        y_ref, k_ref, v_ref, o_scr, s_scr, p_scr, stat_scr, *, alpha):
    x = x_ref[0]
    seq = x.shape[0]
    m = mod_ref[0]
    h = (x * (1.0 + m[1:2]) + m[0:1]).astype(BF16)
    qkv = _bdot(h, wqkv_ref[...])
    k = qkv[:, D_MODEL:QK_WIDTH]
    v = qkv[:, QK_WIDTH:]
    k_ref[0] = k
    v_ref[0] = v
    q = (qkv[:, :D_MODEL] * Q_SCALE).astype(BF16)
    plan = [(0, seq, list(range(seq // BLOCK)), {})]
    kk = _cached(lambda kv_head: _head_pair_operand(k, kv_head))
    vv = _cached(lambda kv_head: _head_pair_operand(v, kv_head))
    units = [dict(q=functools.partial(lambda t: q[:, LANES * t:LANES * (t + 1)], t),
                  kk=functools.partial(kk, t // (GROUP // 2)), vv=functools.partial(vv, t // (GROUP // 2)),
                  t=t, rows=slice(0, seq), s_ref=s_scr.at[t % 2], p_ref=p_scr.at[t % 2], plan=plan)
             for t in range(D_MODEL // LANES)]
    _attention_pipeline(units, sink_ref, o_scr, stat_scr)
    out = _bdot(o_scr[...], wo_ref[...])
    z = alpha * x + m[2:3] * out
    y_ref[0] = _layer_norm(z, g_ref[...], b_ref[...])


def _attn_scratch(rows, n_key_blocks):
    width = 2 * n_key_blocks * LANES
    return [pltpu.VMEM((rows, D_MODEL), BF16),
            pltpu.VMEM((2, rows, width), F32),
            pltpu.VMEM((2, rows, width), BF16),
            pltpu.VMEM((2, 2, rows, 2 * LANES), F32)]


def _prompt_attn_layer(x, mod, sink, layer, attn_layer, wqkv, wo, g, b, alpha):
    bsz, seq, _ = x.shape
    kern = functools.partial(_prompt_attn_kernel, alpha=alpha)
    return pl.pallas_call(
        kern,
        grid=(bsz,),
        in_specs=[
            pl.BlockSpec(memory_space=pltpu.SMEM),
            pl.BlockSpec((1, seq, D_MODEL), lambda i: (i, 0, 0)),
            pl.BlockSpec((1, MOD_ROWS, D_MODEL), lambda i: (0, 0, 0)),
            _layer_spec(wqkv.shape, attn_layer),
            _layer_spec(wo.shape, attn_layer),
            _layer_spec(g.shape, layer),
            _layer_spec(b.shape, layer),
        ],
        out_specs=[
            pl.BlockSpec((1, seq, D_MODEL), lambda i: (i, 0, 0)),
            pl.BlockSpec((1, seq, KV_WIDTH), lambda i: (i, 0, 0)),
            pl.BlockSpec((1, seq, KV_WIDTH), lambda i: (i, 0, 0)),
        ],
        out_shape=[
            jax.ShapeDtypeStruct((bsz, seq, D_MODEL), F32),
            jax.ShapeDtypeStruct((bsz, seq, KV_WIDTH), F32),
            jax.ShapeDtypeStruct((bsz, seq, KV_WIDTH), F32),
        ],
        scratch_shapes=_attn_scratch(seq, seq // BLOCK),
        compiler_params=_params(1),
        name="prompt_attn",
    )(sink, x, mod, wqkv, wo, g, b)


def _rope_tables(n_tokens):
    rows = n_tokens // GRID_W
    row = jnp.repeat(jnp.arange(rows, dtype=F32), GRID_W)
    col = jnp.tile(jnp.arange(GRID_W, dtype=F32), rows)
    half = HEAD_DIM // 2
    inv = ROPE_THETA ** (-jnp.arange(0, half, 2, dtype=F32) / half)
    ang_r = row[:, None] * inv
    ang_c = col[:, None] * inv
    cos = jnp.concatenate([jnp.cos(ang_r)] * 2 + [jnp.cos(ang_c)] * 2, axis=-1)
    sin = jnp.concatenate([-jnp.sin(ang_r), jnp.sin(ang_r), -jnp.sin(ang_c), jnp.sin(ang_c)], axis=-1)
    return jnp.tile(cos, (1, LANES // HEAD_DIM)), jnp.tile(sin, (1, LANES // HEAD_DIM))


def _sample_qkv_kernel(x_ref, mod_ref, cos_ref, sin_ref, wqkv_ref, q_ref, kv_ref):
    x = x_ref[...]
    m = mod_ref[0]
    h = (x * (1.0 + m[1:2]) + m[0:1]).astype(BF16)
    qkv = _bdot(h, wqkv_ref[...])
    cos = cos_ref[...]
    sin = sin_ref[...]
    lane = lax.broadcasted_iota(jnp.int32, (1, LANES), 1)
    first = (lane % (HEAD_DIM // 2)) < (HEAD_DIM // 4)
    quarter = HEAD_DIM // 4
    for t in range(QK_WIDTH // LANES):
        tile = qkv[:, LANES * t:LANES * (t + 1)]
        partner = jnp.where(first, pltpu.roll(tile, LANES - quarter, axis=1),
                            pltpu.roll(tile, quarter, axis=1))
        rot = tile * cos + partner * sin
        if t < D_MODEL // LANES:
            q_ref[:, LANES * t:LANES * (t + 1)] = (rot * Q_SCALE).astype(BF16)
        else:
            tk = t - D_MODEL // LANES
            kv_ref[:, LANES * tk:LANES * (tk + 1)] = rot.astype(BF16)
    kv_ref[:, KV_WIDTH:] = qkv[:, QK_WIDTH:].astype(BF16)


def _sample_qkv(x, mod, seq, cos, sin, attn_layer, wqkv):
    m_rows = x.shape[0]
    tm = 512
    tiles_per_seq = seq // tm
    return pl.pallas_call(
        _sample_qkv_kernel,
        grid=(m_rows // tm,),
        in_specs=[
            pl.BlockSpec((tm, D_MODEL), lambda i: (i, 0)),
            pl.BlockSpec((1, MOD_ROWS, D_MODEL), lambda i: (i // tiles_per_seq, 0, 0)),
            pl.BlockSpec((tm, LANES), lambda i: (i % tiles_per_seq, 0)),
            pl.BlockSpec((tm, LANES), lambda i: (i % tiles_per_seq, 0)),
            _layer_spec(wqkv.shape, attn_layer),
        ],
        out_specs=[
            pl.BlockSpec((tm, D_MODEL), lambda i: (i, 0)),
            pl.BlockSpec((tm, 2 * KV_WIDTH), lambda i: (i, 0)),
        ],
        out_shape=[
            jax.ShapeDtypeStruct((m_rows, D_MODEL), BF16),
            jax.ShapeDtypeStruct((m_rows, 2 * KV_WIDTH), BF16),
        ],
        compiler_params=_params(1),
        name="sample_qkv",
    )(x, mod, cos, sin, wqkv)


def _sample_attn_kernel(sink_ref, x_ref, mod_ref, q_ref, kvp_ref, kvm_ref, kvn_ref, ctx_ref,
                        wo_ref, g_ref, b_ref, y_ref, o_scr, s_scr, p_scr, stat_scr, bias_scr, *,
                        alpha):
    i = pl.program_id(1)
    n_tiles = pl.num_programs(1)
    m = mod_ref[0]
    nc = ctx_ref.shape[1] // BLOCK
    window = 4 + nc
    half_rows = 2 * BLOCK
    kv_all = jnp.concatenate([kvp_ref[...].astype(F32), kvm_ref[0:BLOCK, :].astype(F32), ctx_ref[0],
                              kvm_ref[BLOCK:, :].astype(F32), kvn_ref[...].astype(F32)], axis=0)
    k_all, v_all = kv_all[:, :KV_WIDTH], kv_all[:, KV_WIDTH:]
    qi = lax.broadcasted_iota(jnp.int32, (BLOCK, BLOCK), 0)
    ki = lax.broadcasted_iota(jnp.int32, (BLOCK, BLOCK), 1)
    bias_scr[0] = jnp.where(ki >= qi, jnp.where(i > 0, 0.0, NEG_INF), NEG_INF)
    bias_scr[1] = jnp.where(ki <= qi, 0.0, NEG_INF)
    bias_scr[2] = jnp.where(ki >= qi, 0.0, NEG_INF)
    bias_scr[3] = jnp.where(ki <= qi, jnp.where(i < n_tiles - 1, 0.0, NEG_INF), NEG_INF)
    prev_edge, next_plain, prev_plain, next_edge = (bias_scr.at[j] for j in range(4))
    ctx_a = list(range(2, 2 + nc))
    ctx_b = list(range(nc))
    plans = [
        [(0, BLOCK, [0, 1] + ctx_a + [2 + nc], {0: prev_edge, 2 + nc: next_plain}),
         (BLOCK, BLOCK, [1] + ctx_a + [2 + nc, 3 + nc], {1: prev_plain, 3 + nc: next_plain})],
        [(0, BLOCK, ctx_b + [nc, nc + 1, nc + 2], {nc: prev_plain, nc + 2: next_plain}),
         (BLOCK, BLOCK, ctx_b + [nc + 1, nc + 2, nc + 3], {nc + 1: prev_plain, nc + 3: next_edge})],
    ]
    unused = [[3 + nc, 0], [nc + 3, nc]]
    for half in range(2):
        for parity in range(2):
            for qb in range(2):
                blk = unused[half][qb]
                p_scr[2 * half + parity, qb * BLOCK:(qb + 1) * BLOCK,
                      2 * blk * LANES:2 * (blk + 1) * LANES] = jnp.zeros((BLOCK, 2 * LANES), BF16)
    kk = _cached(lambda kv_head: _head_pair_operand(k_all, kv_head))
    vv = _cached(lambda kv_head: _head_pair_operand(v_all, kv_head))

    def operand(full, kv_head, half):
        first = 2 * half * 2 * BLOCK
        return full(kv_head)[first:first + window * 2 * BLOCK]

    units = []
    for kv_head in range(N_KV_HEADS):
        for half in range(2):
            for t in range(kv_head * (GROUP // 2), (kv_head + 1) * (GROUP // 2)):
                n = len(units)
                rows = slice(half * half_rows, (half + 1) * half_rows)
                units.append(dict(
                    q=functools.partial(lambda rows, t: q_ref[rows, LANES * t:LANES * (t + 1)], rows, t),
                    kk=functools.partial(operand, kk, kv_head, half),
                    vv=functools.partial(operand, vv, kv_head, half),
                    t=t, rows=rows, s_ref=s_scr.at[n % 2], p_ref=p_scr.at[2 * half + n % 2],
                    plan=plans[half]))
    _attention_pipeline(units, sink_ref, o_scr, stat_scr)
    out = _bdot(o_scr[...], wo_ref[...])
    z = alpha * x_ref[...] + m[2:3] * out
    y_ref[...] = _layer_norm(z, g_ref[...], b_ref[...])


def _sample_attn_layer(x, mod, seq, q, kv, ctx_kv, sink, layer, attn_layer, wo, g, b, alpha):
    m_rows = x.shape[0]
    bsz = m_rows // seq
    tq = 4 * BLOCK
    tiles = seq // tq
    blocks = seq // BLOCK
    n_ctx = ctx_kv.shape[1]
    assert n_ctx % BLOCK == 0 and seq % tq == 0
    window = 4 + n_ctx // BLOCK
    kern = functools.partial(_sample_attn_kernel, alpha=alpha)
    row_tile = lambda bi, i: (bi * tiles + i, 0)
    prev_blk = lambda bi, i: (bi * blocks + jnp.maximum(4 * i - 1, 0), 0)
    next_blk = lambda bi, i: (bi * blocks + jnp.minimum(4 * i + 4, blocks - 1), 0)
    ctx_blk = lambda bi, i: (bi, 0, 0)
    half_rows = 2 * BLOCK
    width = 2 * window * LANES
    return pl.pallas_call(
        kern,
        grid=(bsz, tiles),
        in_specs=[
            pl.BlockSpec(memory_space=pltpu.SMEM),
            pl.BlockSpec((tq, D_MODEL), row_tile),
            pl.BlockSpec((1, MOD_ROWS, D_MODEL), ctx_blk),
            pl.BlockSpec((tq, D_MODEL), row_tile),
            pl.BlockSpec((BLOCK, 2 * KV_WIDTH), prev_blk),
            pl.BlockSpec((tq, 2 * KV_WIDTH), row_tile),
            pl.BlockSpec((BLOCK, 2 * KV_WIDTH), next_blk),
            pl.BlockSpec((1, n_ctx, 2 * KV_WIDTH), ctx_blk),
            _layer_spec(wo.shape, attn_layer),
            _layer_spec(g.shape, layer),
            _layer_spec(b.shape, layer),
        ],
        out_specs=pl.BlockSpec((tq, D_MODEL), row_tile),
        out_shape=jax.ShapeDtypeStruct((m_rows, D_MODEL), F32),
        scratch_shapes=[pltpu.VMEM((tq, D_MODEL), BF16),
                        pltpu.VMEM((2, half_rows, width), F32),
                        pltpu.VMEM((4, half_rows, width), BF16),
                        pltpu.VMEM((2, 2, half_rows, 2 * LANES), F32),
                        pltpu.VMEM((4, BLOCK, BLOCK), F32)],
        compiler_params=_params(2),
        name="sample_attn",
    )(sink, x, mod, q, kv, kv, kv, ctx_kv, wo, g, b)


def _conv_kernel(xp_ref, x_ref, xn_ref, mod_ref, pw1_ref, pb1_ref, dw_ref, db_ref,
                 cg_ref, cb_ref, pw2_ref, pb2_ref, g_ref, b_ref, y_ref, *scratch,
                 alpha, tm, sub, tiles_per_seq, seq_per_sub):
    i = pl.program_id(0)
    t = i % tiles_per_seq
    m = mod_ref[0]
    lane_tiles = D_MODEL // LANES
    n_sub = tm // sub
    h_scrs, u_scrs, c_scrs = scratch[:n_sub], scratch[n_sub:2 * n_sub], scratch[2 * n_sub:]
    recompute_halo = not seq_per_sub

    def modulated_input(s):
        r0 = s * sub
        rows = x_ref[r0:r0 + sub, :]
        if recompute_halo:
            top = xp_ref[...] if s == 0 else x_ref[r0 - HALO:r0, :]
            bot = xn_ref[...] if s == n_sub - 1 else x_ref[r0 + sub:r0 + sub + HALO, :]
            rows = jnp.concatenate([top, rows, bot], axis=0)
        h_scrs[s][...] = (rows * (1.0 + m[1:2]) + m[0:1]).astype(BF16)

    def glu_chunk(s, chunk):
        width = CONV_CHUNK_TILES * LANES
        value_cols = slice(chunk * width, (chunk + 1) * width)
        gate_cols = slice(D_MODEL + chunk * width, D_MODEL + (chunk + 1) * width)
        h = h_scrs[s][...]
        value = _bdot(h, pw1_ref[:, value_cols]) + pb1_ref[:, value_cols]
        gate = _bdot(h, pw1_ref[:, gate_cols]) + pb1_ref[:, gate_cols]
        u_chunk = value * jax.nn.sigmoid(gate)
        zeros = jnp.zeros((HALO, LANES), F32)
        for j in range(CONV_CHUNK_TILES):
            c = chunk * CONV_CHUNK_TILES + j
            u = u_chunk[:, j * LANES:(j + 1) * LANES]
            if recompute_halo:
                u_scrs[s][c, :, :] = u
                if s == 0:
                    u_scrs[s][c, 0:HALO, :] = jnp.where(t > 0, u[:HALO], 0.0)
                if s == n_sub - 1:
                    u_scrs[s][c, HALO + sub:, :] = jnp.where(t < tiles_per_seq - 1, u[HALO + sub:], 0.0)
            else:
                u_scrs[s][c, 0:HALO, :] = zeros
                u_scrs[s][c, HALO:HALO + sub, :] = u
                u_scrs[s][c, HALO + sub:, :] = zeros

    def conv_taps(s, c):
        span_rows = SUBLANES * CONV_ROW_STRIDE
        for base in range(0, sub, CONV_ROW_BLOCK):
            starts = [base + sp + j for sp in range(0, CONV_ROW_BLOCK, span_rows)
                      for j in range(CONV_ROW_STRIDE)]
            bias = db_ref[c]
            accs = [jnp.zeros((SUBLANES, LANES), F32) + bias for _ in starts]
            for tap in range(CONV_WIDTH):
                w = dw_ref[c, tap:tap + 1, :]
                off = HALO - CONV_PAD + tap
                for n, start in enumerate(starts):
                    rows = pl.ds(start + off, SUBLANES, stride=CONV_ROW_STRIDE)
                    accs[n] = accs[n] + w * u_scrs[s][c, rows, :]
            for n, start in enumerate(starts):
                c_scrs[s][c, pl.ds(start, SUBLANES, stride=CONV_ROW_STRIDE), :] = accs[n]

    def finish(s):
        r0 = s * sub
        x = x_ref[r0:r0 + sub, :]
        conv = jnp.concatenate([c_scrs[s][c] for c in range(lane_tiles)], axis=1)
        un = _layer_norm(conv, cg_ref[...], cb_ref[...])
        un = un * jax.nn.sigmoid(un)
        out = _bdot(un.astype(BF16), pw2_ref[...]) + pb2_ref[...]
        z = alpha * x + m[2:3] * out
        y_ref[r0:r0 + sub, :] = _layer_norm(z, g_ref[...], b_ref[...])

    for s in range(n_sub):
        modulated_input(s)
    n_chunks = lane_tiles // CONV_CHUNK_TILES
    units = [(s, chunk) for s in range(n_sub) for chunk in range(n_chunks)]
    glu_chunk(*units[0])
    for n, (s, chunk) in enumerate(units):
        if n + 1 < len(units):
            glu_chunk(*units[n + 1])
        for j in range(CONV_CHUNK_TILES):
            conv_taps(s, chunk * CONV_CHUNK_TILES + j)
        if chunk == n_chunks - 1:
            finish(s)


def _conv_layer(x, mod, rows_per_group, seq, layer, conv_layer, pw1, pb1, dw, db, cg, cb, pw2, pb2,
                g, b, alpha):
    m_rows = x.shape[0]
    tm = CONV_TILE_ROWS
    sub = CONV_SUB_ROWS
    seq_per_sub = seq == sub
    assert seq_per_sub or seq % tm == 0
    tiles_per_seq = max(seq // tm, 1)
    in_rows = sub if seq_per_sub else sub + 2 * HALO
    halo_per_tile = tm // HALO
    n_halo_blocks = m_rows // HALO
    kern = functools.partial(_conv_kernel, alpha=alpha, tm=tm, sub=sub, tiles_per_seq=tiles_per_seq,
                             seq_per_sub=seq_per_sub)
    prev_blk = lambda i: (jnp.maximum(i * halo_per_tile - 1, 0), 0)
    next_blk = lambda i: (jnp.minimum((i + 1) * halo_per_tile, n_halo_blocks - 1), 0)
    return pl.pallas_call(
        kern,
        grid=(m_rows // tm,),
        in_specs=[
            pl.BlockSpec((HALO, D_MODEL), prev_blk),
            pl.BlockSpec((tm, D_MODEL), lambda i: (i, 0)),
            pl.BlockSpec((HALO, D_MODEL), next_blk),
            pl.BlockSpec((1, MOD_ROWS, D_MODEL), lambda i: (i * tm // rows_per_group, 0, 0)),
            _layer_spec(pw1.shape, conv_layer),
            _layer_spec(pb1.shape, conv_layer),
            _layer_spec(dw.shape, conv_layer),
            _layer_spec(db.shape, conv_layer),
            _layer_spec(cg.shape, conv_layer),
            _layer_spec(cb.shape, conv_layer),
            _layer_spec(pw2.shape, conv_layer),
            _layer_spec(pb2.shape, conv_layer),
            _layer_spec(g.shape, layer),
            _layer_spec(b.shape, layer),
        ],
        out_specs=pl.BlockSpec((tm, D_MODEL), lambda i: (i, 0)),
        out_shape=jax.ShapeDtypeStruct((m_rows, D_MODEL), F32),
        scratch_shapes=([pltpu.VMEM((in_rows, D_MODEL), BF16)] * (tm // sub)
                        + [pltpu.VMEM((D_MODEL // LANES, sub + 2 * HALO, LANES), F32)] * (tm // sub)
                        + [pltpu.VMEM((D_MODEL // LANES, sub, LANES), F32)] * (tm // sub)),
        compiler_params=_params(1),
        name="conv",
    )(x, x, x, mod, pw1, pb1, dw, db, cg, cb, pw2, pb2, g, b)


def kernel(x_prompt, x_sample, cache_k, cache_v, c, c_ctx, ada_w, ada_b, attn_w_qkv, attn_w_o, attn_sink, conv_pw1_w, conv_pw1_b, conv_dw_w, conv_dw_b, conv_norm_g, conv_norm_b, conv_pw2_w, conv_pw2_b, ln1_g, ln1_b, mlp_w1, mlp_b1, mlp_w2, mlp_b2, ln2_g, ln2_b):
    depth = ada_w.shape[0]
    bsz, seq, _ = x_prompt.shape
    dec_bsz, dec_seq, _ = x_sample.shape
    n_ctx = cache_k.shape[2]
    alpha = (2.0 * depth) ** 0.25
    assert 1 + dec_bsz <= COND_ROWS

    rows = lambda a: a.reshape(a.shape[0], 1, a.shape[-1])
    lane_tiles = D_MODEL // LANES

    cond = jnp.concatenate([c_ctx[None], c, jnp.zeros((COND_ROWS - 1 - dec_bsz, D_MODEL), F32)], axis=0)
    mods = _modulation(cond, 1 + dec_bsz, ada_w, ada_b).reshape(depth, COND_ROWS, N_MOD, D_MODEL)
    mods = jnp.pad(mods, ((0, 0), (0, 0), (0, MOD_ROWS - N_MOD), (0, 0)))

    cos, sin = _rope_tables(dec_seq)

    wqkv, wo = attn_w_qkv.astype(BF16), attn_w_o.astype(BF16)
    pw1, pw2 = conv_pw1_w.astype(BF16), conv_pw2_w.astype(BF16)
    w1, w2 = mlp_w1.astype(BF16), mlp_w2.astype(BF16)
    ln1 = (rows(ln1_g), rows(ln1_b))
    n_conv = conv_dw_w.shape[0]
    dw = conv_dw_w.reshape(n_conv, CONV_WIDTH, lane_tiles, LANES).transpose(0, 2, 1, 3)
    db = conv_dw_b.reshape(n_conv, lane_tiles, 1, LANES)
    conv_w = (pw1, rows(conv_pw1_b), dw, db, rows(conv_norm_g), rows(conv_norm_b), pw2, rows(conv_pw2_b))
    mlp_w = (w1, rows(mlp_b1), w2, rows(mlp_b2), rows(ln2_g), rows(ln2_b))

    yp = x_prompt
    ys = x_sample.reshape(dec_bsz * dec_seq, D_MODEL)
    new_k, new_v = [], []
    for i in range(depth):
        mod_p = mods[i, 0:1]
        mod_s = mods[i, 1:1 + dec_bsz]
        j = i // N_MIXERS
        if i % N_MIXERS == 0:
            yp, kp, vp = _prompt_attn_layer(yp, mod_p, attn_sink[j], i, j, wqkv, wo, *ln1, alpha)
            new_k.append(kp.reshape(bsz, seq, N_KV_HEADS, HEAD_DIM))
            new_v.append(vp.reshape(bsz, seq, N_KV_HEADS, HEAD_DIM))
            q, kv = _sample_qkv(ys, mod_s, dec_seq, cos, sin, j, wqkv)
            ctx_kv = jnp.concatenate([cache_k[:, j].reshape(dec_bsz, n_ctx, KV_WIDTH),
                                      cache_v[:, j].reshape(dec_bsz, n_ctx, KV_WIDTH)], axis=-1)
            ys = _sample_attn_layer(ys, mod_s, dec_seq, q, kv, ctx_kv, attn_sink[j], i, j, wo, *ln1, alpha)
        else:
            yp = _conv_layer(yp.reshape(bsz * seq, D_MODEL), mod_p, bsz * seq, seq, i, j, *conv_w, *ln1,
                             alpha).reshape(bsz, seq, D_MODEL)
            ys = _conv_layer(ys, mod_s, dec_seq, dec_seq, i, j, *conv_w, *ln1, alpha)
        yp = _mlp(yp.reshape(bsz * seq, D_MODEL), mod_p, bsz * seq, i, *mlp_w, alpha).reshape(bsz, seq, D_MODEL)
        ys = _mlp(ys, mod_s, dec_seq, i, *mlp_w, alpha)
    new_cache_k = jnp.stack(new_k, axis=1)
    new_cache_v = jnp.stack(new_v, axis=1)
    return (yp, ys.reshape(dec_bsz, dec_seq, D_MODEL), new_cache_k, new_cache_v)
```

```python
import functools
import math

import jax
import jax.numpy as jnp
from jax import lax
from jax.experimental import pallas as pl
from jax.experimental.pallas import tpu as pltpu

D_MODEL = 1024
GRID_W = 64
N_MIXERS = 2
HEAD_DIM = 64
N_HEADS = D_MODEL // HEAD_DIM
N_KV_HEADS = 4
GROUP = N_HEADS // N_KV_HEADS
KV_WIDTH = N_KV_HEADS * HEAD_DIM
QK_WIDTH = D_MODEL + KV_WIDTH
QKV_WIDTH = D_MODEL + 2 * KV_WIDTH
BLOCK = 128
ROPE_THETA = 10000.0
CONV_WIDTH = 31
CONV_PAD = (CONV_WIDTH - 1) // 2
D_FF = 4 * D_MODEL
N_MOD = 6
LN_EPS = 1e-5
NEG_INF = -1e30
ATTN_SCALE = HEAD_DIM ** -0.5
LOG2E = math.log2(math.e)
Q_SCALE = ATTN_SCALE * LOG2E

LANES = 128
SUBLANES = 8
VMEM_LIMIT_BYTES = 56 * 1024 * 1024

MOD_ROWS = 8
COND_ROWS = 8
HALO = 16
CONV_ROW_STRIDE = 4
CONV_ROW_BLOCK = 128
CONV_TILE_ROWS = 512
CONV_SUB_ROWS = 256
CONV_CHUNK_TILES = 2
SOFTMAX_ROWS = 64
PROMPT_SEQS_PER_STEP = 4
MOD_TILE_COLS = 3072
MLP_TILE_ROWS = 512
MLP_SUB_ROWS = 256

BF16 = jnp.bfloat16
F32 = jnp.float32


def _const_spec(shape):
    n = len(shape)
    return pl.BlockSpec(shape, lambda *_: (0,) * n, pipeline_mode=pl.Buffered(1))


def _layer_spec(shape, layer):
    n = len(shape)
    return pl.BlockSpec((None,) + tuple(shape[1:]), lambda *_: (layer,) + (0,) * (n - 1),
                        pipeline_mode=pl.Buffered(1))


def _params(n_axes):
    return pltpu.CompilerParams(dimension_semantics=("arbitrary",) * n_axes,
                                vmem_limit_bytes=VMEM_LIMIT_BYTES)


def _layer_norm(z, g, b):
    mu = jnp.mean(z, axis=-1, keepdims=True)
    zc = z - mu
    var = jnp.mean(zc * zc, axis=-1, keepdims=True)
    return zc * lax.rsqrt(var + LN_EPS) * g + b


def _bdot(a, b):
    return jnp.dot(a, b, preferred_element_type=F32)


def _bdot_nt(a, b):
    return lax.dot_general(a, b, (((1,), (1,)), ((), ())), preferred_element_type=F32)


def _mod_kernel(cond_ref, w_ref, b_ref, out_ref, s_scr, *, n_cond):
    @pl.when((pl.program_id(0) == 0) & (pl.program_id(1) == 0))
    def _():
        c = cond_ref[...]
        s_t = (c * jax.nn.sigmoid(c)).T
        for m in range(n_cond):
            s_scr[m] = jnp.broadcast_to(s_t[:, m:m + 1], (D_MODEL, LANES))

    zero_rows = [jnp.zeros((1, LANES), F32)] * (COND_ROWS - n_cond)
    for n in range(w_ref.shape[2] // LANES):
        cols = slice(n * LANES, (n + 1) * LANES)
        accs = [jnp.zeros((SUBLANES, LANES), F32) for _ in range(n_cond)]
        for k in range(0, D_MODEL, SUBLANES):
            w = w_ref[0, k:k + SUBLANES, cols]
            for m in range(n_cond):
                accs[m] = accs[m] + w * s_scr[m, k:k + SUBLANES, :]
        out_rows = [jnp.sum(a, axis=0, keepdims=True) for a in accs] + zero_rows
        out_ref[0, :, cols] = jnp.concatenate(out_rows, axis=0) + b_ref[0, :, cols]


def _modulation(cond, n_cond, ada_w, ada_b):
    depth = ada_w.shape[0]
    tn = MOD_TILE_COLS
    width = N_MOD * D_MODEL
    return pl.pallas_call(
        functools.partial(_mod_kernel, n_cond=n_cond),
        grid=(depth, width // tn),
        in_specs=[
            pl.BlockSpec((COND_ROWS, D_MODEL), lambda l, n: (0, 0)),
            pl.BlockSpec((1, D_MODEL, tn), lambda l, n: (l, 0, n)),
            pl.BlockSpec((1, 1, tn), lambda l, n: (l, 0, n)),
        ],
        out_specs=pl.BlockSpec((1, COND_ROWS, tn), lambda l, n: (l, 0, n)),
        out_shape=jax.ShapeDtypeStruct((depth, COND_ROWS, width), F32),
        scratch_shapes=[pltpu.VMEM((n_cond, D_MODEL, LANES), F32)],
        compiler_params=_params(2),
        name="modulation",
    )(cond, ada_w, ada_b.reshape(depth, 1, width))


def _mlp_kernel(y_ref, mod_ref, w1_ref, b1_ref, w2_ref, b2_ref, g_ref, b_ref, out_ref, *,
                alpha, ff_chunk):
    m = mod_ref[0]
    for r0 in range(0, y_ref.shape[0], MLP_SUB_ROWS):
        rows = slice(r0, r0 + MLP_SUB_ROWS)
        y = y_ref[rows, :]
        h = (y * (1.0 + m[4:5]) + m[3:4]).astype(BF16)
        acc = jnp.zeros(y.shape, F32)
        for c in range(D_FF // ff_chunk):
            cols = slice(c * ff_chunk, (c + 1) * ff_chunk)
            a = _bdot(h, w1_ref[:, cols]) + b1_ref[:, cols]
            a = jnp.maximum(a, 0.0)
            acc = acc + _bdot((a * a).astype(BF16), w2_ref[cols, :])
        z = alpha * y + m[5:6] * (acc + b2_ref[...])
        out_ref[rows, :] = _layer_norm(z, g_ref[...], b_ref[...])


def _mlp(y, mod, rows_per_group, layer, w1, b1, w2, b2, g, b, alpha):
    m_rows = y.shape[0]
    tm = MLP_TILE_ROWS
    kern = functools.partial(_mlp_kernel, alpha=alpha, ff_chunk=1024)
    return pl.pallas_call(
        kern,
        grid=(m_rows // tm,),
        in_specs=[
            pl.BlockSpec((tm, D_MODEL), lambda i: (i, 0)),
            pl.BlockSpec((1, MOD_ROWS, D_MODEL), lambda i: (i * tm // rows_per_group, 0, 0)),
            _layer_spec(w1.shape, layer),
            _layer_spec(b1.shape, layer),
            _layer_spec(w2.shape, layer),
            _layer_spec(b2.shape, layer),
            _layer_spec(g.shape, layer),
            _layer_spec(b.shape, layer),
        ],
        out_specs=pl.BlockSpec((tm, D_MODEL), lambda i: (i, 0)),
        out_shape=jax.ShapeDtypeStruct((m_rows, D_MODEL), F32),
        compiler_params=_params(1),
        name="mlp",
    )(y, mod, w1, b1, w2, b2, g, b)


def _head_pair_operand(x, kv_head, transpose=False):
    tile = x[:, LANES * (kv_head // 2):LANES * (kv_head // 2 + 1)]
    swapped = pltpu.roll(tile, HEAD_DIM, axis=1)
    lane = lax.broadcasted_iota(jnp.int32, (1, LANES), 1)
    low = lane < HEAD_DIM
    if kv_head % 2 == 0:
        lo_src, hi_src = tile, swapped
    else:
        lo_src, hi_src = swapped, tile
    lo = jnp.where(low, lo_src, 0.0)
    hi = jnp.where(low, 0.0, hi_src)
    pieces = []
    for blk in range(x.shape[0] // BLOCK):
        pieces += [lo[blk * BLOCK:(blk + 1) * BLOCK], hi[blk * BLOCK:(blk + 1) * BLOCK]]
    operand = jnp.concatenate(pieces, axis=0)
    if transpose:
        operand = operand.T
    return operand.astype(BF16)


def _pair_softmax(s_ref, p_ref, stat_ref, sinks, row_plan):
    def load(rows, block, head, plan_row, biases):
        tile = 2 * block + head
        v = s_ref[rows, tile * LANES:(tile + 1) * LANES]
        if block in biases:
            v = v + biases[block][plan_row:plan_row + SOFTMAX_ROWS, :]
        return v

    for row0, n_rows, blocks, biases in row_plan:
        for r in range(0, n_rows, SOFTMAX_ROWS):
            rows = slice(row0 + r, row0 + r + SOFTMAX_ROWS)
            for head in range(2):
                acc = None
                for block in blocks:
                    v = load(rows, block, head, r, biases)
                    acc = v if acc is None else jnp.maximum(acc, v)
                stat_ref[0, rows, head * LANES:(head + 1) * LANES] = acc
    mx = [jnp.maximum(jnp.max(stat_ref[0, :, head * LANES:(head + 1) * LANES], axis=-1, keepdims=True),
                      sinks[head]) for head in range(2)]
    for row0, n_rows, blocks, biases in row_plan:
        for r in range(0, n_rows, SOFTMAX_ROWS):
            rows = slice(row0 + r, row0 + r + SOFTMAX_ROWS)
            for head in range(2):
                m_rows = mx[head][row0 + r:row0 + r + SOFTMAX_ROWS]
                acc = None
                for block in blocks:
                    p = jnp.exp2(load(rows, block, head, r, biases) - m_rows)
                    acc = p if acc is None else acc + p
                    tile = 2 * block + head
                    p_ref[rows, tile * LANES:(tile + 1) * LANES] = p.astype(BF16)
                stat_ref[1, rows, head * LANES:(head + 1) * LANES] = acc
    recip = [1.0 / (jnp.sum(stat_ref[1, :, head * LANES:(head + 1) * LANES], axis=-1, keepdims=True)
                    + jnp.exp2(sinks[head] - mx[head])) for head in range(2)]
    lane = lax.broadcasted_iota(jnp.int32, (1, LANES), 1)
    return jnp.where(lane < HEAD_DIM, recip[0], recip[1])


def _attention_pipeline(units, sink_ref, o_scr, stat_scr):
    def scores(u):
        dot = _bdot if u.get("kk_transposed") else _bdot_nt
        u["s_ref"][...] = dot(u["q"](), u["kk"]())

    def values(u, recip):
        o = _bdot(u["p_ref"][...], u["vv"]()) * recip
        o_scr[u["rows"], LANES * u["t"]:LANES * (u["t"] + 1)] = o.astype(BF16)

    scores(units[0])
    recip_prev = None
    for n, u in enumerate(units):
        if n + 1 < len(units):
            scores(units[n + 1])
        if n > 0:
            values(units[n - 1], recip_prev)
        sinks = (sink_ref[2 * u["t"]] * LOG2E, sink_ref[2 * u["t"] + 1] * LOG2E)
        recip_prev = _pair_softmax(u["s_ref"], u["p_ref"], stat_scr.at[n % 2], sinks, u["plan"])
    values(units[-1], recip_prev)


def _cached(fn):
    memo = {}

    def get(*key):
        if key not in memo:
            memo[key] = fn(*key)
        return memo[key]
    return get


def _prompt_attn_kernel(sink_ref, x_ref, mod_ref, wqkv_ref, wo_ref, g_ref, b_ref,
                        y_ref, k_ref, v_ref, o_scr, s_scr, p_scr, stat_scr, *, alpha):
    n_seq, seq, _ = x_ref.shape
    m = mod_ref[0]
    x = x_ref[...].reshape(n_seq * seq, D_MODEL)
    h = (x * (1.0 + m[1:2]) + m[0:1]).astype(BF16)
    qkv = _bdot(h, wqkv_ref[...])
    q = (qkv[:, :D_MODEL] * Q_SCALE).astype(BF16)
    plan = [(0, seq, list(range(seq // BLOCK)), {})]
    units = []
    for sq in range(n_seq):
        rows = slice(sq * seq, (sq + 1) * seq)
        k = qkv[rows, D_MODEL:QK_WIDTH]
        v = qkv[rows, QK_WIDTH:]
        k_ref[sq] = k
        v_ref[sq] = v
        kk = _cached(functools.partial(_head_pair_operand, k))
        vv = _cached(functools.partial(_head_pair_operand, v))
        for t in range(D_MODEL // LANES):
            n = len(units)
            units.append(dict(q=functools.partial(lambda rows, t: q[rows, LANES * t:LANES * (t + 1)], rows, t),
                              kk=functools.partial(kk, t // (GROUP // 2)),
                              vv=functools.partial(vv, t // (GROUP // 2)),
                              t=t, rows=rows, s_ref=s_scr.at[n % 2], p_ref=p_scr.at[n % 2], plan=plan))
    _attention_pipeline(units, sink_ref, o_scr, stat_scr)
    out = _bdot(o_scr[...], wo_ref[...])
    z = alpha * x + m[2:3] * out
    y_ref[...] = _layer_norm(z, g_ref[...], b_ref[...]).reshape(n_seq, seq, D_MODEL)


def _prompt_attn_layer(x, mod, sink, layer, attn_layer, wqkv, wo, g, b, alpha):
    bsz, seq, _ = x.shape
    n_seq = PROMPT_SEQS_PER_STEP
    assert bsz % n_seq == 0
    width = 2 * (seq // BLOCK) * LANES
    kern = functools.partial(_prompt_attn_kernel, alpha=alpha)
    return pl.pallas_call(
        kern,
        grid=(bsz // n_seq,),
        in_specs=[
            pl.BlockSpec(memory_space=pltpu.SMEM),
            pl.BlockSpec((n_seq, seq, D_MODEL), lambda i: (i, 0, 0)),
            pl.BlockSpec((1, MOD_ROWS, D_MODEL), lambda i: (0, 0, 0)),
            _layer_spec(wqkv.shape, attn_layer),
            _layer_spec(wo.shape, attn_layer),
            _layer_spec(g.shape, layer),
            _layer_spec(b.shape, layer),
        ],
        out_specs=[
            pl.BlockSpec((n_seq, seq, D_MODEL), lambda i: (i, 0, 0)),
            pl.BlockSpec((n_seq, seq, KV_WIDTH), lambda i: (i, 0, 0)),
            pl.BlockSpec((n_seq, seq, KV_WIDTH), lambda i: (i, 0, 0)),
        ],
        out_shape=[
            jax.ShapeDtypeStruct((bsz, seq, D_MODEL), F32),
            jax.ShapeDtypeStruct((bsz, seq, KV_WIDTH), F32),
            jax.ShapeDtypeStruct((bsz, seq, KV_WIDTH), F32),
        ],
        scratch_shapes=[pltpu.VMEM((n_seq * seq, D_MODEL), BF16),
                        pltpu.VMEM((2, seq, width), F32),
                        pltpu.VMEM((2, seq, width), BF16),
                        pltpu.VMEM((2, 2, seq, 2 * LANES), F32)],
        compiler_params=_params(1),
        name="prompt_attn",
    )(sink, x, mod, wqkv, wo, g, b)


def _rope_tables(n_tokens):
    rows = n_tokens // GRID_W
    row = jnp.repeat(jnp.arange(rows, dtype=F32), GRID_W)
    col = jnp.tile(jnp.arange(GRID_W, dtype=F32), rows)
    half = HEAD_DIM // 2
    inv = ROPE_THETA ** (-jnp.arange(0, half, 2, dtype=F32) / half)
    ang_r = row[:, None] * inv
    ang_c = col[:, None] * inv
    cos = jnp.concatenate([jnp.cos(ang_r)] * 2 + [jnp.cos(ang_c)] * 2, axis=-1)
    sin = jnp.concatenate([-jnp.sin(ang_r), jnp.sin(ang_r), -jnp.sin(ang_c), jnp.sin(ang_c)], axis=-1)
    return jnp.tile(cos, (1, LANES // HEAD_DIM)), jnp.tile(sin, (1, LANES // HEAD_DIM))


def _sample_qkv_kernel(x_ref, mod_ref, cos_ref, sin_ref, wqkv_ref, q_ref, kv_ref):
    x = x_ref[...]
    m = mod_ref[0]
    h = (x * (1.0 + m[1:2]) + m[0:1]).astype(BF16)
    qkv = _bdot(h, wqkv_ref[...])
    cos = cos_ref[...]
    sin = sin_ref[...]
    lane = lax.broadcasted_iota(jnp.int32, (1, LANES), 1)
    first = (lane % (HEAD_DIM // 2)) < (HEAD_DIM // 4)
    quarter = HEAD_DIM // 4
    for t in range(QK_WIDTH // LANES):
        tile = qkv[:, LANES * t:LANES * (t + 1)]
        partner = jnp.where(first, pltpu.roll(tile, LANES - quarter, axis=1),
                            pltpu.roll(tile, quarter, axis=1))
        rot = tile * cos + partner * sin
        if t < D_MODEL // LANES:
            q_ref[:, LANES * t:LANES * (t + 1)] = (rot * Q_SCALE).astype(BF16)
        else:
            tk = t - D_MODEL // LANES
            kv_ref[:, LANES * tk:LANES * (tk + 1)] = rot.astype(BF16)
    kv_ref[:, KV_WIDTH:] = qkv[:, QK_WIDTH:].astype(BF16)


def _sample_qkv(x, mod, seq, cos, sin, attn_layer, wqkv):
    m_rows = x.shape[0]
    tm = 512
    tiles_per_seq = seq // tm
    return pl.pallas_call(
        _sample_qkv_kernel,
        grid=(m_rows // tm,),
        in_specs=[
            pl.BlockSpec((tm, D_MODEL), lambda i: (i, 0)),
            pl.BlockSpec((1, MOD_ROWS, D_MODEL), lambda i: (i // tiles_per_seq, 0, 0)),
            pl.BlockSpec((tm, LANES), lambda i: (i % tiles_per_seq, 0)),
            pl.BlockSpec((tm, LANES), lambda i: (i % tiles_per_seq, 0)),
            _layer_spec(wqkv.shape, attn_layer),
        ],
        out_specs=[
            pl.BlockSpec((tm, D_MODEL), lambda i: (i, 0)),
            pl.BlockSpec((tm, 2 * KV_WIDTH), lambda i: (i, 0)),
        ],
        out_shape=[
            jax.ShapeDtypeStruct((m_rows, D_MODEL), BF16),
            jax.ShapeDtypeStruct((m_rows, 2 * KV_WIDTH), BF16),
        ],
        compiler_params=_params(1),
        name="sample_qkv",
    )(x, mod, cos, sin, wqkv)


def _sample_attn_kernel(sink_ref, x_ref, mod_ref, q_ref, kvp_ref, kvm_ref, kvn_ref, ctx_ref,
                        wo_ref, g_ref, b_ref, y_ref, o_scr, s_scr, p_scr, stat_scr, bias_scr, *,
                        alpha):
    i = pl.program_id(1)
    n_tiles = pl.num_programs(1)
    m = mod_ref[0]
    nc = ctx_ref.shape[1] // BLOCK
    window = 4 + nc
    half_rows = 2 * BLOCK
    kv_all = jnp.concatenate([kvp_ref[...].astype(F32), kvm_ref[0:BLOCK, :].astype(F32), ctx_ref[0],
                              kvm_ref[BLOCK:, :].astype(F32), kvn_ref[...].astype(F32)], axis=0)
    k_all, v_all = kv_all[:, :KV_WIDTH], kv_all[:, KV_WIDTH:]
    qi = lax.broadcasted_iota(jnp.int32, (BLOCK, BLOCK), 0)
    ki = lax.broadcasted_iota(jnp.int32, (BLOCK, BLOCK), 1)
    bias_scr[0] = jnp.where(ki >= qi, jnp.where(i > 0, 0.0, NEG_INF), NEG_INF)
    bias_scr[1] = jnp.where(ki <= qi, 0.0, NEG_INF)
    bias_scr[2] = jnp.where(ki >= qi, 0.0, NEG_INF)
    bias_scr[3] = jnp.where(ki <= qi, jnp.where(i < n_tiles - 1, 0.0, NEG_INF), NEG_INF)
    prev_edge, next_plain, prev_plain, next_edge = (bias_scr.at[j] for j in range(4))
    ctx_a = list(range(2, 2 + nc))
    ctx_b = list(range(nc))
    plans = [
        [(0, BLOCK, [0, 1] + ctx_a + [2 + nc], {0: prev_edge, 2 + nc: next_plain}),
         (BLOCK, BLOCK, [1] + ctx_a + [2 + nc, 3 + nc], {1: prev_plain, 3 + nc: next_plain})],
        [(0, BLOCK, ctx_b + [nc, nc + 1, nc + 2], {nc: prev_plain, nc + 2: next_plain}),
         (BLOCK, BLOCK, ctx_b + [nc + 1, nc + 2, nc + 3], {nc + 1: prev_plain, nc + 3: next_edge})],
    ]
    unused = [[3 + nc, 0], [nc + 3, nc]]
    for half in range(2):
        for parity in range(2):
            for qb in range(2):
                blk = unused[half][qb]
                p_scr[2 * half + parity, qb * BLOCK:(qb + 1) * BLOCK,
                      2 * blk * LANES:2 * (blk + 1) * LANES] = jnp.zeros((BLOCK, 2 * LANES), BF16)
    kk = _cached(lambda kv_head: _head_pair_operand(k_all, kv_head, transpose=True))
    vv = _cached(lambda kv_head: _head_pair_operand(v_all, kv_head))

    def operand(full, kv_head, half, transposed):
        first = 2 * half * 2 * BLOCK
        if transposed:
            return full(kv_head)[:, first:first + window * 2 * BLOCK]
        return full(kv_head)[first:first + window * 2 * BLOCK]

    units = []
    for kv_head in range(N_KV_HEADS):
        for half in range(2):
            for t in range(kv_head * (GROUP // 2), (kv_head + 1) * (GROUP // 2)):
                n = len(units)
                rows = slice(half * half_rows, (half + 1) * half_rows)
                units.append(dict(
                    q=functools.partial(lambda rows, t: q_ref[rows, LANES * t:LANES * (t + 1)], rows, t),
                    kk=functools.partial(operand, kk, kv_head, half, True),
                    vv=functools.partial(operand, vv, kv_head, half, False),
                    t=t, rows=rows, s_ref=s_scr.at[n % 2], p_ref=p_scr.at[2 * half + n % 2],
                    plan=plans[half], kk_transposed=True))
    _attention_pipeline(units, sink_ref, o_scr, stat_scr)
    out = _bdot(o_scr[...], wo_ref[...])
    z = alpha * x_ref[...] + m[2:3] * out
    y_ref[...] = _layer_norm(z, g_ref[...], b_ref[...])


def _sample_attn_layer(x, mod, seq, q, kv, ctx_kv, sink, layer, attn_layer, wo, g, b, alpha):
    m_rows = x.shape[0]
    bsz = m_rows // seq
    tq = 4 * BLOCK
    tiles = seq // tq
    blocks = seq // BLOCK
    n_ctx = ctx_kv.shape[1]
    assert n_ctx % BLOCK == 0 and seq % tq == 0
    window = 4 + n_ctx // BLOCK
    kern = functools.partial(_sample_attn_kernel, alpha=alpha)
    row_tile = lambda bi, i: (bi * tiles + i, 0)
    prev_blk = lambda bi, i: (bi * blocks + jnp.maximum(4 * i - 1, 0), 0)
    next_blk = lambda bi, i: (bi * blocks + jnp.minimum(4 * i + 4, blocks - 1), 0)
    ctx_blk = lambda bi, i: (bi, 0, 0)
    half_rows = 2 * BLOCK
    width = 2 * window * LANES
    return pl.pallas_call(
        kern,
        grid=(bsz, tiles),
        in_specs=[
            pl.BlockSpec(memory_space=pltpu.SMEM),
            pl.BlockSpec((tq, D_MODEL), row_tile),
            pl.BlockSpec((1, MOD_ROWS, D_MODEL), ctx_blk),
            pl.BlockSpec((tq, D_MODEL), row_tile),
            pl.BlockSpec((BLOCK, 2 * KV_WIDTH), prev_blk),
            pl.BlockSpec((tq, 2 * KV_WIDTH), row_tile),
            pl.BlockSpec((BLOCK, 2 * KV_WIDTH), next_blk),
            pl.BlockSpec((1, n_ctx, 2 * KV_WIDTH), ctx_blk),
            _layer_spec(wo.shape, attn_layer),
            _layer_spec(g.shape, layer),
            _layer_spec(b.shape, layer),
        ],
        out_specs=pl.BlockSpec((tq, D_MODEL), row_tile),
        out_shape=jax.ShapeDtypeStruct((m_rows, D_MODEL), F32),
        scratch_shapes=[pltpu.VMEM((tq, D_MODEL), BF16),
                        pltpu.VMEM((2, half_rows, width), F32),
                        pltpu.VMEM((4, half_rows, width), BF16),
                        pltpu.VMEM((2, 2, half_rows, 2 * LANES), F32),
                        pltpu.VMEM((4, BLOCK, BLOCK), F32)],
        compiler_params=_params(2),
        name="sample_attn",
    )(sink, x, mod, q, kv, kv, kv, ctx_kv, wo, g, b)


def _conv_kernel(xp_ref, x_ref, xn_ref, mod_ref, pw1_ref, pb1_ref, dw_ref, db_ref,
                 cg_ref, cb_ref, pw2_ref, pb2_ref, g_ref, b_ref, y_ref, *scratch,
                 alpha, tm, sub, tiles_per_seq, seq_per_sub):
    i = pl.program_id(0)
    t = i % tiles_per_seq
    m = mod_ref[0]
    lane_tiles = D_MODEL // LANES
    n_sub = tm // sub
    h_scrs, u_scrs, c_scrs = scratch[:n_sub], scratch[n_sub:2 * n_sub], scratch[2 * n_sub:]
    recompute_halo = not seq_per_sub

    def modulated_input(s):
        r0 = s * sub
        rows = x_ref[r0:r0 + sub, :]
        if recompute_halo:
            top = xp_ref[...] if s == 0 else x_ref[r0 - HALO:r0, :]
            bot = xn_ref[...] if s == n_sub - 1 else x_ref[r0 + sub:r0 + sub + HALO, :]
            rows = jnp.concatenate([top, rows, bot], axis=0)
        h_scrs[s][...] = (rows * (1.0 + m[1:2]) + m[0:1]).astype(BF16)

    def glu_chunk(s, chunk):
        width = CONV_CHUNK_TILES * LANES
        value_cols = slice(chunk * width, (chunk + 1) * width)
        gate_cols = slice(D_MODEL + chunk * width, D_MODEL + (chunk + 1) * width)
        h = h_scrs[s][...]
        value = _bdot(h, pw1_ref[:, value_cols]) + pb1_ref[:, value_cols]
        gate = _bdot(h, pw1_ref[:, gate_cols]) + pb1_ref[:, gate_cols]
        u_chunk = value * jax.nn.sigmoid(gate)
        zeros = jnp.zeros((HALO, LANES), F32)
        for j in range(CONV_CHUNK_TILES):
            c = chunk * CONV_CHUNK_TILES + j
            u = u_chunk[:, j * LANES:(j + 1) * LANES]
            if recompute_halo:
                u_scrs[s][c, :, :] = u
                if s == 0:
                    u_scrs[s][c, 0:HALO, :] = jnp.where(t > 0, u[:HALO], 0.0)
                if s == n_sub - 1:
                    u_scrs[s][c, HALO + sub:, :] = jnp.where(t < tiles_per_seq - 1, u[HALO + sub:], 0.0)
            else:
                u_scrs[s][c, 0:HALO, :] = zeros
                u_scrs[s][c, HALO:HALO + sub, :] = u
                u_scrs[s][c, HALO + sub:, :] = zeros

    def conv_taps(s, c):
        span_rows = SUBLANES * CONV_ROW_STRIDE
        for base in range(0, sub, CONV_ROW_BLOCK):
            starts = [base + sp + j for sp in range(0, CONV_ROW_BLOCK, span_rows)
                      for j in range(CONV_ROW_STRIDE)]
            bias = db_ref[c]
            accs = [jnp.zeros((SUBLANES, LANES), F32) + bias for _ in starts]
            for tap in range(CONV_WIDTH):
                w = dw_ref[c, tap:tap + 1, :]
                off = HALO - CONV_PAD + tap
                for n, start in enumerate(starts):
                    rows = pl.ds(start + off, SUBLANES, stride=CONV_ROW_STRIDE)
                    accs[n] = accs[n] + w * u_scrs[s][c, rows, :]
            for n, start in enumerate(starts):
                c_scrs[s][c, pl.ds(start, SUBLANES, stride=CONV_ROW_STRIDE), :] = accs[n]

    def finish(s):
        r0 = s * sub
        x = x_ref[r0:r0 + sub, :]
        conv = jnp.concatenate([c_scrs[s][c] for c in range(lane_tiles)], axis=1)
        un = _layer_norm(conv, cg_ref[...], cb_ref[...])
        un = un * jax.nn.sigmoid(un)
        out = _bdot(un.astype(BF16), pw2_ref[...]) + pb2_ref[...]
        z = alpha * x + m[2:3] * out
        y_ref[r0:r0 + sub, :] = _layer_norm(z, g_ref[...], b_ref[...])

    for s in range(n_sub):
        modulated_input(s)
    n_chunks = lane_tiles // CONV_CHUNK_TILES
    units = [(s, chunk) for s in range(n_sub) for chunk in range(n_chunks)]
    glu_chunk(*units[0])
    for n, (s, chunk) in enumerate(units):
        if n + 1 < len(units):
            glu_chunk(*units[n + 1])
        for j in range(CONV_CHUNK_TILES):
            conv_taps(s, chunk * CONV_CHUNK_TILES + j)
        if chunk == n_chunks - 1:
            finish(s)


def _conv_layer(x, mod, rows_per_group, seq, layer, conv_layer, pw1, pb1, dw, db, cg, cb, pw2, pb2,
                g, b, alpha):
    m_rows = x.shape[0]
    tm = CONV_TILE_ROWS
    sub = CONV_SUB_ROWS
    seq_per_sub = seq == sub
    assert seq_per_sub or seq % tm == 0
    tiles_per_seq = max(seq // tm, 1)
    in_rows = sub if seq_per_sub else sub + 2 * HALO
    halo_per_tile = tm // HALO
    n_halo_blocks = m_rows // HALO
    kern = functools.partial(_conv_kernel, alpha=alpha, tm=tm, sub=sub, tiles_per_seq=tiles_per_seq,
                             seq_per_sub=seq_per_sub)
    prev_blk = lambda i: (jnp.maximum(i * halo_per_tile - 1, 0), 0)
    next_blk = lambda i: (jnp.minimum((i + 1) * halo_per_tile, n_halo_blocks - 1), 0)
    return pl.pallas_call(
        kern,
        grid=(m_rows // tm,),
        in_specs=[
            pl.BlockSpec((HALO, D_MODEL), prev_blk),
            pl.BlockSpec((tm, D_MODEL), lambda i: (i, 0)),
            pl.BlockSpec((HALO, D_MODEL), next_blk),
            pl.BlockSpec((1, MOD_ROWS, D_MODEL), lambda i: (i * tm // rows_per_group, 0, 0)),
            _layer_spec(pw1.shape, conv_layer),
            _layer_spec(pb1.shape, conv_layer),
            _layer_spec(dw.shape, conv_layer),
            _layer_spec(db.shape, conv_layer),
            _layer_spec(cg.shape, conv_layer),
            _layer_spec(cb.shape, conv_layer),
            _layer_spec(pw2.shape, conv_layer),
            _layer_spec(pb2.shape, conv_layer),
            _layer_spec(g.shape, layer),
            _layer_spec(b.shape, layer),
        ],
        out_specs=pl.BlockSpec((tm, D_MODEL), lambda i: (i, 0)),
        out_shape=jax.ShapeDtypeStruct((m_rows, D_MODEL), F32),
        scratch_shapes=([pltpu.VMEM((in_rows, D_MODEL), BF16)] * (tm // sub)
                        + [pltpu.VMEM((D_MODEL // LANES, sub + 2 * HALO, LANES), F32)] * (tm // sub)
                        + [pltpu.VMEM((D_MODEL // LANES, sub, LANES), F32)] * (tm // sub)),
        compiler_params=_params(1),
        name="conv",
    )(x, x, x, mod, pw1, pb1, dw, db, cg, cb, pw2, pb2, g, b)


def kernel(x_prompt, x_sample, cache_k, cache_v, c, c_ctx, ada_w, ada_b, attn_w_qkv, attn_w_o, attn_sink, conv_pw1_w, conv_pw1_b, conv_dw_w, conv_dw_b, conv_norm_g, conv_norm_b, conv_pw2_w, conv_pw2_b, ln1_g, ln1_b, mlp_w1, mlp_b1, mlp_w2, mlp_b2, ln2_g, ln2_b):
    depth = ada_w.shape[0]
    bsz, seq, _ = x_prompt.shape
    dec_bsz, dec_seq, _ = x_sample.shape
    n_ctx = cache_k.shape[2]
    alpha = (2.0 * depth) ** 0.25
    assert 1 + dec_bsz <= COND_ROWS

    rows = lambda a: a.reshape(a.shape[0], 1, a.shape[-1])
    lane_tiles = D_MODEL // LANES

    cond = jnp.concatenate([c_ctx[None], c, jnp.zeros((COND_ROWS - 1 - dec_bsz, D_MODEL), F32)], axis=0)
    mods = _modulation(cond, 1 + dec_bsz, ada_w, ada_b).reshape(depth, COND_ROWS, N_MOD, D_MODEL)
    mods = jnp.pad(mods, ((0, 0), (0, 0), (0, MOD_ROWS - N_MOD), (0, 0)))

    cos, sin = _rope_tables(dec_seq)

    wqkv, wo = attn_w_qkv.astype(BF16), attn_w_o.astype(BF16)
    pw1, pw2 = conv_pw1_w.astype(BF16), conv_pw2_w.astype(BF16)
    w1, w2 = mlp_w1.astype(BF16), mlp_w2.astype(BF16)
    ln1 = (rows(ln1_g), rows(ln1_b))
    n_conv = conv_dw_w.shape[0]
    dw = conv_dw_w.reshape(n_conv, CONV_WIDTH, lane_tiles, LANES).transpose(0, 2, 1, 3)
    db = conv_dw_b.reshape(n_conv, lane_tiles, 1, LANES)
    conv_w = (pw1, rows(conv_pw1_b), dw, db, rows(conv_norm_g), rows(conv_norm_b), pw2, rows(conv_pw2_b))
    mlp_w = (w1, rows(mlp_b1), w2, rows(mlp_b2), rows(ln2_g), rows(ln2_b))

    yp = x_prompt
    ys = x_sample.reshape(dec_bsz * dec_seq, D_MODEL)
    new_k, new_v = [], []
    for i in range(depth):
        mod_p = mods[i, 0:1]
        mod_s = mods[i, 1:1 + dec_bsz]
        j = i // N_MIXERS
        if i % N_MIXERS == 0:
            yp, kp, vp = _prompt_attn_layer(yp, mod_p, attn_sink[j], i, j, wqkv, wo, *ln1, alpha)
            new_k.append(kp.reshape(bsz, seq, N_KV_HEADS, HEAD_DIM))
            new_v.append(vp.reshape(bsz, seq, N_KV_HEADS, HEAD_DIM))
            q, kv = _sample_qkv(ys, mod_s, dec_seq, cos, sin, j, wqkv)
            ctx_kv = jnp.concatenate([cache_k[:, j].reshape(dec_bsz, n_ctx, KV_WIDTH),
                                      cache_v[:, j].reshape(dec_bsz, n_ctx, KV_WIDTH)], axis=-1)
            ys = _sample_attn_layer(ys, mod_s, dec_seq, q, kv, ctx_kv, attn_sink[j], i, j, wo, *ln1, alpha)
        else:
            yp = _conv_layer(yp.reshape(bsz * seq, D_MODEL), mod_p, bsz * seq, seq, i, j, *conv_w, *ln1,
                             alpha).reshape(bsz, seq, D_MODEL)
            ys = _conv_layer(ys, mod_s, dec_seq, dec_seq, i, j, *conv_w, *ln1, alpha)
        yp = _mlp(yp.reshape(bsz * seq, D_MODEL), mod_p, bsz * seq, i, *mlp_w, alpha).reshape(bsz, seq, D_MODEL)
        ys = _mlp(ys, mod_s, dec_seq, i, *mlp_w, alpha)
    new_cache_k = jnp.stack(new_k, axis=1)
    new_cache_v = jnp.stack(new_v, axis=1)
    return (yp, ys.reshape(dec_bsz, dec_seq, D_MODEL), new_cache_k, new_cache_v)
```

```python
import functools
import math

import jax
import jax.numpy as jnp
from jax import lax
from jax.experimental import pallas as pl
from jax.experimental.pallas import tpu as pltpu

D_MODEL = 1024
GRID_W = 64
N_MIXERS = 2
HEAD_DIM = 64
N_HEADS = D_MODEL // HEAD_DIM
N_KV_HEADS = 4
GROUP = N_HEADS // N_KV_HEADS
KV_WIDTH = N_KV_HEADS * HEAD_DIM
QK_WIDTH = D_MODEL + KV_WIDTH
QKV_WIDTH = D_MODEL + 2 * KV_WIDTH
BLOCK = 128
ROPE_THETA = 10000.0
CONV_WIDTH = 31
CONV_PAD = (CONV_WIDTH - 1) // 2
D_FF = 4 * D_MODEL
N_MOD = 6
LN_EPS = 1e-5
NEG_INF = -1e30
ATTN_SCALE = HEAD_DIM ** -0.5
LOG2E = math.log2(math.e)
Q_SCALE = ATTN_SCALE * LOG2E

LANES = 128
SUBLANES = 8
VMEM_LIMIT_BYTES = 56 * 1024 * 1024

MOD_ROWS = 8
COND_ROWS = 8
HALO = 16
CONV_ROW_STRIDE = 4
CONV_ROW_BLOCK = 128
CONV_TILE_ROWS = 512
CONV_SUB_ROWS = 256
SOFTMAX_ROWS = 64
PROMPT_SEQS_PER_STEP = 4
MOD_TILE_COLS = 3072
MLP_TILE_ROWS = 512
MLP_SUB_ROWS = 256

BF16 = jnp.bfloat16
F32 = jnp.float32


def _const_spec(shape):
    n = len(shape)
    return pl.BlockSpec(shape, lambda *_: (0,) * n, pipeline_mode=pl.Buffered(1))


def _layer_spec(shape, layer):
    n = len(shape)
    return pl.BlockSpec((None,) + tuple(shape[1:]), lambda *_: (layer,) + (0,) * (n - 1),
                        pipeline_mode=pl.Buffered(1))


def _params(n_axes):
    return pltpu.CompilerParams(dimension_semantics=("arbitrary",) * n_axes,
                                vmem_limit_bytes=VMEM_LIMIT_BYTES)


def _layer_norm(z, g, b):
    mu = jnp.mean(z, axis=-1, keepdims=True)
    zc = z - mu
    var = jnp.mean(zc * zc, axis=-1, keepdims=True)
    return zc * lax.rsqrt(var + LN_EPS) * g + b


def _bdot(a, b):
    return jnp.dot(a, b, preferred_element_type=F32)


def _bdot_nt(a, b):
    return lax.dot_general(a, b, (((1,), (1,)), ((), ())), preferred_element_type=F32)


def _mod_kernel(cond_ref, w_ref, b_ref, out_ref, s_scr, *, n_cond):
    @pl.when((pl.program_id(0) == 0) & (pl.program_id(1) == 0))
    def _():
        c = cond_ref[...]
        s_t = (c * jax.nn.sigmoid(c)).T
        for m in range(n_cond):
            s_scr[m] = jnp.broadcast_to(s_t[:, m:m + 1], (D_MODEL, LANES))

    zero_rows = [jnp.zeros((1, LANES), F32)] * (COND_ROWS - n_cond)
    for n in range(w_ref.shape[2] // LANES):
        cols = slice(n * LANES, (n + 1) * LANES)
        accs = [jnp.zeros((SUBLANES, LANES), F32) for _ in range(n_cond)]
        for k in range(0, D_MODEL, SUBLANES):
            w = w_ref[0, k:k + SUBLANES, cols]
            for m in range(n_cond):
                accs[m] = accs[m] + w * s_scr[m, k:k + SUBLANES, :]
        out_rows = [jnp.sum(a, axis=0, keepdims=True) for a in accs] + zero_rows
        out_ref[0, :, cols] = jnp.concatenate(out_rows, axis=0) + b_ref[0, :, cols]


def _modulation(cond, n_cond, ada_w, ada_b):
    depth = ada_w.shape[0]
    tn = MOD_TILE_COLS
    width = N_MOD * D_MODEL
    return pl.pallas_call(
        functools.partial(_mod_kernel, n_cond=n_cond),
        grid=(depth, width // tn),
        in_specs=[
            pl.BlockSpec((COND_ROWS, D_MODEL), lambda l, n: (0, 0)),
            pl.BlockSpec((1, D_MODEL, tn), lambda l, n: (l, 0, n)),
            pl.BlockSpec((1, 1, tn), lambda l, n: (l, 0, n)),
        ],
        out_specs=pl.BlockSpec((1, COND_ROWS, tn), lambda l, n: (l, 0, n)),
        out_shape=jax.ShapeDtypeStruct((depth, COND_ROWS, width), F32),
        scratch_shapes=[pltpu.VMEM((n_cond, D_MODEL, LANES), F32)],
        compiler_params=_params(2),
        name="modulation",
    )(cond, ada_w, ada_b.reshape(depth, 1, width))


def _mlp_kernel(y_ref, mod_ref, w1_ref, b1_ref, w2_ref, b2_ref, g_ref, b_ref, out_ref, *,
                alpha, ff_chunk):
    m = mod_ref[0]
    for r0 in range(0, y_ref.shape[0], MLP_SUB_ROWS):
        rows = slice(r0, r0 + MLP_SUB_ROWS)
        y = y_ref[rows, :]
        h = (y * (1.0 + m[4:5]) + m[3:4]).astype(BF16)
        acc = jnp.zeros(y.shape, F32)
        for c in range(D_FF // ff_chunk):
            cols = slice(c * ff_chunk, (c + 1) * ff_chunk)
            a = _bdot(h, w1_ref[:, cols]) + b1_ref[:, cols]
            a = jnp.maximum(a, 0.0)
            acc = acc + _bdot((a * a).astype(BF16), w2_ref[cols, :])
        z = alpha * y + m[5:6] * (acc + b2_ref[...])
        out_ref[rows, :] = _layer_norm(z, g_ref[...], b_ref[...])


def _mlp(y, mod, rows_per_group, layer, w1, b1, w2, b2, g, b, alpha):
    m_rows = y.shape[0]
    tm = MLP_TILE_ROWS
    kern = functools.partial(_mlp_kernel, alpha=alpha, ff_chunk=1024)
    return pl.pallas_call(
        kern,
        grid=(m_rows // tm,),
        in_specs=[
            pl.BlockSpec((tm, D_MODEL), lambda i: (i, 0)),
            pl.BlockSpec((1, MOD_ROWS, D_MODEL), lambda i: (i * tm // rows_per_group, 0, 0)),
            _layer_spec(w1.shape, layer),
            _layer_spec(b1.shape, layer),
            _layer_spec(w2.shape, layer),
            _layer_spec(b2.shape, layer),
            _layer_spec(g.shape, layer),
            _layer_spec(b.shape, layer),
        ],
        out_specs=pl.BlockSpec((tm, D_MODEL), lambda i: (i, 0)),
        out_shape=jax.ShapeDtypeStruct((m_rows, D_MODEL), F32),
        compiler_params=_params(1),
        name="mlp",
    )(y, mod, w1, b1, w2, b2, g, b)


def _head_pair_operand(x, kv_head):
    tile = x[:, LANES * (kv_head // 2):LANES * (kv_head // 2 + 1)]
    swapped = pltpu.roll(tile, HEAD_DIM, axis=1)
    lane = lax.broadcasted_iota(jnp.int32, (1, LANES), 1)
    low = lane < HEAD_DIM
    if kv_head % 2 == 0:
        lo_src, hi_src = tile, swapped
    else:
        lo_src, hi_src = swapped, tile
    lo = jnp.where(low, lo_src, 0.0).astype(BF16)
    hi = jnp.where(low, 0.0, hi_src).astype(BF16)
    pieces = []
    for blk in range(x.shape[0] // BLOCK):
        pieces += [lo[blk * BLOCK:(blk + 1) * BLOCK], hi[blk * BLOCK:(blk + 1) * BLOCK]]
    return jnp.concatenate(pieces, axis=0)


def _pair_softmax(s_ref, p_ref, stat_ref, sinks, row_plan):
    def load(rows, block, head, plan_row, biases):
        tile = 2 * block + head
        v = s_ref[rows, tile * LANES:(tile + 1) * LANES]
        if block in biases:
            v = v + biases[block][plan_row:plan_row + SOFTMAX_ROWS, :]
        return v

    for row0, n_rows, blocks, biases in row_plan:
        for r in range(0, n_rows, SOFTMAX_ROWS):
            rows = slice(row0 + r, row0 + r + SOFTMAX_ROWS)
            for head in range(2):
                acc = None
                for block in blocks:
                    v = load(rows, block, head, r, biases)
                    acc = v if acc is None else jnp.maximum(acc, v)
                stat_ref[0, rows, head * LANES:(head + 1) * LANES] = acc
    mx = [jnp.maximum(jnp.max(stat_ref[0, :, head * LANES:(head + 1) * LANES], axis=-1, keepdims=True),
                      sinks[head]) for head in range(2)]
    for row0, n_rows, blocks, biases in row_plan:
        for r in range(0, n_rows, SOFTMAX_ROWS):
            rows = slice(row0 + r, row0 + r + SOFTMAX_ROWS)
            for head in range(2):
                m_rows = mx[head][row0 + r:row0 + r + SOFTMAX_ROWS]
                acc = None
                for block in blocks:
                    p = jnp.exp2(load(rows, block, head, r, biases) - m_rows)
                    acc = p if acc is None else acc + p
                    tile = 2 * block + head
                    p_ref[rows, tile * LANES:(tile + 1) * LANES] = p.astype(BF16)
                stat_ref[1, rows, head * LANES:(head + 1) * LANES] = acc
    recip = [1.0 / (jnp.sum(stat_ref[1, :, head * LANES:(head + 1) * LANES], axis=-1, keepdims=True)
                    + jnp.exp2(sinks[head] - mx[head])) for head in range(2)]
    lane = lax.broadcasted_iota(jnp.int32, (1, LANES), 1)
    return jnp.where(lane < HEAD_DIM, recip[0], recip[1])


def _attention_pipeline(units, sink_ref, o_scr, stat_scr):
    def scores(u):
        u["s_ref"][...] = _bdot_nt(u["q"](), u["kk"]())

    def values(u, recip):
        o = _bdot(u["p_ref"][...], u["vv"]()) * recip
        o_scr[u["rows"], LANES * u["t"]:LANES * (u["t"] + 1)] = o.astype(BF16)

    scores(units[0])
    recip_prev = None
    for n, u in enumerate(units):
        if n + 1 < len(units):
            scores(units[n + 1])
        if n > 0:
            values(units[n - 1], recip_prev)
        sinks = (sink_ref[2 * u["t"]] * LOG2E, sink_ref[2 * u["t"] + 1] * LOG2E)
        recip_prev = _pair_softmax(u["s_ref"], u["p_ref"], stat_scr.at[n % 2], sinks, u["plan"])
    values(units[-1], recip_prev)


def _cached(fn):
    memo = {}

    def get(*key):
        if key not in memo:
            memo[key] = fn(*key)
        return memo[key]
    return get


def _prompt_attn_kernel(sink_ref, x_ref, mod_ref, wqkv_ref, wo_ref, g_ref, b_ref,
                        y_ref, k_ref, v_ref, o_scr, s_scr, p_scr, stat_scr, *, alpha):
    n_seq, seq, _ = x_ref.shape
    m = mod_ref[0]
    x = x_ref[...].reshape(n_seq * seq, D_MODEL)
    h = (x * (1.0 + m[1:2]) + m[0:1]).astype(BF16)
    qkv = _bdot(h, wqkv_ref[...])
    q = (qkv[:, :D_MODEL] * Q_SCALE).astype(BF16)
    plan = [(0, seq, list(range(seq // BLOCK)), {})]
    units = []
    for sq in range(n_seq):
        rows = slice(sq * seq, (sq + 1) * seq)
        k = qkv[rows, D_MODEL:QK_WIDTH]
        v = qkv[rows, QK_WIDTH:]
        k_ref[sq] = k
        v_ref[sq] = v
        kk = _cached(functools.partial(_head_pair_operand, k))
        vv = _cached(functools.partial(_head_pair_operand, v))
        for t in range(D_MODEL // LANES):
            n = len(units)
            units.append(dict(q=functools.partial(lambda rows, t: q[rows, LANES * t:LANES * (t + 1)], rows, t),
                              kk=functools.partial(kk, t // (GROUP // 2)),
                              vv=functools.partial(vv, t // (GROUP // 2)),
                              t=t, rows=rows, s_ref=s_scr.at[n % 2], p_ref=p_scr.at[n % 2], plan=plan))
    _attention_pipeline(units, sink_ref, o_scr, stat_scr)
    out = _bdot(o_scr[...], wo_ref[...])
    z = alpha * x + m[2:3] * out
    y_ref[...] = _layer_norm(z, g_ref[...], b_ref[...]).reshape(n_seq, seq, D_MODEL)


def _prompt_attn_layer(x, mod, sink, layer, attn_layer, wqkv, wo, g, b, alpha):
    bsz, seq, _ = x.shape
    n_seq = PROMPT_SEQS_PER_STEP
    assert bsz % n_seq == 0
    width = 2 * (seq // BLOCK) * LANES
    kern = functools.partial(_prompt_attn_kernel, alpha=alpha)
    return pl.pallas_call(
        kern,
        grid=(bsz // n_seq,),
        in_specs=[
            pl.BlockSpec(memory_space=pltpu.SMEM),
            pl.BlockSpec((n_seq, seq, D_MODEL), lambda i: (i, 0, 0)),
            pl.BlockSpec((1, MOD_ROWS, D_MODEL), lambda i: (0, 0, 0)),
            _layer_spec(wqkv.shape, attn_layer),
            _layer_spec(wo.shape, attn_layer),
            _layer_spec(g.shape, layer),
            _layer_spec(b.shape, layer),
        ],
        out_specs=[
            pl.BlockSpec((n_seq, seq, D_MODEL), lambda i: (i, 0, 0)),
            pl.BlockSpec((n_seq, seq, KV_WIDTH), lambda i: (i, 0, 0)),
            pl.BlockSpec((n_seq, seq, KV_WIDTH), lambda i: (i, 0, 0)),
        ],
        out_shape=[
            jax.ShapeDtypeStruct((bsz, seq, D_MODEL), F32),
            jax.ShapeDtypeStruct((bsz, seq, KV_WIDTH), F32),
            jax.ShapeDtypeStruct((bsz, seq, KV_WIDTH), F32),
        ],
        scratch_shapes=[pltpu.VMEM((n_seq * seq, D_MODEL), BF16),
                        pltpu.VMEM((2, seq, width), F32),
                        pltpu.VMEM((2, seq, width), BF16),
                        pltpu.VMEM((2, 2, seq, 2 * LANES), F32)],
        compiler_params=_params(1),
        name="prompt_attn",
    )(sink, x, mod, wqkv, wo, g, b)


def _rope_tables(n_tokens):
    rows = n_tokens // GRID_W
    row = jnp.repeat(jnp.arange(rows, dtype=F32), GRID_W)
    col = jnp.tile(jnp.arange(GRID_W, dtype=F32), rows)
    half = HEAD_DIM // 2
    inv = ROPE_THETA ** (-jnp.arange(0, half, 2, dtype=F32) / half)
    ang_r = row[:, None] * inv
    ang_c = col[:, None] * inv
    cos = jnp.concatenate([jnp.cos(ang_r)] * 2 + [jnp.cos(ang_c)] * 2, axis=-1)
    sin = jnp.concatenate([-jnp.sin(ang_r), jnp.sin(ang_r), -jnp.sin(ang_c), jnp.sin(ang_c)], axis=-1)
    return jnp.tile(cos, (1, LANES // HEAD_DIM)), jnp.tile(sin, (1, LANES // HEAD_DIM))


def _sample_qkv_kernel(x_ref, mod_ref, cos_ref, sin_ref, wqkv_ref, q_ref, kv_ref):
    x = x_ref[...]
    m = mod_ref[0]
    h = (x * (1.0 + m[1:2]) + m[0:1]).astype(BF16)
    qkv = _bdot(h, wqkv_ref[...])
    cos = cos_ref[...]
    sin = sin_ref[...]
    lane = lax.broadcasted_iota(jnp.int32, (1, LANES), 1)
    first = (lane % (HEAD_DIM // 2)) < (HEAD_DIM // 4)
    quarter = HEAD_DIM // 4
    for t in range(QK_WIDTH // LANES):
        tile = qkv[:, LANES * t:LANES * (t + 1)]
        partner = jnp.where(first, pltpu.roll(tile, LANES - quarter, axis=1),
                            pltpu.roll(tile, quarter, axis=1))
        rot = tile * cos + partner * sin
        if t < D_MODEL // LANES:
            q_ref[:, LANES * t:LANES * (t + 1)] = (rot * Q_SCALE).astype(BF16)
        else:
            tk = t - D_MODEL // LANES
            kv_ref[:, LANES * tk:LANES * (tk + 1)] = rot.astype(BF16)
    kv_ref[:, KV_WIDTH:] = qkv[:, QK_WIDTH:].astype(BF16)


def _sample_qkv(x, mod, seq, cos, sin, attn_layer, wqkv):
    m_rows = x.shape[0]
    tm = 512
    tiles_per_seq = seq // tm
    return pl.pallas_call(
        _sample_qkv_kernel,
        grid=(m_rows // tm,),
        in_specs=[
            pl.BlockSpec((tm, D_MODEL), lambda i: (i, 0)),
            pl.BlockSpec((1, MOD_ROWS, D_MODEL), lambda i: (i // tiles_per_seq, 0, 0)),
            pl.BlockSpec((tm, LANES), lambda i: (i % tiles_per_seq, 0)),
            pl.BlockSpec((tm, LANES), lambda i: (i % tiles_per_seq, 0)),
            _layer_spec(wqkv.shape, attn_layer),
        ],
        out_specs=[
            pl.BlockSpec((tm, D_MODEL), lambda i: (i, 0)),
            pl.BlockSpec((tm, 2 * KV_WIDTH), lambda i: (i, 0)),
        ],
        out_shape=[
            jax.ShapeDtypeStruct((m_rows, D_MODEL), BF16),
            jax.ShapeDtypeStruct((m_rows, 2 * KV_WIDTH), BF16),
        ],
        compiler_params=_params(1),
        name="sample_qkv",
    )(x, mod, cos, sin, wqkv)


def _sample_attn_kernel(sink_ref, x_ref, mod_ref, q_ref, kvp_ref, kvm_ref, kvn_ref, ctx_ref,
                        wo_ref, g_ref, b_ref, y_ref, o_scr, s_scr, p_scr, stat_scr, bias_scr, *,
                        alpha):
    i = pl.program_id(1)
    n_tiles = pl.num_programs(1)
    m = mod_ref[0]
    nc = ctx_ref.shape[1] // BLOCK
    window = 4 + nc
    half_rows = 2 * BLOCK
    kv_all = jnp.concatenate([kvp_ref[...].astype(F32), kvm_ref[0:BLOCK, :].astype(F32), ctx_ref[0],
                              kvm_ref[BLOCK:, :].astype(F32), kvn_ref[...].astype(F32)], axis=0)
    k_all, v_all = kv_all[:, :KV_WIDTH], kv_all[:, KV_WIDTH:]
    qi = lax.broadcasted_iota(jnp.int32, (BLOCK, BLOCK), 0)
    ki = lax.broadcasted_iota(jnp.int32, (BLOCK, BLOCK), 1)
    bias_scr[0] = jnp.where(ki >= qi, jnp.where(i > 0, 0.0, NEG_INF), NEG_INF)
    bias_scr[1] = jnp.where(ki <= qi, 0.0, NEG_INF)
    bias_scr[2] = jnp.where(ki >= qi, 0.0, NEG_INF)
    bias_scr[3] = jnp.where(ki <= qi, jnp.where(i < n_tiles - 1, 0.0, NEG_INF), NEG_INF)
    prev_edge, next_plain, prev_plain, next_edge = (bias_scr.at[j] for j in range(4))
    ctx_a = list(range(2, 2 + nc))
    ctx_b = list(range(nc))
    plans = [
        [(0, BLOCK, [0, 1] + ctx_a + [2 + nc], {0: prev_edge, 2 + nc: next_plain}),
         (BLOCK, BLOCK, [1] + ctx_a + [2 + nc, 3 + nc], {1: prev_plain, 3 + nc: next_plain})],
        [(0, BLOCK, ctx_b + [nc, nc + 1, nc + 2], {nc: prev_plain, nc + 2: next_plain}),
         (BLOCK, BLOCK, ctx_b + [nc + 1, nc + 2, nc + 3], {nc + 1: prev_plain, nc + 3: next_edge})],
    ]
    unused = [[3 + nc, 0], [nc + 3, nc]]
    for half in range(2):
        for parity in range(2):
            for qb in range(2):
                blk = unused[half][qb]
                p_scr[2 * half + parity, qb * BLOCK:(qb + 1) * BLOCK,
                      2 * blk * LANES:2 * (blk + 1) * LANES] = jnp.zeros((BLOCK, 2 * LANES), BF16)
    kk = _cached(lambda kv_head: _head_pair_operand(k_all, kv_head))
    vv = _cached(lambda kv_head: _head_pair_operand(v_all, kv_head))

    def operand(full, kv_head, half):
        first = 2 * half * 2 * BLOCK
        return full(kv_head)[first:first + window * 2 * BLOCK]

    units = []
    for kv_head in range(N_KV_HEADS):
        for half in range(2):
            for t in range(kv_head * (GROUP // 2), (kv_head + 1) * (GROUP // 2)):
                n = len(units)
                rows = slice(half * half_rows, (half + 1) * half_rows)
                units.append(dict(
                    q=functools.partial(lambda rows, t: q_ref[rows, LANES * t:LANES * (t + 1)], rows, t),
                    kk=functools.partial(operand, kk, kv_head, half),
                    vv=functools.partial(operand, vv, kv_head, half),
                    t=t, rows=rows, s_ref=s_scr.at[n % 2], p_ref=p_scr.at[2 * half + n % 2],
                    plan=plans[half]))
    _attention_pipeline(units, sink_ref, o_scr, stat_scr)
    out = _bdot(o_scr[...], wo_ref[...])
    z = alpha * x_ref[...] + m[2:3] * out
    y_ref[...] = _layer_norm(z, g_ref[...], b_ref[...])


def _sample_attn_layer(x, mod, seq, q, kv, ctx_kv, sink, layer, attn_layer, wo, g, b, alpha):
    m_rows = x.shape[0]
    bsz = m_rows // seq
    tq = 4 * BLOCK
    tiles = seq // tq
    blocks = seq // BLOCK
    n_ctx = ctx_kv.shape[1]
    assert n_ctx % BLOCK == 0 and seq % tq == 0
    window = 4 + n_ctx // BLOCK
    kern = functools.partial(_sample_attn_kernel, alpha=alpha)
    row_tile = lambda bi, i: (bi * tiles + i, 0)
    prev_blk = lambda bi, i: (bi * blocks + jnp.maximum(4 * i - 1, 0), 0)
    next_blk = lambda bi, i: (bi * blocks + jnp.minimum(4 * i + 4, blocks - 1), 0)
    ctx_blk = lambda bi, i: (bi, 0, 0)
    half_rows = 2 * BLOCK
    width = 2 * window * LANES
    return pl.pallas_call(
        kern,
        grid=(bsz, tiles),
        in_specs=[
            pl.BlockSpec(memory_space=pltpu.SMEM),
            pl.BlockSpec((tq, D_MODEL), row_tile),
            pl.BlockSpec((1, MOD_ROWS, D_MODEL), ctx_blk),
            pl.BlockSpec((tq, D_MODEL), row_tile),
            pl.BlockSpec((BLOCK, 2 * KV_WIDTH), prev_blk),
            pl.BlockSpec((tq, 2 * KV_WIDTH), row_tile),
            pl.BlockSpec((BLOCK, 2 * KV_WIDTH), next_blk),
            pl.BlockSpec((1, n_ctx, 2 * KV_WIDTH), ctx_blk),
            _layer_spec(wo.shape, attn_layer),
            _layer_spec(g.shape, layer),
            _layer_spec(b.shape, layer),
        ],
        out_specs=pl.BlockSpec((tq, D_MODEL), row_tile),
        out_shape=jax.ShapeDtypeStruct((m_rows, D_MODEL), F32),
        scratch_shapes=[pltpu.VMEM((tq, D_MODEL), BF16),
                        pltpu.VMEM((2, half_rows, width), F32),
                        pltpu.VMEM((4, half_rows, width), BF16),
                        pltpu.VMEM((2, 2, half_rows, 2 * LANES), F32),
                        pltpu.VMEM((4, BLOCK, BLOCK), F32)],
        compiler_params=_params(2),
        name="sample_attn",
    )(sink, x, mod, q, kv, kv, kv, ctx_kv, wo, g, b)


def _glu_weight_kernel(value_ref, gate_ref, out_ref):
    out_ref[:, :LANES] = value_ref[...].astype(BF16)
    out_ref[:, LANES:] = gate_ref[...].astype(BF16)


def _glu_weight_tiles(pw1):
    n_layers = pw1.shape[0]
    lane_tiles = D_MODEL // LANES
    return pl.pallas_call(
        _glu_weight_kernel,
        grid=(n_layers, lane_tiles),
        in_specs=[
            pl.BlockSpec((None, D_MODEL, LANES), lambda l, c: (l, 0, c)),
            pl.BlockSpec((None, D_MODEL, LANES), lambda l, c: (l, 0, lane_tiles + c)),
        ],
        out_specs=pl.BlockSpec((None, None, D_MODEL, 2 * LANES), lambda l, c: (l, c, 0, 0)),
        out_shape=jax.ShapeDtypeStruct((n_layers, lane_tiles, D_MODEL, 2 * LANES), BF16),
        compiler_params=_params(2),
        name="glu_weight_tiles",
    )(pw1, pw1)


def _conv_kernel(xp_ref, x_ref, xn_ref, mod_ref, pw1_ref, pb1_ref, dw_ref, db_ref,
                 cg_ref, cb_ref, pw2_ref, pb2_ref, g_ref, b_ref, y_ref, *scratch,
                 alpha, tm, sub, tiles_per_seq, seq_per_sub):
    i = pl.program_id(0)
    t = i % tiles_per_seq
    m = mod_ref[0]
    lane_tiles = D_MODEL // LANES
    n_sub = tm // sub
    h_scrs, u_scrs, c_scrs = scratch[:n_sub], scratch[n_sub:2 * n_sub], scratch[2 * n_sub:]
    recompute_halo = not seq_per_sub

    def modulated_input(s):
        r0 = s * sub
        rows = x_ref[r0:r0 + sub, :]
        if recompute_halo:
            top = xp_ref[...] if s == 0 else x_ref[r0 - HALO:r0, :]
            bot = xn_ref[...] if s == n_sub - 1 else x_ref[r0 + sub:r0 + sub + HALO, :]
            rows = jnp.concatenate([top, rows, bot], axis=0)
        h_scrs[s][...] = (rows * (1.0 + m[1:2]) + m[0:1]).astype(BF16)

    def glu_chunk(s, c):
        a = _bdot(h_scrs[s][...], pw1_ref[c]) + pb1_ref[c]
        u = a[:, :LANES] * jax.nn.sigmoid(a[:, LANES:])
        zeros = jnp.zeros((HALO, LANES), F32)
        if recompute_halo:
            u_scrs[s][c, :, :] = u
            if s == 0:
                u_scrs[s][c, 0:HALO, :] = jnp.where(t > 0, u[:HALO], 0.0)
            if s == n_sub - 1:
                u_scrs[s][c, HALO + sub:, :] = jnp.where(t < tiles_per_seq - 1, u[HALO + sub:], 0.0)
        else:
            u_scrs[s][c, 0:HALO, :] = zeros
            u_scrs[s][c, HALO:HALO + sub, :] = u
            u_scrs[s][c, HALO + sub:, :] = zeros

    def conv_taps(s, c):
        span_rows = SUBLANES * CONV_ROW_STRIDE
        for base in range(0, sub, CONV_ROW_BLOCK):
            starts = [base + sp + j for sp in range(0, CONV_ROW_BLOCK, span_rows)
                      for j in range(CONV_ROW_STRIDE)]
            bias = db_ref[c]
            accs = [jnp.zeros((SUBLANES, LANES), F32) + bias for _ in starts]
            for tap in range(CONV_WIDTH):
                w = dw_ref[c, tap:tap + 1, :]
                off = HALO - CONV_PAD + tap
                for n, start in enumerate(starts):
                    rows = pl.ds(start + off, SUBLANES, stride=CONV_ROW_STRIDE)
                    accs[n] = accs[n] + w * u_scrs[s][c, rows, :]
            for n, start in enumerate(starts):
                c_scrs[s][c, pl.ds(start, SUBLANES, stride=CONV_ROW_STRIDE), :] = accs[n]

    def finish(s):
        r0 = s * sub
        x = x_ref[r0:r0 + sub, :]
        conv = jnp.concatenate([c_scrs[s][c] for c in range(lane_tiles)], axis=1)
        un = _layer_norm(conv, cg_ref[...], cb_ref[...])
        un = un * jax.nn.sigmoid(un)
        out = _bdot(un.astype(BF16), pw2_ref[...]) + pb2_ref[...]
        z = alpha * x + m[2:3] * out
        y_ref[r0:r0 + sub, :] = _layer_norm(z, g_ref[...], b_ref[...])

    for s in range(n_sub):
        modulated_input(s)
    for c in range(lane_tiles):
        glu_chunk(0, c)
    for s in range(n_sub):
        for c in range(lane_tiles):
            conv_taps(s, c)
            if s + 1 < n_sub:
                glu_chunk(s + 1, c)
        finish(s)


def _conv_layer(x, mod, rows_per_group, seq, layer, conv_layer, pw1, pb1, dw, db, cg, cb, pw2, pb2,
                g, b, alpha):
    m_rows = x.shape[0]
    tm = CONV_TILE_ROWS
    sub = CONV_SUB_ROWS
    seq_per_sub = seq == sub
    assert seq_per_sub or seq % tm == 0
    tiles_per_seq = max(seq // tm, 1)
    in_rows = sub if seq_per_sub else sub + 2 * HALO
    halo_per_tile = tm // HALO
    n_halo_blocks = m_rows // HALO
    kern = functools.partial(_conv_kernel, alpha=alpha, tm=tm, sub=sub, tiles_per_seq=tiles_per_seq,
                             seq_per_sub=seq_per_sub)
    prev_blk = lambda i: (jnp.maximum(i * halo_per_tile - 1, 0), 0)
    next_blk = lambda i: (jnp.minimum((i + 1) * halo_per_tile, n_halo_blocks - 1), 0)
    return pl.pallas_call(
        kern,
        grid=(m_rows // tm,),
        in_specs=[
            pl.BlockSpec((HALO, D_MODEL), prev_blk),
            pl.BlockSpec((tm, D_MODEL), lambda i: (i, 0)),
            pl.BlockSpec((HALO, D_MODEL), next_blk),
            pl.BlockSpec((1, MOD_ROWS, D_MODEL), lambda i: (i * tm // rows_per_group, 0, 0)),
            _layer_spec(pw1.shape, conv_layer),
            _layer_spec(pb1.shape, conv_layer),
            _layer_spec(dw.shape, conv_layer),
            _layer_spec(db.shape, conv_layer),
            _layer_spec(cg.shape, conv_layer),
            _layer_spec(cb.shape, conv_layer),
            _layer_spec(pw2.shape, conv_layer),
            _layer_spec(pb2.shape, conv_layer),
            _layer_spec(g.shape, layer),
            _layer_spec(b.shape, layer),
        ],
        out_specs=pl.BlockSpec((tm, D_MODEL), lambda i: (i, 0)),
        out_shape=jax.ShapeDtypeStruct((m_rows, D_MODEL), F32),
        scratch_shapes=([pltpu.VMEM((in_rows, D_MODEL), BF16)] * (tm // sub)
                        + [pltpu.VMEM((D_MODEL // LANES, sub + 2 * HALO, LANES), F32)] * (tm // sub)
                        + [pltpu.VMEM((D_MODEL // LANES, sub, LANES), F32)] * (tm // sub)),
        compiler_params=_params(1),
        name="conv",
    )(x, x, x, mod, pw1, pb1, dw, db, cg, cb, pw2, pb2, g, b)


def kernel(x_prompt, x_sample, cache_k, cache_v, c, c_ctx, ada_w, ada_b, attn_w_qkv, attn_w_o, attn_sink, conv_pw1_w, conv_pw1_b, conv_dw_w, conv_dw_b, conv_norm_g, conv_norm_b, conv_pw2_w, conv_pw2_b, ln1_g, ln1_b, mlp_w1, mlp_b1, mlp_w2, mlp_b2, ln2_g, ln2_b):
    depth = ada_w.shape[0]
    bsz, seq, _ = x_prompt.shape
    dec_bsz, dec_seq, _ = x_sample.shape
    n_ctx = cache_k.shape[2]
    alpha = (2.0 * depth) ** 0.25
    assert 1 + dec_bsz <= COND_ROWS

    rows = lambda a: a.reshape(a.shape[0], 1, a.shape[-1])
    lane_tiles = D_MODEL // LANES

    cond = jnp.concatenate([c_ctx[None], c, jnp.zeros((COND_ROWS - 1 - dec_bsz, D_MODEL), F32)], axis=0)
    mods = _modulation(cond, 1 + dec_bsz, ada_w, ada_b).reshape(depth, COND_ROWS, N_MOD, D_MODEL)
    mods = jnp.pad(mods, ((0, 0), (0, 0), (0, MOD_ROWS - N_MOD), (0, 0)))

    cos, sin = _rope_tables(dec_seq)

    wqkv, wo = attn_w_qkv.astype(BF16), attn_w_o.astype(BF16)
    pw1, pw2 = _glu_weight_tiles(conv_pw1_w), conv_pw2_w.astype(BF16)
    w1, w2 = mlp_w1.astype(BF16), mlp_w2.astype(BF16)
    ln1 = (rows(ln1_g), rows(ln1_b))
    n_conv = conv_dw_w.shape[0]
    dw = conv_dw_w.reshape(n_conv, CONV_WIDTH, lane_tiles, LANES).transpose(0, 2, 1, 3)
    db = conv_dw_b.reshape(n_conv, lane_tiles, 1, LANES)
    pb1 = conv_pw1_b.reshape(n_conv, 2, lane_tiles, 1, LANES).transpose(0, 2, 3, 1, 4)
    pb1 = pb1.reshape(n_conv, lane_tiles, 1, 2 * LANES)
    conv_w = (pw1, pb1, dw, db, rows(conv_norm_g), rows(conv_norm_b), pw2, rows(conv_pw2_b))
    mlp_w = (w1, rows(mlp_b1), w2, rows(mlp_b2), rows(ln2_g), rows(ln2_b))

    yp = x_prompt
    ys = x_sample.reshape(dec_bsz * dec_seq, D_MODEL)
    new_k, new_v = [], []
    for i in range(depth):
        mod_p = mods[i, 0:1]
        mod_s = mods[i, 1:1 + dec_bsz]
        j = i // N_MIXERS
        if i % N_MIXERS == 0:
            yp, kp, vp = _prompt_attn_layer(yp, mod_p, attn_sink[j], i, j, wqkv, wo, *ln1, alpha)
            new_k.append(kp.reshape(bsz, seq, N_KV_HEADS, HEAD_DIM))
            new_v.append(vp.reshape(bsz, seq, N_KV_HEADS, HEAD_DIM))
            q, kv = _sample_qkv(ys, mod_s, dec_seq, cos, sin, j, wqkv)
            ctx_kv = jnp.concatenate([cache_k[:, j].reshape(dec_bsz, n_ctx, KV_WIDTH),
                                      cache_v[:, j].reshape(dec_bsz, n_ctx, KV_WIDTH)], axis=-1)
            ys = _sample_attn_layer(ys, mod_s, dec_seq, q, kv, ctx_kv, attn_sink[j], i, j, wo, *ln1, alpha)
        else:
            yp = _conv_layer(yp.reshape(bsz * seq, D_MODEL), mod_p, bsz * seq, seq, i, j, *conv_w, *ln1,
                             alpha).reshape(bsz, seq, D_MODEL)
            ys = _conv_layer(ys, mod_s, dec_seq, dec_seq, i, j, *conv_w, *ln1, alpha)
        yp = _mlp(yp.reshape(bsz * seq, D_MODEL), mod_p, bsz * seq, i, *mlp_w, alpha).reshape(bsz, seq, D_MODEL)
        ys = _mlp(ys, mod_s, dec_seq, i, *mlp_w, alpha)
    new_cache_k = jnp.stack(new_k, axis=1)
    new_cache_v = jnp.stack(new_v, axis=1)
    return (yp, ys.reshape(dec_bsz, dec_seq, D_MODEL), new_cache_k, new_cache_v)
```

```python
import functools
import math

import jax
import jax.numpy as jnp
import numpy as np
from jax import lax
from jax.experimental import pallas as pl
from jax.experimental.pallas import tpu as pltpu

D_MODEL = 1024
GRID_W = 64
N_MIXERS = 2
HEAD_DIM = 64
N_HEADS = D_MODEL // HEAD_DIM
N_KV_HEADS = 4
GROUP = N_HEADS // N_KV_HEADS
KV_WIDTH = N_KV_HEADS * HEAD_DIM
QK_WIDTH = D_MODEL + KV_WIDTH
QKV_WIDTH = D_MODEL + 2 * KV_WIDTH
BLOCK = 128
ROPE_THETA = 10000.0
CONV_WIDTH = 31
CONV_PAD = (CONV_WIDTH - 1) // 2
D_FF = 4 * D_MODEL
N_MOD = 6
LN_EPS = 1e-5
NEG_INF = -1e30
ATTN_SCALE = HEAD_DIM ** -0.5
LOG2E = math.log2(math.e)
Q_SCALE = ATTN_SCALE * LOG2E

LANES = 128
SUBLANES = 8
VMEM_LIMIT_BYTES = 56 * 1024 * 1024

COND_ROWS = 8
HALO = 16
CONV_ROW_STRIDE = 4
CONV_ROW_BLOCK = 128
CONV_TILE_ROWS = 512
CONV_SUB_ROWS = 256
SOFTMAX_ROWS = 64
PROMPT_SEQS_PER_STEP = 4
MOD_TILE_COLS = 3072
MLP_TILE_ROWS = 512
MLP_SUB_ROWS = 256

BF16 = jnp.bfloat16
F32 = jnp.float32


def _layer_spec(shape, layer):
    n = len(shape)
    return pl.BlockSpec((None,) + tuple(shape[1:]), lambda *_: (layer,) + (0,) * (n - 1),
                        pipeline_mode=pl.Buffered(1))


def _mod_spec(layer, cond_of_step):
    return pl.BlockSpec((None, None, N_MOD, D_MODEL), lambda *ids: (layer, cond_of_step(*ids), 0, 0))


def _params(n_axes):
    return pltpu.CompilerParams(dimension_semantics=("arbitrary",) * n_axes,
                                vmem_limit_bytes=VMEM_LIMIT_BYTES)


def _layer_norm(z, g, b):
    mu = jnp.mean(z, axis=-1, keepdims=True)
    zc = z - mu
    var = jnp.mean(zc * zc, axis=-1, keepdims=True)
    return zc * lax.rsqrt(var + LN_EPS) * g + b


def _bdot(a, b):
    return jnp.dot(a, b, preferred_element_type=F32)


def _bdot_nt(a, b):
    return lax.dot_general(a, b, (((1,), (1,)), ((), ())), preferred_element_type=F32)


def _mod_kernel(cond_ref, w_ref, b_ref, out_ref, s_scr, *, n_cond):
    @pl.when((pl.program_id(0) == 0) & (pl.program_id(1) == 0))
    def _():
        c = cond_ref[...]
        s_t = (c * jax.nn.sigmoid(c)).T
        for m in range(n_cond):
            s_scr[m] = jnp.broadcast_to(s_t[:, m:m + 1], (D_MODEL, LANES))

    zero_rows = [jnp.zeros((1, LANES), F32)] * (COND_ROWS - n_cond)
    for n in range(w_ref.shape[2] // LANES):
        cols = slice(n * LANES, (n + 1) * LANES)
        accs = [jnp.zeros((SUBLANES, LANES), F32) for _ in range(n_cond)]
        for k in range(0, D_MODEL, SUBLANES):
            w = w_ref[0, k:k + SUBLANES, cols]
            for m in range(n_cond):
                accs[m] = accs[m] + w * s_scr[m, k:k + SUBLANES, :]
        out_rows = [jnp.sum(a, axis=0, keepdims=True) for a in accs] + zero_rows
        out_ref[0, :, cols] = jnp.concatenate(out_rows, axis=0) + b_ref[0, :, cols]


def _modulation(cond, n_cond, ada_w, ada_b):
    depth = ada_w.shape[0]
    tn = MOD_TILE_COLS
    width = N_MOD * D_MODEL
    return pl.pallas_call(
        functools.partial(_mod_kernel, n_cond=n_cond),
        grid=(depth, width // tn),
        in_specs=[
            pl.BlockSpec((COND_ROWS, D_MODEL), lambda l, n: (0, 0)),
            pl.BlockSpec((1, D_MODEL, tn), lambda l, n: (l, 0, n)),
            pl.BlockSpec((1, 1, tn), lambda l, n: (l, 0, n)),
        ],
        out_specs=pl.BlockSpec((1, COND_ROWS, tn), lambda l, n: (l, 0, n)),
        out_shape=jax.ShapeDtypeStruct((depth, COND_ROWS, width), F32),
        scratch_shapes=[pltpu.VMEM((n_cond, D_MODEL, LANES), F32)],
        compiler_params=_params(2),
        name="modulation",
    )(cond, ada_w, ada_b.reshape(depth, 1, width))


def _mlp_kernel(y_ref, mod_ref, w1_ref, b1_ref, w2_ref, b2_ref, g_ref, b_ref, out_ref, *,
                alpha, ff_chunk):
    m = mod_ref[...]
    for r0 in range(0, y_ref.shape[0], MLP_SUB_ROWS):
        rows = slice(r0, r0 + MLP_SUB_ROWS)
        y = y_ref[rows, :]
        h = (y * (1.0 + m[4:5]) + m[3:4]).astype(BF16)
        acc = jnp.zeros(y.shape, F32)
        for c in range(D_FF // ff_chunk):
            cols = slice(c * ff_chunk, (c + 1) * ff_chunk)
            a = _bdot(h, w1_ref[:, cols]) + b1_ref[:, cols]
            a = jnp.maximum(a, 0.0)
            acc = acc + _bdot((a * a).astype(BF16), w2_ref[cols, :])
        z = alpha * y + m[5:6] * (acc + b2_ref[...])
        out_ref[rows, :] = _layer_norm(z, g_ref[...], b_ref[...])


def _mlp(y, mods, cond0, rows_per_group, layer, w1, b1, w2, b2, g, b, alpha):
    m_rows = y.shape[0]
    tm = MLP_TILE_ROWS
    kern = functools.partial(_mlp_kernel, alpha=alpha, ff_chunk=1024)
    return pl.pallas_call(
        kern,
        grid=(m_rows // tm,),
        in_specs=[
            pl.BlockSpec((tm, D_MODEL), lambda i: (i, 0)),
            _mod_spec(layer, lambda i: cond0 + i * tm // rows_per_group),
            _layer_spec(w1.shape, layer),
            _layer_spec(b1.shape, layer),
            _layer_spec(w2.shape, layer),
            _layer_spec(b2.shape, layer),
            _layer_spec(g.shape, layer),
            _layer_spec(b.shape, layer),
        ],
        out_specs=pl.BlockSpec((tm, D_MODEL), lambda i: (i, 0)),
        out_shape=jax.ShapeDtypeStruct((m_rows, D_MODEL), F32),
        compiler_params=_params(1),
        name="mlp",
    )(y, mods, w1, b1, w2, b2, g, b)


def _head_pair_operand(x, kv_head):
    tile = x[:, LANES * (kv_head // 2):LANES * (kv_head // 2 + 1)]
    swapped = pltpu.roll(tile, HEAD_DIM, axis=1)
    lane = lax.broadcasted_iota(jnp.int32, (1, LANES), 1)
    low = lane < HEAD_DIM
    if kv_head % 2 == 0:
        lo_src, hi_src = tile, swapped
    else:
        lo_src, hi_src = swapped, tile
    lo = jnp.where(low, lo_src, 0.0).astype(BF16)
    hi = jnp.where(low, 0.0, hi_src).astype(BF16)
    pieces = []
    for blk in range(x.shape[0] // BLOCK):
        pieces += [lo[blk * BLOCK:(blk + 1) * BLOCK], hi[blk * BLOCK:(blk + 1) * BLOCK]]
    return jnp.concatenate(pieces, axis=0)


def _pair_softmax(s_ref, p_ref, stat_ref, sinks, row_plan):
    def load(rows, block, head, plan_row, biases):
        tile = 2 * block + head
        v = s_ref[rows, tile * LANES:(tile + 1) * LANES]
        if block in biases:
            v = v + biases[block][plan_row:plan_row + SOFTMAX_ROWS, :]
        return v

    for row0, n_rows, blocks, biases in row_plan:
        for r in range(0, n_rows, SOFTMAX_ROWS):
            rows = slice(row0 + r, row0 + r + SOFTMAX_ROWS)
            for head in range(2):
                acc = None
                for block in blocks:
                    v = load(rows, block, head, r, biases)
                    acc = v if acc is None else jnp.maximum(acc, v)
                stat_ref[0, rows, head * LANES:(head + 1) * LANES] = acc
    mx = [jnp.maximum(jnp.max(stat_ref[0, :, head * LANES:(head + 1) * LANES], axis=-1, keepdims=True),
                      sinks[head]) for head in range(2)]
    for row0, n_rows, blocks, biases in row_plan:
        for r in range(0, n_rows, SOFTMAX_ROWS):
            rows = slice(row0 + r, row0 + r + SOFTMAX_ROWS)
            for head in range(2):
                m_rows = mx[head][row0 + r:row0 + r + SOFTMAX_ROWS]
                acc = None
                for block in blocks:
                    p = jnp.exp2(load(rows, block, head, r, biases) - m_rows)
                    acc = p if acc is None else acc + p
                    tile = 2 * block + head
                    p_ref[rows, tile * LANES:(tile + 1) * LANES] = p.astype(BF16)
                stat_ref[1, rows, head * LANES:(head + 1) * LANES] = acc
    recip = [1.0 / (jnp.sum(stat_ref[1, :, head * LANES:(head + 1) * LANES], axis=-1, keepdims=True)
                    + jnp.exp2(sinks[head] - mx[head])) for head in range(2)]
    lane = lax.broadcasted_iota(jnp.int32, (1, LANES), 1)
    return jnp.where(lane < HEAD_DIM, recip[0], recip[1])


def _attention_pipeline(units, sink_ref, attn_layer, o_scr, stat_scr):
    def scores(u):
        u["s_ref"][...] = _bdot_nt(u["q"](), u["kk"]())

    def values(u, recip):
        o = _bdot(u["p_ref"][...], u["vv"]()) * recip
        o_scr[u["rows"], LANES * u["t"]:LANES * (u["t"] + 1)] = o.astype(BF16)

    scores(units[0])
    recip_prev = None
    for n, u in enumerate(units):
        if n + 1 < len(units):
            scores(units[n + 1])
        if n > 0:
            values(units[n - 1], recip_prev)
        sinks = (sink_ref[attn_layer, 2 * u["t"]] * LOG2E, sink_ref[attn_layer, 2 * u["t"] + 1] * LOG2E)
        recip_prev = _pair_softmax(u["s_ref"], u["p_ref"], stat_scr.at[n % 2], sinks, u["plan"])
    values(units[-1], recip_prev)


def _cached(fn):
    memo = {}

    def get(*key):
        if key not in memo:
            memo[key] = fn(*key)
        return memo[key]
    return get


def _prompt_attn_kernel(sink_ref, x_ref, mod_ref, wqkv_ref, wo_ref, g_ref, b_ref,
                        y_ref, k_ref, v_ref, o_scr, s_scr, p_scr, stat_scr, *, alpha, attn_layer):
    n_seq, seq, _ = x_ref.shape
    m = mod_ref[...]
    x = x_ref[...].reshape(n_seq * seq, D_MODEL)
    h = (x * (1.0 + m[1:2]) + m[0:1]).astype(BF16)
    qkv = _bdot(h, wqkv_ref[...])
    q = (qkv[:, :D_MODEL] * Q_SCALE).astype(BF16)
    plan = [(0, seq, list(range(seq // BLOCK)), {})]
    units = []
    for sq in range(n_seq):
        rows = slice(sq * seq, (sq + 1) * seq)
        k = qkv[rows, D_MODEL:QK_WIDTH]
        v = qkv[rows, QK_WIDTH:]
        k_ref[sq] = k
        v_ref[sq] = v
        kk = _cached(functools.partial(_head_pair_operand, k))
        vv = _cached(functools.partial(_head_pair_operand, v))
        for t in range(D_MODEL // LANES):
            n = len(units)
            units.append(dict(q=functools.partial(lambda rows, t: q[rows, LANES * t:LANES * (t + 1)], rows, t),
                              kk=functools.partial(kk, t // (GROUP // 2)),
                              vv=functools.partial(vv, t // (GROUP // 2)),
                              t=t, rows=rows, s_ref=s_scr.at[n % 2], p_ref=p_scr.at[n % 2], plan=plan))
    _attention_pipeline(units, sink_ref, attn_layer, o_scr, stat_scr)
    out = _bdot(o_scr[...], wo_ref[...])
    z = alpha * x + m[2:3] * out
    y_ref[...] = _layer_norm(z, g_ref[...], b_ref[...]).reshape(n_seq, seq, D_MODEL)


def _prompt_attn_layer(x, mods, cond0, sink, layer, attn_layer, wqkv, wo, g, b, alpha):
    bsz, seq, _ = x.shape
    n_seq = PROMPT_SEQS_PER_STEP
    assert bsz % n_seq == 0
    width = 2 * (seq // BLOCK) * LANES
    kern = functools.partial(_prompt_attn_kernel, alpha=alpha, attn_layer=attn_layer)
    return pl.pallas_call(
        kern,
        grid=(bsz // n_seq,),
        in_specs=[
            pl.BlockSpec(memory_space=pltpu.SMEM),
            pl.BlockSpec((n_seq, seq, D_MODEL), lambda i: (i, 0, 0)),
            _mod_spec(layer, lambda i: cond0),
            _layer_spec(wqkv.shape, attn_layer),
            _layer_spec(wo.shape, attn_layer),
            _layer_spec(g.shape, layer),
            _layer_spec(b.shape, layer),
        ],
        out_specs=[
            pl.BlockSpec((n_seq, seq, D_MODEL), lambda i: (i, 0, 0)),
            pl.BlockSpec((n_seq, seq, KV_WIDTH), lambda i: (i, 0, 0)),
            pl.BlockSpec((n_seq, seq, KV_WIDTH), lambda i: (i, 0, 0)),
        ],
        out_shape=[
            jax.ShapeDtypeStruct((bsz, seq, D_MODEL), F32),
            jax.ShapeDtypeStruct((bsz, seq, KV_WIDTH), F32),
            jax.ShapeDtypeStruct((bsz, seq, KV_WIDTH), F32),
        ],
        scratch_shapes=[pltpu.VMEM((n_seq * seq, D_MODEL), BF16),
                        pltpu.VMEM((2, seq, width), F32),
                        pltpu.VMEM((2, seq, width), BF16),
                        pltpu.VMEM((2, 2, seq, 2 * LANES), F32)],
        compiler_params=_params(1),
        name="prompt_attn",
    )(sink, x, mods, wqkv, wo, g, b)


def _rope_tables(n_tokens):
    rows = n_tokens // GRID_W
    row = np.repeat(np.arange(rows, dtype=np.float32), GRID_W)
    col = np.tile(np.arange(GRID_W, dtype=np.float32), rows)
    half = HEAD_DIM // 2
    inv = np.float32(ROPE_THETA) ** (-np.arange(0, half, 2, dtype=np.float32) / np.float32(half))
    ang_r = row[:, None] * inv
    ang_c = col[:, None] * inv
    cos = np.concatenate([np.cos(ang_r)] * 2 + [np.cos(ang_c)] * 2, axis=-1)
    sin = np.concatenate([-np.sin(ang_r), np.sin(ang_r), -np.sin(ang_c), np.sin(ang_c)], axis=-1)
    reps = (1, LANES // HEAD_DIM)
    return jnp.asarray(np.tile(cos, reps), F32), jnp.asarray(np.tile(sin, reps), F32)


def _sample_qkv_kernel(x_ref, mod_ref, cos_ref, sin_ref, wqkv_ref, q_ref, kv_ref):
    x = x_ref[...]
    m = mod_ref[...]
    h = (x * (1.0 + m[1:2]) + m[0:1]).astype(BF16)
    qkv = _bdot(h, wqkv_ref[...])
    cos = cos_ref[...]
    sin = sin_ref[...]
    lane = lax.broadcasted_iota(jnp.int32, (1, LANES), 1)
    first = (lane % (HEAD_DIM // 2)) < (HEAD_DIM // 4)
    quarter = HEAD_DIM // 4
    for t in range(QK_WIDTH // LANES):
        tile = qkv[:, LANES * t:LANES * (t + 1)]
        partner = jnp.where(first, pltpu.roll(tile, LANES - quarter, axis=1),
                            pltpu.roll(tile, quarter, axis=1))
        rot = tile * cos + partner * sin
        if t < D_MODEL // LANES:
            q_ref[:, LANES * t:LANES * (t + 1)] = (rot * Q_SCALE).astype(BF16)
        else:
            tk = t - D_MODEL // LANES
            kv_ref[:, LANES * tk:LANES * (tk + 1)] = rot.astype(BF16)
    kv_ref[:, KV_WIDTH:] = qkv[:, QK_WIDTH:].astype(BF16)


def _sample_qkv(x, mods, cond0, seq, cos, sin, layer, attn_layer, wqkv):
    m_rows = x.shape[0]
    tm = 512
    tiles_per_seq = seq // tm
    return pl.pallas_call(
        _sample_qkv_kernel,
        grid=(m_rows // tm,),
        in_specs=[
            pl.BlockSpec((tm, D_MODEL), lambda i: (i, 0)),
            _mod_spec(layer, lambda i: cond0 + i // tiles_per_seq),
            pl.BlockSpec((tm, LANES), lambda i: (i % tiles_per_seq, 0)),
            pl.BlockSpec((tm, LANES), lambda i: (i % tiles_per_seq, 0)),
            _layer_spec(wqkv.shape, attn_layer),
        ],
        out_specs=[
            pl.BlockSpec((tm, D_MODEL), lambda i: (i, 0)),
            pl.BlockSpec((tm, 2 * KV_WIDTH), lambda i: (i, 0)),
        ],
        out_shape=[
            jax.ShapeDtypeStruct((m_rows, D_MODEL), BF16),
            jax.ShapeDtypeStruct((m_rows, 2 * KV_WIDTH), BF16),
        ],
        compiler_params=_params(1),
        name="sample_qkv",
    )(x, mods, cos, sin, wqkv)


def _sample_attn_kernel(sink_ref, x_ref, mod_ref, q_ref, kvp_ref, kvm_ref, kvn_ref, ctx_ref,
                        wo_ref, g_ref, b_ref, y_ref, o_scr, s_scr, p_scr, stat_scr, bias_scr, *,
                        alpha, attn_layer):
    i = pl.program_id(1)
    n_tiles = pl.num_programs(1)
    m = mod_ref[...]
    nc = ctx_ref.shape[1] // BLOCK
    window = 4 + nc
    half_rows = 2 * BLOCK
    kv_all = jnp.concatenate([kvp_ref[...].astype(F32), kvm_ref[0:BLOCK, :].astype(F32), ctx_ref[0],
                              kvm_ref[BLOCK:, :].astype(F32), kvn_ref[...].astype(F32)], axis=0)
    k_all, v_all = kv_all[:, :KV_WIDTH], kv_all[:, KV_WIDTH:]
    qi = lax.broadcasted_iota(jnp.int32, (BLOCK, BLOCK), 0)
    ki = lax.broadcasted_iota(jnp.int32, (BLOCK, BLOCK), 1)
    bias_scr[0] = jnp.where(ki >= qi, jnp.where(i > 0, 0.0, NEG_INF), NEG_INF)
    bias_scr[1] = jnp.where(ki <= qi, 0.0, NEG_INF)
    bias_scr[2] = jnp.where(ki >= qi, 0.0, NEG_INF)
    bias_scr[3] = jnp.where(ki <= qi, jnp.where(i < n_tiles - 1, 0.0, NEG_INF), NEG_INF)
    prev_edge, next_plain, prev_plain, next_edge = (bias_scr.at[j] for j in range(4))
    ctx_a = list(range(2, 2 + nc))
    ctx_b = list(range(nc))
    plans = [
        [(0, BLOCK, [0, 1] + ctx_a + [2 + nc], {0: prev_edge, 2 + nc: next_plain}),
         (BLOCK, BLOCK, [1] + ctx_a + [2 + nc, 3 + nc], {1: prev_plain, 3 + nc: next_plain})],
        [(0, BLOCK, ctx_b + [nc, nc + 1, nc + 2], {nc: prev_plain, nc + 2: next_plain}),
         (BLOCK, BLOCK, ctx_b + [nc + 1, nc + 2, nc + 3], {nc + 1: prev_plain, nc + 3: next_edge})],
    ]
    unused = [[3 + nc, 0], [nc + 3, nc]]
    for half in range(2):
        for parity in range(2):
            for qb in range(2):
                blk = unused[half][qb]
                p_scr[2 * half + parity, qb * BLOCK:(qb + 1) * BLOCK,
                      2 * blk * LANES:2 * (blk + 1) * LANES] = jnp.zeros((BLOCK, 2 * LANES), BF16)
    kk = _cached(lambda kv_head: _head_pair_operand(k_all, kv_head))
    vv = _cached(lambda kv_head: _head_pair_operand(v_all, kv_head))

    def operand(full, kv_head, half):
        first = 2 * half * 2 * BLOCK
        return full(kv_head)[first:first + window * 2 * BLOCK]

    units = []
    for kv_head in range(N_KV_HEADS):
        for half in range(2):
            for t in range(kv_head * (GROUP // 2), (kv_head + 1) * (GROUP // 2)):
                n = len(units)
                rows = slice(half * half_rows, (half + 1) * half_rows)
                units.append(dict(
                    q=functools.partial(lambda rows, t: q_ref[rows, LANES * t:LANES * (t + 1)], rows, t),
                    kk=functools.partial(operand, kk, kv_head, half),
                    vv=functools.partial(operand, vv, kv_head, half),
                    t=t, rows=rows, s_ref=s_scr.at[n % 2], p_ref=p_scr.at[2 * half + n % 2],
                    plan=plans[half]))
    _attention_pipeline(units, sink_ref, attn_layer, o_scr, stat_scr)
    out = _bdot(o_scr[...], wo_ref[...])
    z = alpha * x_ref[...] + m[2:3] * out
    y_ref[...] = _layer_norm(z, g_ref[...], b_ref[...])


def _sample_attn_layer(x, mods, cond0, seq, q, kv, ctx_kv, sink, layer, attn_layer, wo, g, b, alpha):
    m_rows = x.shape[0]
    bsz = m_rows // seq
    tq = 4 * BLOCK
    tiles = seq // tq
    blocks = seq // BLOCK
    n_ctx = ctx_kv.shape[1]
    assert n_ctx % BLOCK == 0 and seq % tq == 0
    window = 4 + n_ctx // BLOCK
    kern = functools.partial(_sample_attn_kernel, alpha=alpha, attn_layer=attn_layer)
    row_tile = lambda bi, i: (bi * tiles + i, 0)
    prev_blk = lambda bi, i: (bi * blocks + jnp.maximum(4 * i - 1, 0), 0)
    next_blk = lambda bi, i: (bi * blocks + jnp.minimum(4 * i + 4, blocks - 1), 0)
    ctx_blk = lambda bi, i: (bi, 0, 0)
    half_rows = 2 * BLOCK
    width = 2 * window * LANES
    return pl.pallas_call(
        kern,
        grid=(bsz, tiles),
        in_specs=[
            pl.BlockSpec(memory_space=pltpu.SMEM),
            pl.BlockSpec((tq, D_MODEL), row_tile),
            _mod_spec(layer, lambda bi, i: cond0 + bi),
            pl.BlockSpec((tq, D_MODEL), row_tile),
            pl.BlockSpec((BLOCK, 2 * KV_WIDTH), prev_blk),
            pl.BlockSpec((tq, 2 * KV_WIDTH), row_tile),
            pl.BlockSpec((BLOCK, 2 * KV_WIDTH), next_blk),
            pl.BlockSpec((1, n_ctx, 2 * KV_WIDTH), ctx_blk),
            _layer_spec(wo.shape, attn_layer),
            _layer_spec(g.shape, layer),
            _layer_spec(b.shape, layer),
        ],
        out_specs=pl.BlockSpec((tq, D_MODEL), row_tile),
        out_shape=jax.ShapeDtypeStruct((m_rows, D_MODEL), F32),
        scratch_shapes=[pltpu.VMEM((tq, D_MODEL), BF16),
                        pltpu.VMEM((2, half_rows, width), F32),
                        pltpu.VMEM((4, half_rows, width), BF16),
                        pltpu.VMEM((2, 2, half_rows, 2 * LANES), F32),
                        pltpu.VMEM((4, BLOCK, BLOCK), F32)],
        compiler_params=_params(2),
        name="sample_attn",
    )(sink, x, mods, q, kv, kv, kv, ctx_kv, wo, g, b)


def _glu_weight_kernel(value_ref, gate_ref, out_ref):
    out_ref[:, :LANES] = value_ref[...].astype(BF16)
    out_ref[:, LANES:] = gate_ref[...].astype(BF16)


def _glu_weight_tiles(pw1):
    n_layers = pw1.shape[0]
    lane_tiles = D_MODEL // LANES
    return pl.pallas_call(
        _glu_weight_kernel,
        grid=(n_layers, lane_tiles),
        in_specs=[
            pl.BlockSpec((None, D_MODEL, LANES), lambda l, c: (l, 0, c)),
            pl.BlockSpec((None, D_MODEL, LANES), lambda l, c: (l, 0, lane_tiles + c)),
        ],
        out_specs=pl.BlockSpec((None, None, D_MODEL, 2 * LANES), lambda l, c: (l, c, 0, 0)),
        out_shape=jax.ShapeDtypeStruct((n_layers, lane_tiles, D_MODEL, 2 * LANES), BF16),
        compiler_params=_params(2),
        name="glu_weight_tiles",
    )(pw1, pw1)


def _conv_kernel(xp_ref, x_ref, xn_ref, mod_ref, pw1_ref, pb1_ref, dw_ref, db_ref,
                 cg_ref, cb_ref, pw2_ref, pb2_ref, g_ref, b_ref, y_ref, *scratch,
                 alpha, tm, sub, tiles_per_seq, seq_per_sub):
    i = pl.program_id(0)
    t = i % tiles_per_seq
    m = mod_ref[...]
    lane_tiles = D_MODEL // LANES
    n_sub = tm // sub
    h_scrs, u_scrs, c_scrs = scratch[:n_sub], scratch[n_sub:2 * n_sub], scratch[2 * n_sub:]
    recompute_halo = not seq_per_sub

    def modulated_input(s):
        r0 = s * sub
        rows = x_ref[r0:r0 + sub, :]
        if recompute_halo:
            top = xp_ref[...] if s == 0 else x_ref[r0 - HALO:r0, :]
            bot = xn_ref[...] if s == n_sub - 1 else x_ref[r0 + sub:r0 + sub + HALO, :]
            rows = jnp.concatenate([top, rows, bot], axis=0)
        h_scrs[s][...] = (rows * (1.0 + m[1:2]) + m[0:1]).astype(BF16)

    def glu_chunk(s, c):
        a = _bdot(h_scrs[s][...], pw1_ref[c])
        value = a[:, :LANES] + pb1_ref[:, c * LANES:(c + 1) * LANES]
        gate = a[:, LANES:] + pb1_ref[:, D_MODEL + c * LANES:D_MODEL + (c + 1) * LANES]
        u = value * jax.nn.sigmoid(gate)
        zeros = jnp.zeros((HALO, LANES), F32)
        if recompute_halo:
            u_scrs[s][c, :, :] = u
            if s == 0:
                u_scrs[s][c, 0:HALO, :] = jnp.where(t > 0, u[:HALO], 0.0)
            if s == n_sub - 1:
                u_scrs[s][c, HALO + sub:, :] = jnp.where(t < tiles_per_seq - 1, u[HALO + sub:], 0.0)
        else:
            u_scrs[s][c, 0:HALO, :] = zeros
            u_scrs[s][c, HALO:HALO + sub, :] = u
            u_scrs[s][c, HALO + sub:, :] = zeros

    def conv_taps(s, c):
        span_rows = SUBLANES * CONV_ROW_STRIDE
        for base in range(0, sub, CONV_ROW_BLOCK):
            starts = [base + sp + j for sp in range(0, CONV_ROW_BLOCK, span_rows)
                      for j in range(CONV_ROW_STRIDE)]
            bias = db_ref[:, c * LANES:(c + 1) * LANES]
            accs = [jnp.zeros((SUBLANES, LANES), F32) + bias for _ in starts]
            for tap in range(CONV_WIDTH):
                w = dw_ref[tap:tap + 1, c * LANES:(c + 1) * LANES]
                off = HALO - CONV_PAD + tap
                for n, start in enumerate(starts):
                    rows = pl.ds(start + off, SUBLANES, stride=CONV_ROW_STRIDE)
                    accs[n] = accs[n] + w * u_scrs[s][c, rows, :]
            for n, start in enumerate(starts):
                c_scrs[s][c, pl.ds(start, SUBLANES, stride=CONV_ROW_STRIDE), :] = accs[n]

    def finish(s):
        r0 = s * sub
        x = x_ref[r0:r0 + sub, :]
        conv = jnp.concatenate([c_scrs[s][c] for c in range(lane_tiles)], axis=1)
        un = _layer_norm(conv, cg_ref[...], cb_ref[...])
        un = un * jax.nn.sigmoid(un)
        out = _bdot(un.astype(BF16), pw2_ref[...]) + pb2_ref[...]
        z = alpha * x + m[2:3] * out
        y_ref[r0:r0 + sub, :] = _layer_norm(z, g_ref[...], b_ref[...])

    for s in range(n_sub):
        modulated_input(s)
    for c in range(lane_tiles):
        glu_chunk(0, c)
    for s in range(n_sub):
        for c in range(lane_tiles):
            conv_taps(s, c)
            if s + 1 < n_sub:
                glu_chunk(s + 1, c)
        finish(s)


def _conv_layer(x, mods, cond0, rows_per_group, seq, layer, conv_layer, pw1, pb1, dw, db, cg, cb, pw2, pb2,
                g, b, alpha):
    m_rows = x.shape[0]
    tm = CONV_TILE_ROWS
    sub = CONV_SUB_ROWS
    seq_per_sub = seq == sub
    assert seq_per_sub or seq % tm == 0
    tiles_per_seq = max(seq // tm, 1)
    in_rows = sub if seq_per_sub else sub + 2 * HALO
    halo_per_tile = tm // HALO
    n_halo_blocks = m_rows // HALO
    kern = functools.partial(_conv_kernel, alpha=alpha, tm=tm, sub=sub, tiles_per_seq=tiles_per_seq,
                             seq_per_sub=seq_per_sub)
    prev_blk = lambda i: (jnp.maximum(i * halo_per_tile - 1, 0), 0)
    next_blk = lambda i: (jnp.minimum((i + 1) * halo_per_tile, n_halo_blocks - 1), 0)
    return pl.pallas_call(
        kern,
        grid=(m_rows // tm,),
        in_specs=[
            pl.BlockSpec((HALO, D_MODEL), prev_blk),
            pl.BlockSpec((tm, D_MODEL), lambda i: (i, 0)),
            pl.BlockSpec((HALO, D_MODEL), next_blk),
            _mod_spec(layer, lambda i: cond0 + i * tm // rows_per_group),
            _layer_spec(pw1.shape, conv_layer),
            _layer_spec(pb1.shape, conv_layer),
            _layer_spec(dw.shape, conv_layer),
            _layer_spec(db.shape, conv_layer),
            _layer_spec(cg.shape, conv_layer),
            _layer_spec(cb.shape, conv_layer),
            _layer_spec(pw2.shape, conv_layer),
            _layer_spec(pb2.shape, conv_layer),
            _layer_spec(g.shape, layer),
            _layer_spec(b.shape, layer),
        ],
        out_specs=pl.BlockSpec((tm, D_MODEL), lambda i: (i, 0)),
        out_shape=jax.ShapeDtypeStruct((m_rows, D_MODEL), F32),
        scratch_shapes=([pltpu.VMEM((in_rows, D_MODEL), BF16)] * (tm // sub)
                        + [pltpu.VMEM((D_MODEL // LANES, sub + 2 * HALO, LANES), F32)] * (tm // sub)
                        + [pltpu.VMEM((D_MODEL // LANES, sub, LANES), F32)] * (tm // sub)),
        compiler_params=_params(1),
        name="conv",
    )(x, x, x, mods, pw1, pb1, dw, db, cg, cb, pw2, pb2, g, b)


def kernel(x_prompt, x_sample, cache_k, cache_v, c, c_ctx, ada_w, ada_b, attn_w_qkv, attn_w_o, attn_sink, conv_pw1_w, conv_pw1_b, conv_dw_w, conv_dw_b, conv_norm_g, conv_norm_b, conv_pw2_w, conv_pw2_b, ln1_g, ln1_b, mlp_w1, mlp_b1, mlp_w2, mlp_b2, ln2_g, ln2_b):
    depth = ada_w.shape[0]
    bsz, seq, _ = x_prompt.shape
    dec_bsz, dec_seq, _ = x_sample.shape
    n_ctx = cache_k.shape[2]
    alpha = (2.0 * depth) ** 0.25
    assert 1 + dec_bsz <= COND_ROWS

    rows = lambda a: a.reshape(a.shape[0], 1, a.shape[-1])

    cond = jnp.concatenate([c_ctx[None], c, jnp.zeros((COND_ROWS - 1 - dec_bsz, D_MODEL), F32)], axis=0)
    mods = _modulation(cond, 1 + dec_bsz, ada_w, ada_b).reshape(depth, COND_ROWS, N_MOD, D_MODEL)
    cond_p, cond_s = 0, 1

    cos, sin = _rope_tables(dec_seq)

    wqkv, wo = attn_w_qkv.astype(BF16), attn_w_o.astype(BF16)
    pw1, pw2 = _glu_weight_tiles(conv_pw1_w), conv_pw2_w.astype(BF16)
    w1, w2 = mlp_w1.astype(BF16), mlp_w2.astype(BF16)
    ln1 = (rows(ln1_g), rows(ln1_b))
    conv_w = (pw1, rows(conv_pw1_b), conv_dw_w, rows(conv_dw_b), rows(conv_norm_g), rows(conv_norm_b),
              pw2, rows(conv_pw2_b))
    mlp_w = (w1, rows(mlp_b1), w2, rows(mlp_b2), rows(ln2_g), rows(ln2_b))

    yp = x_prompt
    ys = x_sample.reshape(dec_bsz * dec_seq, D_MODEL)
    new_k, new_v = [], []
    for i in range(depth):
        j = i // N_MIXERS
        if i % N_MIXERS == 0:
            yp, kp, vp = _prompt_attn_layer(yp, mods, cond_p, attn_sink, i, j, wqkv, wo, *ln1, alpha)
            new_k.append(kp.reshape(bsz, seq, N_KV_HEADS, HEAD_DIM))
            new_v.append(vp.reshape(bsz, seq, N_KV_HEADS, HEAD_DIM))
            q, kv = _sample_qkv(ys, mods, cond_s, dec_seq, cos, sin, i, j, wqkv)
            ctx_kv = jnp.concatenate([cache_k[:, j].reshape(dec_bsz, n_ctx, KV_WIDTH),
                                      cache_v[:, j].reshape(dec_bsz, n_ctx, KV_WIDTH)], axis=-1)
            ys = _sample_attn_layer(ys, mods, cond_s, dec_seq, q, kv, ctx_kv, attn_sink, i, j, wo, *ln1, alpha)
        else:
            yp = _conv_layer(yp.reshape(bsz * seq, D_MODEL), mods, cond_p, bsz * seq, seq, i, j, *conv_w, *ln1,
                             alpha).reshape(bsz, seq, D_MODEL)
            ys = _conv_layer(ys, mods, cond_s, dec_seq, dec_seq, i, j, *conv_w, *ln1, alpha)
        yp = _mlp(yp.reshape(bsz * seq, D_MODEL), mods, cond_p, bsz * seq, i, *mlp_w, alpha).reshape(bsz, seq, D_MODEL)
        ys = _mlp(ys, mods, cond_s, dec_seq, i, *mlp_w, alpha)
    new_cache_k = jnp.stack(new_k, axis=1)
    new_cache_v = jnp.stack(new_v, axis=1)
    return (yp, ys.reshape(dec_bsz, dec_seq, D_MODEL), new_cache_k, new_cache_v)
```

```python
import functools
import math

import jax
import jax.numpy as jnp
import numpy as np
from jax import lax
from jax.experimental import pallas as pl
from jax.experimental.pallas import tpu as pltpu

D_MODEL = 1024
GRID_W = 64
N_MIXERS = 2
HEAD_DIM = 64
N_HEADS = D_MODEL // HEAD_DIM
N_KV_HEADS = 4
GROUP = N_HEADS // N_KV_HEADS
KV_WIDTH = N_KV_HEADS * HEAD_DIM
QK_WIDTH = D_MODEL + KV_WIDTH
QKV_WIDTH = D_MODEL + 2 * KV_WIDTH
BLOCK = 128
ROPE_THETA = 10000.0
CONV_WIDTH = 31
CONV_PAD = (CONV_WIDTH - 1) // 2
D_FF = 4 * D_MODEL
N_MOD = 6
LN_EPS = 1e-5
NEG_INF = -1e30
ATTN_SCALE = HEAD_DIM ** -0.5
LOG2E = math.log2(math.e)
Q_SCALE = ATTN_SCALE * LOG2E

LANES = 128
SUBLANES = 8
VMEM_LIMIT_BYTES = 56 * 1024 * 1024

COND_ROWS = 8
HALO = 16
CONV_ROW_STRIDE = 4
CONV_ROW_BLOCK = 128
CONV_TILE_ROWS = 512
CONV_SUB_ROWS = 256
SOFTMAX_ROWS = 64
PROMPT_SEQS_PER_STEP = 4
MOD_TILE_COLS = 3072
MLP_TILE_ROWS = 512
MLP_SUB_ROWS = 256

BF16 = jnp.bfloat16
F32 = jnp.float32


def _layer_spec(shape, layer):
    n = len(shape)
    return pl.BlockSpec((None,) + tuple(shape[1:]), lambda *_: (layer,) + (0,) * (n - 1),
                        pipeline_mode=pl.Buffered(1))


def _whole_spec(shape):
    n = len(shape)
    return pl.BlockSpec(tuple(shape), lambda *_: (0,) * n, pipeline_mode=pl.Buffered(1))


def _mod_spec(layer, cond_of_step):
    return pl.BlockSpec((None, None, N_MOD, D_MODEL), lambda *ids: (layer, cond_of_step(*ids), 0, 0))


def _params(n_axes):
    return pltpu.CompilerParams(dimension_semantics=("arbitrary",) * n_axes,
                                vmem_limit_bytes=VMEM_LIMIT_BYTES)


def _layer_norm(z, g, b):
    mu = jnp.mean(z, axis=-1, keepdims=True)
    zc = z - mu
    var = jnp.mean(zc * zc, axis=-1, keepdims=True)
    return zc * lax.rsqrt(var + LN_EPS) * g + b


def _bdot(a, b):
    return jnp.dot(a, b, preferred_element_type=F32)


def _bdot_nt(a, b):
    return lax.dot_general(a, b, (((1,), (1,)), ((), ())), preferred_element_type=F32)


def _mod_kernel(cond_ref, w_ref, b_ref, out_ref, s_scr, *, n_cond):
    @pl.when((pl.program_id(0) == 0) & (pl.program_id(1) == 0))
    def _():
        c = cond_ref[...]
        s_t = (c * jax.nn.sigmoid(c)).T
        for m in range(n_cond):
            s_scr[m] = jnp.broadcast_to(s_t[:, m:m + 1], (D_MODEL, LANES))

    zero_rows = [jnp.zeros((1, LANES), F32)] * (COND_ROWS - n_cond)
    for n in range(w_ref.shape[2] // LANES):
        cols = slice(n * LANES, (n + 1) * LANES)
        accs = [jnp.zeros((SUBLANES, LANES), F32) for _ in range(n_cond)]
        for k in range(0, D_MODEL, SUBLANES):
            w = w_ref[0, k:k + SUBLANES, cols]
            for m in range(n_cond):
                accs[m] = accs[m] + w * s_scr[m, k:k + SUBLANES, :]
        out_rows = [jnp.sum(a, axis=0, keepdims=True) for a in accs] + zero_rows
        out_ref[0, :, cols] = jnp.concatenate(out_rows, axis=0) + b_ref[0, :, cols]


def _modulation(cond, n_cond, ada_w, ada_b):
    depth = ada_w.shape[0]
    tn = MOD_TILE_COLS
    width = N_MOD * D_MODEL
    return pl.pallas_call(
        functools.partial(_mod_kernel, n_cond=n_cond),
        grid=(depth, width // tn),
        in_specs=[
            pl.BlockSpec((COND_ROWS, D_MODEL), lambda l, n: (0, 0)),
            pl.BlockSpec((1, D_MODEL, tn), lambda l, n: (l, 0, n)),
            pl.BlockSpec((1, 1, tn), lambda l, n: (l, 0, n)),
        ],
        out_specs=pl.BlockSpec((1, COND_ROWS, tn), lambda l, n: (l, 0, n)),
        out_shape=jax.ShapeDtypeStruct((depth, COND_ROWS, width), F32),
        scratch_shapes=[pltpu.VMEM((n_cond, D_MODEL, LANES), F32)],
        compiler_params=_params(2),
        name="modulation",
    )(cond, ada_w, ada_b.reshape(depth, 1, width))


def _mlp_kernel(y_ref, mod_ref, w1_ref, b1_ref, w2_ref, b2_ref, g_ref, b_ref, out_ref, *,
                alpha, ff_chunk, layer):
    m = mod_ref[...]
    for r0 in range(0, y_ref.shape[0], MLP_SUB_ROWS):
        rows = slice(r0, r0 + MLP_SUB_ROWS)
        y = y_ref[rows, :]
        h = (y * (1.0 + m[4:5]) + m[3:4]).astype(BF16)
        acc = jnp.zeros(y.shape, F32)
        for c in range(D_FF // ff_chunk):
            cols = slice(c * ff_chunk, (c + 1) * ff_chunk)
            a = _bdot(h, w1_ref[:, cols]) + b1_ref[layer:layer + 1, cols]
            a = jnp.maximum(a, 0.0)
            acc = acc + _bdot((a * a).astype(BF16), w2_ref[cols, :])
        z = alpha * y + m[5:6] * (acc + b2_ref[layer:layer + 1, :])
        out_ref[rows, :] = _layer_norm(z, g_ref[layer:layer + 1, :], b_ref[layer:layer + 1, :])


def _mlp(y, mods, cond0, rows_per_group, layer, w1, b1, w2, b2, g, b, alpha):
    m_rows = y.shape[0]
    tm = MLP_TILE_ROWS
    kern = functools.partial(_mlp_kernel, alpha=alpha, ff_chunk=1024, layer=layer)
    return pl.pallas_call(
        kern,
        grid=(m_rows // tm,),
        in_specs=[
            pl.BlockSpec((tm, D_MODEL), lambda i: (i, 0)),
            _mod_spec(layer, lambda i: cond0 + i * tm // rows_per_group),
            _layer_spec(w1.shape, layer),
            _whole_spec(b1.shape),
            _layer_spec(w2.shape, layer),
            _whole_spec(b2.shape),
            _whole_spec(g.shape),
            _whole_spec(b.shape),
        ],
        out_specs=pl.BlockSpec((tm, D_MODEL), lambda i: (i, 0)),
        out_shape=jax.ShapeDtypeStruct((m_rows, D_MODEL), F32),
        compiler_params=_params(1),
        name="mlp",
    )(y, mods, w1, b1, w2, b2, g, b)


def _head_pair_operand(x, kv_head):
    tile = x[:, LANES * (kv_head // 2):LANES * (kv_head // 2 + 1)]
    swapped = pltpu.roll(tile, HEAD_DIM, axis=1)
    lane = lax.broadcasted_iota(jnp.int32, (1, LANES), 1)
    low = lane < HEAD_DIM
    if kv_head % 2 == 0:
        lo_src, hi_src = tile, swapped
    else:
        lo_src, hi_src = swapped, tile
    lo = jnp.where(low, lo_src, 0.0).astype(BF16)
    hi = jnp.where(low, 0.0, hi_src).astype(BF16)
    pieces = []
    for blk in range(x.shape[0] // BLOCK):
        pieces += [lo[blk * BLOCK:(blk + 1) * BLOCK], hi[blk * BLOCK:(blk + 1) * BLOCK]]
    return jnp.concatenate(pieces, axis=0)


def _pair_softmax(s_ref, p_ref, stat_ref, sinks, row_plan):
    def load(rows, block, head, plan_row, biases):
        tile = 2 * block + head
        v = s_ref[rows, tile * LANES:(tile + 1) * LANES]
        if block in biases:
            v = v + biases[block][plan_row:plan_row + SOFTMAX_ROWS, :]
        return v

    for row0, n_rows, blocks, biases in row_plan:
        for r in range(0, n_rows, SOFTMAX_ROWS):
            rows = slice(row0 + r, row0 + r + SOFTMAX_ROWS)
            for head in range(2):
                acc = None
                for block in blocks:
                    v = load(rows, block, head, r, biases)
                    acc = v if acc is None else jnp.maximum(acc, v)
                stat_ref[0, rows, head * LANES:(head + 1) * LANES] = acc
    mx = [jnp.maximum(jnp.max(stat_ref[0, :, head * LANES:(head + 1) * LANES], axis=-1, keepdims=True),
                      sinks[head]) for head in range(2)]
    for row0, n_rows, blocks, biases in row_plan:
        for r in range(0, n_rows, SOFTMAX_ROWS):
            rows = slice(row0 + r, row0 + r + SOFTMAX_ROWS)
            for head in range(2):
                m_rows = mx[head][row0 + r:row0 + r + SOFTMAX_ROWS]
                acc = None
                for block in blocks:
                    p = jnp.exp2(load(rows, block, head, r, biases) - m_rows)
                    acc = p if acc is None else acc + p
                    tile = 2 * block + head
                    p_ref[rows, tile * LANES:(tile + 1) * LANES] = p.astype(BF16)
                stat_ref[1, rows, head * LANES:(head + 1) * LANES] = acc
    recip = [1.0 / (jnp.sum(stat_ref[1, :, head * LANES:(head + 1) * LANES], axis=-1, keepdims=True)
                    + jnp.exp2(sinks[head] - mx[head])) for head in range(2)]
    lane = lax.broadcasted_iota(jnp.int32, (1, LANES), 1)
    return jnp.where(lane < HEAD_DIM, recip[0], recip[1])


def _attention_pipeline(units, sink_ref, attn_layer, o_scr, stat_scr):
    def scores(u):
        u["s_ref"][...] = _bdot_nt(u["q"](), u["kk"]())

    def values(u, recip):
        o = _bdot(u["p_ref"][...], u["vv"]()) * recip
        o_scr[u["rows"], LANES * u["t"]:LANES * (u["t"] + 1)] = o.astype(BF16)

    scores(units[0])
    recip_prev = None
    for n, u in enumerate(units):
        if n + 1 < len(units):
            scores(units[n + 1])
        if n > 0:
            values(units[n - 1], recip_prev)
        sinks = (sink_ref[attn_layer, 2 * u["t"]] * LOG2E, sink_ref[attn_layer, 2 * u["t"] + 1] * LOG2E)
        recip_prev = _pair_softmax(u["s_ref"], u["p_ref"], stat_scr.at[n % 2], sinks, u["plan"])
    values(units[-1], recip_prev)


def _cached(fn):
    memo = {}

    def get(*key):
        if key not in memo:
            memo[key] = fn(*key)
        return memo[key]
    return get


def _prompt_attn_kernel(sink_ref, x_ref, mod_ref, wqkv_ref, wo_ref, g_ref, b_ref,
                        y_ref, k_ref, v_ref, o_scr, s_scr, p_scr, stat_scr, *, alpha, layer, attn_layer):
    n_seq, seq, _ = x_ref.shape
    m = mod_ref[...]
    x = x_ref[...].reshape(n_seq * seq, D_MODEL)
    h = (x * (1.0 + m[1:2]) + m[0:1]).astype(BF16)
    qkv = _bdot(h, wqkv_ref[...])
    q = (qkv[:, :D_MODEL] * Q_SCALE).astype(BF16)
    plan = [(0, seq, list(range(seq // BLOCK)), {})]
    units = []
    for sq in range(n_seq):
        rows = slice(sq * seq, (sq + 1) * seq)
        k = qkv[rows, D_MODEL:QK_WIDTH]
        v = qkv[rows, QK_WIDTH:]
        k_ref[sq] = k
        v_ref[sq] = v
        kk = _cached(functools.partial(_head_pair_operand, k))
        vv = _cached(functools.partial(_head_pair_operand, v))
        for t in range(D_MODEL // LANES):
            n = len(units)
            units.append(dict(q=functools.partial(lambda rows, t: q[rows, LANES * t:LANES * (t + 1)], rows, t),
                              kk=functools.partial(kk, t // (GROUP // 2)),
                              vv=functools.partial(vv, t // (GROUP // 2)),
                              t=t, rows=rows, s_ref=s_scr.at[n % 2], p_ref=p_scr.at[n % 2], plan=plan))
    _attention_pipeline(units, sink_ref, attn_layer, o_scr, stat_scr)
    out = _bdot(o_scr[...], wo_ref[...])
    z = alpha * x + m[2:3] * out
    y_ref[...] = _layer_norm(z, g_ref[layer:layer + 1, :], b_ref[layer:layer + 1, :]).reshape(n_seq, seq, D_MODEL)


def _prompt_attn_layer(x, mods, cond0, sink, layer, attn_layer, wqkv, wo, g, b, alpha):
    bsz, seq, _ = x.shape
    n_seq = PROMPT_SEQS_PER_STEP
    assert bsz % n_seq == 0
    width = 2 * (seq // BLOCK) * LANES
    kern = functools.partial(_prompt_attn_kernel, alpha=alpha, layer=layer, attn_layer=attn_layer)
    return pl.pallas_call(
        kern,
        grid=(bsz // n_seq,),
        in_specs=[
            pl.BlockSpec(memory_space=pltpu.SMEM),
            pl.BlockSpec((n_seq, seq, D_MODEL), lambda i: (i, 0, 0)),
            _mod_spec(layer, lambda i: cond0),
            _layer_spec(wqkv.shape, attn_layer),
            _layer_spec(wo.shape, attn_layer),
            _whole_spec(g.shape),
            _whole_spec(b.shape),
        ],
        out_specs=[
            pl.BlockSpec((n_seq, seq, D_MODEL), lambda i: (i, 0, 0)),
            pl.BlockSpec((n_seq, seq, KV_WIDTH), lambda i: (i, 0, 0)),
            pl.BlockSpec((n_seq, seq, KV_WIDTH), lambda i: (i, 0, 0)),
        ],
        out_shape=[
            jax.ShapeDtypeStruct((bsz, seq, D_MODEL), F32),
            jax.ShapeDtypeStruct((bsz, seq, KV_WIDTH), F32),
            jax.ShapeDtypeStruct((bsz, seq, KV_WIDTH), F32),
        ],
        scratch_shapes=[pltpu.VMEM((n_seq * seq, D_MODEL), BF16),
                        pltpu.VMEM((2, seq, width), F32),
                        pltpu.VMEM((2, seq, width), BF16),
                        pltpu.VMEM((2, 2, seq, 2 * LANES), F32)],
        compiler_params=_params(1),
        name="prompt_attn",
    )(sink, x, mods, wqkv, wo, g, b)


def _rope_tables(n_tokens):
    rows = n_tokens // GRID_W
    row = np.repeat(np.arange(rows, dtype=np.float32), GRID_W)
    col = np.tile(np.arange(GRID_W, dtype=np.float32), rows)
    half = HEAD_DIM // 2
    inv = np.float32(ROPE_THETA) ** (-np.arange(0, half, 2, dtype=np.float32) / np.float32(half))
    ang_r = row[:, None] * inv
    ang_c = col[:, None] * inv
    cos = np.concatenate([np.cos(ang_r)] * 2 + [np.cos(ang_c)] * 2, axis=-1)
    sin = np.concatenate([-np.sin(ang_r), np.sin(ang_r), -np.sin(ang_c), np.sin(ang_c)], axis=-1)
    reps = (1, LANES // HEAD_DIM)
    return jnp.asarray(np.tile(cos, reps), F32), jnp.asarray(np.tile(sin, reps), F32)


def _sample_qkv_kernel(x_ref, mod_ref, cos_ref, sin_ref, wqkv_ref, q_ref, kv_ref):
    x = x_ref[...]
    m = mod_ref[...]
    h = (x * (1.0 + m[1:2]) + m[0:1]).astype(BF16)
    qkv = _bdot(h, wqkv_ref[...])
    cos = cos_ref[...]
    sin = sin_ref[...]
    lane = lax.broadcasted_iota(jnp.int32, (1, LANES), 1)
    first = (lane % (HEAD_DIM // 2)) < (HEAD_DIM // 4)
    quarter = HEAD_DIM // 4
    for t in range(QK_WIDTH // LANES):
        tile = qkv[:, LANES * t:LANES * (t + 1)]
        partner = jnp.where(first, pltpu.roll(tile, LANES - quarter, axis=1),
                            pltpu.roll(tile, quarter, axis=1))
        rot = tile * cos + partner * sin
        if t < D_MODEL // LANES:
            q_ref[:, LANES * t:LANES * (t + 1)] = (rot * Q_SCALE).astype(BF16)
        else:
            tk = t - D_MODEL // LANES
            kv_ref[:, LANES * tk:LANES * (tk + 1)] = rot.astype(BF16)
    kv_ref[:, KV_WIDTH:] = qkv[:, QK_WIDTH:].astype(BF16)


def _sample_qkv(x, mods, cond0, seq, cos, sin, layer, attn_layer, wqkv):
    m_rows = x.shape[0]
    tm = 512
    tiles_per_seq = seq // tm
    return pl.pallas_call(
        _sample_qkv_kernel,
        grid=(m_rows // tm,),
        in_specs=[
            pl.BlockSpec((tm, D_MODEL), lambda i: (i, 0)),
            _mod_spec(layer, lambda i: cond0 + i // tiles_per_seq),
            pl.BlockSpec((tm, LANES), lambda i: (i % tiles_per_seq, 0)),
            pl.BlockSpec((tm, LANES), lambda i: (i % tiles_per_seq, 0)),
            _layer_spec(wqkv.shape, attn_layer),
        ],
        out_specs=[
            pl.BlockSpec((tm, D_MODEL), lambda i: (i, 0)),
            pl.BlockSpec((tm, 2 * KV_WIDTH), lambda i: (i, 0)),
        ],
        out_shape=[
            jax.ShapeDtypeStruct((m_rows, D_MODEL), BF16),
            jax.ShapeDtypeStruct((m_rows, 2 * KV_WIDTH), BF16),
        ],
        compiler_params=_params(1),
        name="sample_qkv",
    )(x, mods, cos, sin, wqkv)


def _sample_attn_kernel(sink_ref, x_ref, mod_ref, q_ref, kvp_ref, kvm_ref, kvn_ref, ctx_ref,
                        wo_ref, g_ref, b_ref, y_ref, o_scr, s_scr, p_scr, stat_scr, bias_scr, *,
                        alpha, layer, attn_layer):
    i = pl.program_id(1)
    n_tiles = pl.num_programs(1)
    m = mod_ref[...]
    nc = ctx_ref.shape[1] // BLOCK
    window = 4 + nc
    half_rows = 2 * BLOCK
    kv_all = jnp.concatenate([kvp_ref[...].astype(F32), kvm_ref[0:BLOCK, :].astype(F32), ctx_ref[0],
                              kvm_ref[BLOCK:, :].astype(F32), kvn_ref[...].astype(F32)], axis=0)
    k_all, v_all = kv_all[:, :KV_WIDTH], kv_all[:, KV_WIDTH:]
    qi = lax.broadcasted_iota(jnp.int32, (BLOCK, BLOCK), 0)
    ki = lax.broadcasted_iota(jnp.int32, (BLOCK, BLOCK), 1)
    bias_scr[0] = jnp.where(ki >= qi, jnp.where(i > 0, 0.0, NEG_INF), NEG_INF)
    bias_scr[1] = jnp.where(ki <= qi, 0.0, NEG_INF)
    bias_scr[2] = jnp.where(ki >= qi, 0.0, NEG_INF)
    bias_scr[3] = jnp.where(ki <= qi, jnp.where(i < n_tiles - 1, 0.0, NEG_INF), NEG_INF)
    prev_edge, next_plain, prev_plain, next_edge = (bias_scr.at[j] for j in range(4))
    ctx_a = list(range(2, 2 + nc))
    ctx_b = list(range(nc))
    plans = [
        [(0, BLOCK, [0, 1] + ctx_a + [2 + nc], {0: prev_edge, 2 + nc: next_plain}),
         (BLOCK, BLOCK, [1] + ctx_a + [2 + nc, 3 + nc], {1: prev_plain, 3 + nc: next_plain})],
        [(0, BLOCK, ctx_b + [nc, nc + 1, nc + 2], {nc: prev_plain, nc + 2: next_plain}),
         (BLOCK, BLOCK, ctx_b + [nc + 1, nc + 2, nc + 3], {nc + 1: prev_plain, nc + 3: next_edge})],
    ]
    unused = [[3 + nc, 0], [nc + 3, nc]]
    for half in range(2):
        for parity in range(2):
            for qb in range(2):
                blk = unused[half][qb]
                p_scr[2 * half + parity, qb * BLOCK:(qb + 1) * BLOCK,
                      2 * blk * LANES:2 * (blk + 1) * LANES] = jnp.zeros((BLOCK, 2 * LANES), BF16)
    kk = _cached(lambda kv_head: _head_pair_operand(k_all, kv_head))
    vv = _cached(lambda kv_head: _head_pair_operand(v_all, kv_head))

    def operand(full, kv_head, half):
        first = 2 * half * 2 * BLOCK
        return full(kv_head)[first:first + window * 2 * BLOCK]

    units = []
    for kv_head in range(N_KV_HEADS):
        for half in range(2):
            for t in range(kv_head * (GROUP // 2), (kv_head + 1) * (GROUP // 2)):
                n = len(units)
                rows = slice(half * half_rows, (half + 1) * half_rows)
                units.append(dict(
                    q=functools.partial(lambda rows, t: q_ref[rows, LANES * t:LANES * (t + 1)], rows, t),
                    kk=functools.partial(operand, kk, kv_head, half),
                    vv=functools.partial(operand, vv, kv_head, half),
                    t=t, rows=rows, s_ref=s_scr.at[n % 2], p_ref=p_scr.at[2 * half + n % 2],
                    plan=plans[half]))
    _attention_pipeline(units, sink_ref, attn_layer, o_scr, stat_scr)
    out = _bdot(o_scr[...], wo_ref[...])
    z = alpha * x_ref[...] + m[2:3] * out
    y_ref[...] = _layer_norm(z, g_ref[layer:layer + 1, :], b_ref[layer:layer + 1, :])


def _sample_attn_layer(x, mods, cond0, seq, q, kv, ctx_kv, sink, layer, attn_layer, wo, g, b, alpha):
    m_rows = x.shape[0]
    bsz = m_rows // seq
    tq = 4 * BLOCK
    tiles = seq // tq
    blocks = seq // BLOCK
    n_ctx = ctx_kv.shape[1]
    assert n_ctx % BLOCK == 0 and seq % tq == 0
    window = 4 + n_ctx // BLOCK
    kern = functools.partial(_sample_attn_kernel, alpha=alpha, layer=layer, attn_layer=attn_layer)
    row_tile = lambda bi, i: (bi * tiles + i, 0)
    prev_blk = lambda bi, i: (bi * blocks + jnp.maximum(4 * i - 1, 0), 0)
    next_blk = lambda bi, i: (bi * blocks + jnp.minimum(4 * i + 4, blocks - 1), 0)
    ctx_blk = lambda bi, i: (bi, 0, 0)
    half_rows = 2 * BLOCK
    width = 2 * window * LANES
    return pl.pallas_call(
        kern,
        grid=(bsz, tiles),
        in_specs=[
            pl.BlockSpec(memory_space=pltpu.SMEM),
            pl.BlockSpec((tq, D_MODEL), row_tile),
            _mod_spec(layer, lambda bi, i: cond0 + bi),
            pl.BlockSpec((tq, D_MODEL), row_tile),
            pl.BlockSpec((BLOCK, 2 * KV_WIDTH), prev_blk),
            pl.BlockSpec((tq, 2 * KV_WIDTH), row_tile),
            pl.BlockSpec((BLOCK, 2 * KV_WIDTH), next_blk),
            pl.BlockSpec((1, n_ctx, 2 * KV_WIDTH), ctx_blk),
            _layer_spec(wo.shape, attn_layer),
            _whole_spec(g.shape),
            _whole_spec(b.shape),
        ],
        out_specs=pl.BlockSpec((tq, D_MODEL), row_tile),
        out_shape=jax.ShapeDtypeStruct((m_rows, D_MODEL), F32),
        scratch_shapes=[pltpu.VMEM((tq, D_MODEL), BF16),
                        pltpu.VMEM((2, half_rows, width), F32),
                        pltpu.VMEM((4, half_rows, width), BF16),
                        pltpu.VMEM((2, 2, half_rows, 2 * LANES), F32),
                        pltpu.VMEM((4, BLOCK, BLOCK), F32)],
        compiler_params=_params(2),
        name="sample_attn",
    )(sink, x, mods, q, kv, kv, kv, ctx_kv, wo, g, b)


def _glu_weight_kernel(w_ref, out_ref):
    for c in range(D_MODEL // LANES):
        out_ref[c, :, :LANES] = w_ref[:, c * LANES:(c + 1) * LANES].astype(BF16)
        out_ref[c, :, LANES:] = w_ref[:, D_MODEL + c * LANES:D_MODEL + (c + 1) * LANES].astype(BF16)


def _glu_weight_tiles(pw1):
    n_layers = pw1.shape[0]
    lane_tiles = D_MODEL // LANES
    tk = D_MODEL // 2
    return pl.pallas_call(
        _glu_weight_kernel,
        grid=(n_layers, D_MODEL // tk),
        in_specs=[pl.BlockSpec((None, tk, 2 * D_MODEL), lambda l, r: (l, r, 0))],
        out_specs=pl.BlockSpec((None, lane_tiles, tk, 2 * LANES), lambda l, r: (l, 0, r, 0)),
        out_shape=jax.ShapeDtypeStruct((n_layers, lane_tiles, D_MODEL, 2 * LANES), BF16),
        compiler_params=_params(2),
        name="glu_weight_tiles",
    )(pw1)


def _conv_kernel(xp_ref, x_ref, xn_ref, mod_ref, pw1_ref, pb1_ref, dw_ref, db_ref,
                 cg_ref, cb_ref, pw2_ref, pb2_ref, g_ref, b_ref, y_ref, *scratch,
                 alpha, layer, conv_layer, tm, sub, tiles_per_seq, seq_per_sub):
    i = pl.program_id(0)
    t = i % tiles_per_seq
    m = mod_ref[...]
    lane_tiles = D_MODEL // LANES
    n_sub = tm // sub
    h_scrs, u_scrs, c_scrs = scratch[:n_sub], scratch[n_sub:2 * n_sub], scratch[2 * n_sub:]
    recompute_halo = not seq_per_sub
    conv_row = slice(conv_layer, conv_layer + 1)

    def modulated_input(s):
        r0 = s * sub
        rows = x_ref[r0:r0 + sub, :]
        if recompute_halo:
            top = xp_ref[...] if s == 0 else x_ref[r0 - HALO:r0, :]
            bot = xn_ref[...] if s == n_sub - 1 else x_ref[r0 + sub:r0 + sub + HALO, :]
            rows = jnp.concatenate([top, rows, bot], axis=0)
        h_scrs[s][...] = (rows * (1.0 + m[1:2]) + m[0:1]).astype(BF16)

    def glu_chunk(s, c):
        a = _bdot(h_scrs[s][...], pw1_ref[c])
        value = a[:, :LANES] + pb1_ref[conv_row, c * LANES:(c + 1) * LANES]
        gate = a[:, LANES:] + pb1_ref[conv_row, D_MODEL + c * LANES:D_MODEL + (c + 1) * LANES]
        u = value * jax.nn.sigmoid(gate)
        zeros = jnp.zeros((HALO, LANES), F32)
        if recompute_halo:
            u_scrs[s][c, :, :] = u
            if s == 0:
                u_scrs[s][c, 0:HALO, :] = jnp.where(t > 0, u[:HALO], 0.0)
            if s == n_sub - 1:
                u_scrs[s][c, HALO + sub:, :] = jnp.where(t < tiles_per_seq - 1, u[HALO + sub:], 0.0)
        else:
            u_scrs[s][c, 0:HALO, :] = zeros
            u_scrs[s][c, HALO:HALO + sub, :] = u
            u_scrs[s][c, HALO + sub:, :] = zeros

    def conv_taps(s, c):
        span_rows = SUBLANES * CONV_ROW_STRIDE
        for base in range(0, sub, CONV_ROW_BLOCK):
            starts = [base + sp + j for sp in range(0, CONV_ROW_BLOCK, span_rows)
                      for j in range(CONV_ROW_STRIDE)]
            bias = db_ref[conv_row, c * LANES:(c + 1) * LANES]
            accs = [jnp.zeros((SUBLANES, LANES), F32) + bias for _ in starts]
            for tap in range(CONV_WIDTH):
                w = dw_ref[tap:tap + 1, c * LANES:(c + 1) * LANES]
                off = HALO - CONV_PAD + tap
                for n, start in enumerate(starts):
                    rows = pl.ds(start + off, SUBLANES, stride=CONV_ROW_STRIDE)
                    accs[n] = accs[n] + w * u_scrs[s][c, rows, :]
            for n, start in enumerate(starts):
                c_scrs[s][c, pl.ds(start, SUBLANES, stride=CONV_ROW_STRIDE), :] = accs[n]

    def finish(s):
        r0 = s * sub
        x = x_ref[r0:r0 + sub, :]
        conv = jnp.concatenate([c_scrs[s][c] for c in range(lane_tiles)], axis=1)
        un = _layer_norm(conv, cg_ref[conv_row, :], cb_ref[conv_row, :])
        un = un * jax.nn.sigmoid(un)
        out = _bdot(un.astype(BF16), pw2_ref[...]) + pb2_ref[conv_row, :]
        z = alpha * x + m[2:3] * out
        y_ref[r0:r0 + sub, :] = _layer_norm(z, g_ref[layer:layer + 1, :], b_ref[layer:layer + 1, :])

    for s in range(n_sub):
        modulated_input(s)
    for c in range(lane_tiles):
        glu_chunk(0, c)
    for s in range(n_sub):
        for c in range(lane_tiles):
            conv_taps(s, c)
            if s + 1 < n_sub:
                glu_chunk(s + 1, c)
        finish(s)


def _conv_layer(x, mods, cond0, rows_per_group, seq, layer, conv_layer, pw1, pb1, dw, db, cg, cb, pw2, pb2,
                g, b, alpha):
    m_rows = x.shape[0]
    tm = CONV_TILE_ROWS
    sub = CONV_SUB_ROWS
    seq_per_sub = seq == sub
    assert seq_per_sub or seq % tm == 0
    tiles_per_seq = max(seq // tm, 1)
    in_rows = sub if seq_per_sub else sub + 2 * HALO
    halo_per_tile = tm // HALO
    n_halo_blocks = m_rows // HALO
    kern = functools.partial(_conv_kernel, alpha=alpha, layer=layer, conv_layer=conv_layer, tm=tm, sub=sub,
                             tiles_per_seq=tiles_per_seq, seq_per_sub=seq_per_sub)
    prev_blk = lambda i: (jnp.maximum(i * halo_per_tile - 1, 0), 0)
    next_blk = lambda i: (jnp.minimum((i + 1) * halo_per_tile, n_halo_blocks - 1), 0)
    return pl.pallas_call(
        kern,
        grid=(m_rows // tm,),
        in_specs=[
            pl.BlockSpec((HALO, D_MODEL), prev_blk),
            pl.BlockSpec((tm, D_MODEL), lambda i: (i, 0)),
            pl.BlockSpec((HALO, D_MODEL), next_blk),
            _mod_spec(layer, lambda i: cond0 + i * tm // rows_per_group),
            _layer_spec(pw1.shape, conv_layer),
            _whole_spec(pb1.shape),
            _layer_spec(dw.shape, conv_layer),
            _whole_spec(db.shape),
            _whole_spec(cg.shape),
            _whole_spec(cb.shape),
            _layer_spec(pw2.shape, conv_layer),
            _whole_spec(pb2.shape),
            _whole_spec(g.shape),
            _whole_spec(b.shape),
        ],
        out_specs=pl.BlockSpec((tm, D_MODEL), lambda i: (i, 0)),
        out_shape=jax.ShapeDtypeStruct((m_rows, D_MODEL), F32),
        scratch_shapes=([pltpu.VMEM((in_rows, D_MODEL), BF16)] * (tm // sub)
                        + [pltpu.VMEM((D_MODEL // LANES, sub + 2 * HALO, LANES), F32)] * (tm // sub)
                        + [pltpu.VMEM((D_MODEL // LANES, sub, LANES), F32)] * (tm // sub)),
        compiler_params=_params(1),
        name="conv",
    )(x, x, x, mods, pw1, pb1, dw, db, cg, cb, pw2, pb2, g, b)


def kernel(x_prompt, x_sample, cache_k, cache_v, c, c_ctx, ada_w, ada_b, attn_w_qkv, attn_w_o, attn_sink, conv_pw1_w, conv_pw1_b, conv_dw_w, conv_dw_b, conv_norm_g, conv_norm_b, conv_pw2_w, conv_pw2_b, ln1_g, ln1_b, mlp_w1, mlp_b1, mlp_w2, mlp_b2, ln2_g, ln2_b):
    depth = ada_w.shape[0]
    bsz, seq, _ = x_prompt.shape
    dec_bsz, dec_seq, _ = x_sample.shape
    n_ctx = cache_k.shape[2]
    alpha = (2.0 * depth) ** 0.25
    assert 1 + dec_bsz <= COND_ROWS

    cond = jnp.concatenate([c_ctx[None], c, jnp.zeros((COND_ROWS - 1 - dec_bsz, D_MODEL), F32)], axis=0)
    mods = _modulation(cond, 1 + dec_bsz, ada_w, ada_b).reshape(depth, COND_ROWS, N_MOD, D_MODEL)
    cond_p, cond_s = 0, 1

    cos, sin = _rope_tables(dec_seq)

    wqkv, wo = attn_w_qkv.astype(BF16), attn_w_o.astype(BF16)
    pw1, pw2 = _glu_weight_tiles(conv_pw1_w), conv_pw2_w.astype(BF16)
    w1, w2 = mlp_w1.astype(BF16), mlp_w2.astype(BF16)
    ln1 = (ln1_g, ln1_b)
    conv_w = (pw1, conv_pw1_b, conv_dw_w, conv_dw_b, conv_norm_g, conv_norm_b, pw2, conv_pw2_b)
    mlp_w = (w1, mlp_b1, w2, mlp_b2, ln2_g, ln2_b)

    yp = x_prompt
    ys = x_sample.reshape(dec_bsz * dec_seq, D_MODEL)
    new_k, new_v = [], []
    for i in range(depth):
        j = i // N_MIXERS
        if i % N_MIXERS == 0:
            yp, kp, vp = _prompt_attn_layer(yp, mods, cond_p, attn_sink, i, j, wqkv, wo, *ln1, alpha)
            new_k.append(kp.reshape(bsz, seq, N_KV_HEADS, HEAD_DIM))
            new_v.append(vp.reshape(bsz, seq, N_KV_HEADS, HEAD_DIM))
            q, kv = _sample_qkv(ys, mods, cond_s, dec_seq, cos, sin, i, j, wqkv)
            ctx_kv = jnp.concatenate([cache_k[:, j].reshape(dec_bsz, n_ctx, KV_WIDTH),
                                      cache_v[:, j].reshape(dec_bsz, n_ctx, KV_WIDTH)], axis=-1)
            ys = _sample_attn_layer(ys, mods, cond_s, dec_seq, q, kv, ctx_kv, attn_sink, i, j, wo, *ln1, alpha)
        else:
            yp = _conv_layer(yp.reshape(bsz * seq, D_MODEL), mods, cond_p, bsz * seq, seq, i, j, *conv_w, *ln1,
                             alpha).reshape(bsz, seq, D_MODEL)
            ys = _conv_layer(ys, mods, cond_s, dec_seq, dec_seq, i, j, *conv_w, *ln1, alpha)
        yp = _mlp(yp.reshape(bsz * seq, D_MODEL), mods, cond_p, bsz * seq, i, *mlp_w, alpha).reshape(bsz, seq, D_MODEL)
        ys = _mlp(ys, mods, cond_s, dec_seq, i, *mlp_w, alpha)
    new_cache_k = jnp.stack(new_k, axis=1)
    new_cache_v = jnp.stack(new_v, axis=1)
    return (yp, ys.reshape(dec_bsz, dec_seq, D_MODEL), new_cache_k, new_cache_v)
```

```python
import functools
import math

import jax
import jax.numpy as jnp
import numpy as np
from jax import lax
from jax.experimental import pallas as pl
from jax.experimental.pallas import tpu as pltpu

D_MODEL = 1024
GRID_W = 64
N_MIXERS = 2
HEAD_DIM = 64
N_HEADS = D_MODEL // HEAD_DIM
N_KV_HEADS = 4
GROUP = N_HEADS // N_KV_HEADS
KV_WIDTH = N_KV_HEADS * HEAD_DIM
QK_WIDTH = D_MODEL + KV_WIDTH
QKV_WIDTH = D_MODEL + 2 * KV_WIDTH
BLOCK = 128
ROPE_THETA = 10000.0
CONV_WIDTH = 31
CONV_PAD = (CONV_WIDTH - 1) // 2
D_FF = 4 * D_MODEL
N_MOD = 6
LN_EPS = 1e-5
NEG_INF = -1e30
ATTN_SCALE = HEAD_DIM ** -0.5
LOG2E = math.log2(math.e)
Q_SCALE = ATTN_SCALE * LOG2E

LANES = 128
SUBLANES = 8
VMEM_LIMIT_BYTES = 56 * 1024 * 1024

COND_ROWS = 8
HALO = 16
CONV_ROW_STRIDE = 4
CONV_ROW_BLOCK = 128
CONV_TILE_ROWS = 512
CONV_SUB_ROWS = 256
SOFTMAX_ROWS = 64
QKV_TILE_ROWS = 1024
QKV_SUB_ROWS = 256
PROMPT_SEQS_PER_STEP = 4
MOD_TILE_COLS = 1536
MLP_TILE_ROWS = 512
MLP_SUB_ROWS = 256

BF16 = jnp.bfloat16
F32 = jnp.float32


def _layer_spec(shape, layer):
    n = len(shape)
    return pl.BlockSpec((None,) + tuple(shape[1:]), lambda *_: (layer,) + (0,) * (n - 1),
                        pipeline_mode=pl.Buffered(1))


def _whole_spec(shape):
    n = len(shape)
    return pl.BlockSpec(tuple(shape), lambda *_: (0,) * n, pipeline_mode=pl.Buffered(1))


def _mod_spec(layer, cond_of_step):
    return pl.BlockSpec((None, None, N_MOD, D_MODEL), lambda *ids: (layer, cond_of_step(*ids), 0, 0))


def _params(n_axes):
    return pltpu.CompilerParams(dimension_semantics=("arbitrary",) * n_axes,
                                vmem_limit_bytes=VMEM_LIMIT_BYTES)


def _layer_norm(z, g, b):
    mu = jnp.mean(z, axis=-1, keepdims=True)
    zc = z - mu
    var = jnp.mean(zc * zc, axis=-1, keepdims=True)
    return zc * lax.rsqrt(var + LN_EPS) * g + b


def _bdot(a, b):
    return jnp.dot(a, b, preferred_element_type=F32)


def _bdot_nt(a, b):
    return lax.dot_general(a, b, (((1,), (1,)), ((), ())), preferred_element_type=F32)


def _mod_kernel(cond_ref, w_ref, b_ref, out_ref, s_scr, *, n_cond):
    @pl.when((pl.program_id(0) == 0) & (pl.program_id(1) == 0))
    def _():
        c = cond_ref[...]
        s_t = (c * jax.nn.sigmoid(c)).T
        for m in range(n_cond):
            s_scr[m] = jnp.broadcast_to(s_t[:, m:m + 1], (D_MODEL, LANES))

    zero_rows = [jnp.zeros((1, LANES), F32)] * (COND_ROWS - n_cond)
    for n in range(w_ref.shape[2] // LANES):
        cols = slice(n * LANES, (n + 1) * LANES)
        accs = [jnp.zeros((SUBLANES, LANES), F32) for _ in range(n_cond)]
        for k in range(0, D_MODEL, SUBLANES):
            w = w_ref[0, k:k + SUBLANES, cols]
            for m in range(n_cond):
                accs[m] = accs[m] + w * s_scr[m, k:k + SUBLANES, :]
        out_rows = [jnp.sum(a, axis=0, keepdims=True) for a in accs] + zero_rows
        out_ref[0, :, cols] = jnp.concatenate(out_rows, axis=0) + b_ref[0, :, cols]


def _modulation(cond, n_cond, ada_w, ada_b):
    depth = ada_w.shape[0]
    tn = MOD_TILE_COLS
    width = N_MOD * D_MODEL
    return pl.pallas_call(
        functools.partial(_mod_kernel, n_cond=n_cond),
        grid=(depth, width // tn),
        in_specs=[
            pl.BlockSpec((COND_ROWS, D_MODEL), lambda l, n: (0, 0)),
            pl.BlockSpec((1, D_MODEL, tn), lambda l, n: (l, 0, n)),
            pl.BlockSpec((1, 1, tn), lambda l, n: (l, 0, n)),
        ],
        out_specs=pl.BlockSpec((1, COND_ROWS, tn), lambda l, n: (l, 0, n)),
        out_shape=jax.ShapeDtypeStruct((depth, COND_ROWS, width), F32),
        scratch_shapes=[pltpu.VMEM((n_cond, D_MODEL, LANES), F32)],
        compiler_params=_params(2),
        name="modulation",
    )(cond, ada_w, ada_b.reshape(depth, 1, width))


def _mlp_kernel(y_ref, mod_ref, w1_ref, b1_ref, w2_ref, b2_ref, g_ref, b_ref, out_ref, *,
                alpha, ff_chunk, layer):
    m = mod_ref[...]
    for r0 in range(0, y_ref.shape[0], MLP_SUB_ROWS):
        rows = slice(r0, r0 + MLP_SUB_ROWS)
        y = y_ref[rows, :]
        h = (y * (1.0 + m[4:5]) + m[3:4]).astype(BF16)
        acc = jnp.zeros(y.shape, F32)
        for c in range(D_FF // ff_chunk):
            cols = slice(c * ff_chunk, (c + 1) * ff_chunk)
            a = _bdot(h, w1_ref[:, cols]) + b1_ref[layer:layer + 1, cols]
            a = jnp.maximum(a, 0.0)
            acc = acc + _bdot((a * a).astype(BF16), w2_ref[cols, :])
        z = alpha * y + m[5:6] * (acc + b2_ref[layer:layer + 1, :])
        out_ref[rows, :] = _layer_norm(z, g_ref[layer:layer + 1, :], b_ref[layer:layer + 1, :])


def _mlp(y, mods, cond0, rows_per_group, layer, w1, b1, w2, b2, g, b, alpha):
    m_rows = y.shape[0]
    tm = MLP_TILE_ROWS
    kern = functools.partial(_mlp_kernel, alpha=alpha, ff_chunk=1024, layer=layer)
    return pl.pallas_call(
        kern,
        grid=(m_rows // tm,),
        in_specs=[
            pl.BlockSpec((tm, D_MODEL), lambda i: (i, 0)),
            _mod_spec(layer, lambda i: cond0 + i * tm // rows_per_group),
            _layer_spec(w1.shape, layer),
            _whole_spec(b1.shape),
            _layer_spec(w2.shape, layer),
            _whole_spec(b2.shape),
            _whole_spec(g.shape),
            _whole_spec(b.shape),
        ],
        out_specs=pl.BlockSpec((tm, D_MODEL), lambda i: (i, 0)),
        out_shape=jax.ShapeDtypeStruct((m_rows, D_MODEL), F32),
        compiler_params=_params(1),
        name="mlp",
    )(y, mods, w1, b1, w2, b2, g, b)


def _head_pair_operand(x, kv_head):
    tile = x[:, LANES * (kv_head // 2):LANES * (kv_head // 2 + 1)]
    swapped = pltpu.roll(tile, HEAD_DIM, axis=1)
    lane = lax.broadcasted_iota(jnp.int32, (1, LANES), 1)
    low = lane < HEAD_DIM
    if kv_head % 2 == 0:
        lo_src, hi_src = tile, swapped
    else:
        lo_src, hi_src = swapped, tile
    lo = jnp.where(low, lo_src, 0.0).astype(BF16)
    hi = jnp.where(low, 0.0, hi_src).astype(BF16)
    pieces = []
    for blk in range(x.shape[0] // BLOCK):
        pieces += [lo[blk * BLOCK:(blk + 1) * BLOCK], hi[blk * BLOCK:(blk + 1) * BLOCK]]
    return jnp.concatenate(pieces, axis=0)


def _pair_softmax(s_ref, p_ref, stat_ref, sinks, row_plan):
    def load(rows, block, head, plan_row, biases):
        tile = 2 * block + head
        v = s_ref[rows, tile * LANES:(tile + 1) * LANES]
        if block in biases:
            v = v + biases[block][plan_row:plan_row + SOFTMAX_ROWS, :]
        return v

    for row0, n_rows, blocks, biases in row_plan:
        for r in range(0, n_rows, SOFTMAX_ROWS):
            rows = slice(row0 + r, row0 + r + SOFTMAX_ROWS)
            for head in range(2):
                acc = None
                for block in blocks:
                    v = load(rows, block, head, r, biases)
                    acc = v if acc is None else jnp.maximum(acc, v)
                stat_ref[0, rows, head * LANES:(head + 1) * LANES] = acc
    mx = [jnp.maximum(jnp.max(stat_ref[0, :, head * LANES:(head + 1) * LANES], axis=-1, keepdims=True),
                      sinks[head]) for head in range(2)]
    for row0, n_rows, blocks, biases in row_plan:
        for r in range(0, n_rows, SOFTMAX_ROWS):
            rows = slice(row0 + r, row0 + r + SOFTMAX_ROWS)
            for head in range(2):
                m_rows = mx[head][row0 + r:row0 + r + SOFTMAX_ROWS]
                acc = None
                for block in blocks:
                    p = jnp.exp2(load(rows, block, head, r, biases) - m_rows)
                    acc = p if acc is None else acc + p
                    tile = 2 * block + head
                    p_ref[rows, tile * LANES:(tile + 1) * LANES] = p.astype(BF16)
                stat_ref[1, rows, head * LANES:(head + 1) * LANES] = acc
    recip = [1.0 / (jnp.sum(stat_ref[1, :, head * LANES:(head + 1) * LANES], axis=-1, keepdims=True)
                    + jnp.exp2(sinks[head] - mx[head])) for head in range(2)]
    lane = lax.broadcasted_iota(jnp.int32, (1, LANES), 1)
    return jnp.where(lane < HEAD_DIM, recip[0], recip[1])


def _attention_pipeline(units, sink_ref, attn_layer, o_scr, stat_scr):
    def scores(u):
        u["s_ref"][...] = _bdot_nt(u["q"](), u["kk"]())

    def values(u, recip):
        o = _bdot(u["p_ref"][...], u["vv"]()) * recip
        o_scr[u["rows"], LANES * u["t"]:LANES * (u["t"] + 1)] = o.astype(BF16)

    scores(units[0])
    recip_prev = None
    for n, u in enumerate(units):
        if n + 1 < len(units):
            scores(units[n + 1])
        if n > 0:
            values(units[n - 1], recip_prev)
        sinks = (sink_ref[attn_layer, 2 * u["t"]] * LOG2E, sink_ref[attn_layer, 2 * u["t"] + 1] * LOG2E)
        recip_prev = _pair_softmax(u["s_ref"], u["p_ref"], stat_scr.at[n % 2], sinks, u["plan"])
    values(units[-1], recip_prev)


def _cached(fn):
    memo = {}

    def get(*key):
        if key not in memo:
            memo[key] = fn(*key)
        return memo[key]
    return get


def _prompt_attn_kernel(sink_ref, x_ref, mod_ref, wqkv_ref, wo_ref, g_ref, b_ref,
                        y_ref, k_ref, v_ref, o_scr, s_scr, p_scr, stat_scr, *, alpha, layer, attn_layer):
    n_seq, seq, _ = x_ref.shape
    m = mod_ref[...]
    x = x_ref[...].reshape(n_seq * seq, D_MODEL)
    h = (x * (1.0 + m[1:2]) + m[0:1]).astype(BF16)
    qkv = _bdot(h, wqkv_ref[...])
    q = (qkv[:, :D_MODEL] * Q_SCALE).astype(BF16)
    plan = [(0, seq, list(range(seq // BLOCK)), {})]
    units = []
    for sq in range(n_seq):
        rows = slice(sq * seq, (sq + 1) * seq)
        k = qkv[rows, D_MODEL:QK_WIDTH]
        v = qkv[rows, QK_WIDTH:]
        k_ref[sq] = k
        v_ref[sq] = v
        kk = _cached(functools.partial(_head_pair_operand, k))
        vv = _cached(functools.partial(_head_pair_operand, v))
        for t in range(D_MODEL // LANES):
            n = len(units)
            units.append(dict(q=functools.partial(lambda rows, t: q[rows, LANES * t:LANES * (t + 1)], rows, t),
                              kk=functools.partial(kk, t // (GROUP // 2)),
                              vv=functools.partial(vv, t // (GROUP // 2)),
                              t=t, rows=rows, s_ref=s_scr.at[n % 2], p_ref=p_scr.at[n % 2], plan=plan))
    _attention_pipeline(units, sink_ref, attn_layer, o_scr, stat_scr)
    out = _bdot(o_scr[...], wo_ref[...])
    z = alpha * x + m[2:3] * out
    y_ref[...] = _layer_norm(z, g_ref[layer:layer + 1, :], b_ref[layer:layer + 1, :]).reshape(n_seq, seq, D_MODEL)


def _prompt_attn_layer(x, mods, cond0, sink, layer, attn_layer, wqkv, wo, g, b, alpha):
    bsz, seq, _ = x.shape
    n_seq = PROMPT_SEQS_PER_STEP
    assert bsz % n_seq == 0
    width = 2 * (seq // BLOCK) * LANES
    kern = functools.partial(_prompt_attn_kernel, alpha=alpha, layer=layer, attn_layer=attn_layer)
    return pl.pallas_call(
        kern,
        grid=(bsz // n_seq,),
        in_specs=[
            pl.BlockSpec(memory_space=pltpu.SMEM),
            pl.BlockSpec((n_seq, seq, D_MODEL), lambda i: (i, 0, 0)),
            _mod_spec(layer, lambda i: cond0),
            _layer_spec(wqkv.shape, attn_layer),
            _layer_spec(wo.shape, attn_layer),
            _whole_spec(g.shape),
            _whole_spec(b.shape),
        ],
        out_specs=[
            pl.BlockSpec((n_seq, seq, D_MODEL), lambda i: (i, 0, 0)),
            pl.BlockSpec((n_seq, seq, KV_WIDTH), lambda i: (i, 0, 0)),
            pl.BlockSpec((n_seq, seq, KV_WIDTH), lambda i: (i, 0, 0)),
        ],
        out_shape=[
            jax.ShapeDtypeStruct((bsz, seq, D_MODEL), F32),
            jax.ShapeDtypeStruct((bsz, seq, KV_WIDTH), F32),
            jax.ShapeDtypeStruct((bsz, seq, KV_WIDTH), F32),
        ],
        scratch_shapes=[pltpu.VMEM((n_seq * seq, D_MODEL), BF16),
                        pltpu.VMEM((2, seq, width), F32),
                        pltpu.VMEM((2, seq, width), BF16),
                        pltpu.VMEM((2, 2, seq, 2 * LANES), F32)],
        compiler_params=_params(1),
        name="prompt_attn",
    )(sink, x, mods, wqkv, wo, g, b)


def _rope_tables(n_tokens):
    rows = n_tokens // GRID_W
    row = np.repeat(np.arange(rows, dtype=np.float32), GRID_W)
    col = np.tile(np.arange(GRID_W, dtype=np.float32), rows)
    half = HEAD_DIM // 2
    inv = np.float32(ROPE_THETA) ** (-np.arange(0, half, 2, dtype=np.float32) / np.float32(half))
    ang_r = row[:, None] * inv
    ang_c = col[:, None] * inv
    cos = np.concatenate([np.cos(ang_r)] * 2 + [np.cos(ang_c)] * 2, axis=-1)
    sin = np.concatenate([-np.sin(ang_r), np.sin(ang_r), -np.sin(ang_c), np.sin(ang_c)], axis=-1)
    reps = (1, LANES // HEAD_DIM)
    return jnp.asarray(np.tile(cos, reps), F32), jnp.asarray(np.tile(sin, reps), F32)


def _sample_qkv_kernel(x_ref, mod_ref, cos_ref, sin_ref, wqkv_ref, q_ref, kv_ref, h_scr):
    m = mod_ref[...]
    lane = lax.broadcasted_iota(jnp.int32, (1, LANES), 1)
    first = (lane % (HEAD_DIM // 2)) < (HEAD_DIM // 4)
    quarter = HEAD_DIM // 4
    h_scr[...] = (x_ref[...] * (1.0 + m[1:2]) + m[0:1]).astype(BF16)
    for r0 in range(0, x_ref.shape[0], QKV_SUB_ROWS):
        rows = slice(r0, r0 + QKV_SUB_ROWS)
        qkv = _bdot(h_scr[rows, :], wqkv_ref[...])
        cos = cos_ref[rows, :]
        sin = sin_ref[rows, :]
        for t in range(QK_WIDTH // LANES):
            tile = qkv[:, LANES * t:LANES * (t + 1)]
            partner = jnp.where(first, pltpu.roll(tile, LANES - quarter, axis=1),
                                pltpu.roll(tile, quarter, axis=1))
            rot = tile * cos + partner * sin
            if t < D_MODEL // LANES:
                q_ref[rows, LANES * t:LANES * (t + 1)] = (rot * Q_SCALE).astype(BF16)
            else:
                tk = t - D_MODEL // LANES
                kv_ref[rows, LANES * tk:LANES * (tk + 1)] = rot.astype(BF16)
        kv_ref[rows, KV_WIDTH:] = qkv[:, QK_WIDTH:].astype(BF16)


def _sample_qkv(x, mods, cond0, seq, cos, sin, layer, attn_layer, wqkv):
    m_rows = x.shape[0]
    tm = QKV_TILE_ROWS
    tiles_per_seq = seq // tm
    return pl.pallas_call(
        _sample_qkv_kernel,
        grid=(m_rows // tm,),
        in_specs=[
            pl.BlockSpec((tm, D_MODEL), lambda i: (i, 0)),
            _mod_spec(layer, lambda i: cond0 + i // tiles_per_seq),
            pl.BlockSpec((tm, LANES), lambda i: (i % tiles_per_seq, 0)),
            pl.BlockSpec((tm, LANES), lambda i: (i % tiles_per_seq, 0)),
            _layer_spec(wqkv.shape, attn_layer),
        ],
        out_specs=[
            pl.BlockSpec((tm, D_MODEL), lambda i: (i, 0)),
            pl.BlockSpec((tm, 2 * KV_WIDTH), lambda i: (i, 0)),
        ],
        out_shape=[
            jax.ShapeDtypeStruct((m_rows, D_MODEL), BF16),
            jax.ShapeDtypeStruct((m_rows, 2 * KV_WIDTH), BF16),
        ],
        scratch_shapes=[pltpu.VMEM((tm, D_MODEL), BF16)],
        compiler_params=_params(1),
        name="sample_qkv",
    )(x, mods, cos, sin, wqkv)


def _sample_attn_kernel(sink_ref, x_ref, mod_ref, q_ref, kvp_ref, kvm_ref, kvn_ref, ctx_ref,
                        wo_ref, g_ref, b_ref, y_ref, o_scr, s_scr, p_scr, stat_scr, bias_scr, *,
                        alpha, layer, attn_layer):
    i = pl.program_id(1)
    n_tiles = pl.num_programs(1)
    m = mod_ref[...]
    nc = ctx_ref.shape[1] // BLOCK
    window = 4 + nc
    half_rows = 2 * BLOCK
    kv_all = jnp.concatenate([kvp_ref[...].astype(F32), kvm_ref[0:BLOCK, :].astype(F32), ctx_ref[0],
                              kvm_ref[BLOCK:, :].astype(F32), kvn_ref[...].astype(F32)], axis=0)
    k_all, v_all = kv_all[:, :KV_WIDTH], kv_all[:, KV_WIDTH:]
    qi = lax.broadcasted_iota(jnp.int32, (BLOCK, BLOCK), 0)
    ki = lax.broadcasted_iota(jnp.int32, (BLOCK, BLOCK), 1)
    bias_scr[0] = jnp.where(ki >= qi, jnp.where(i > 0, 0.0, NEG_INF), NEG_INF)
    bias_scr[1] = jnp.where(ki <= qi, 0.0, NEG_INF)
    bias_scr[2] = jnp.where(ki >= qi, 0.0, NEG_INF)
    bias_scr[3] = jnp.where(ki <= qi, jnp.where(i < n_tiles - 1, 0.0, NEG_INF), NEG_INF)
    prev_edge, next_plain, prev_plain, next_edge = (bias_scr.at[j] for j in range(4))
    ctx_a = list(range(2, 2 + nc))
    ctx_b = list(range(nc))
    plans = [
        [(0, BLOCK, [0, 1] + ctx_a + [2 + nc], {0: prev_edge, 2 + nc: next_plain}),
         (BLOCK, BLOCK, [1] + ctx_a + [2 + nc, 3 + nc], {1: prev_plain, 3 + nc: next_plain})],
        [(0, BLOCK, ctx_b + [nc, nc + 1, nc + 2], {nc: prev_plain, nc + 2: next_plain}),
         (BLOCK, BLOCK, ctx_b + [nc + 1, nc + 2, nc + 3], {nc + 1: prev_plain, nc + 3: next_edge})],
    ]
    unused = [[3 + nc, 0], [nc + 3, nc]]
    for half in range(2):
        for parity in range(2):
            for qb in range(2):
                blk = unused[half][qb]
                p_scr[2 * half + parity, qb * BLOCK:(qb + 1) * BLOCK,
                      2 * blk * LANES:2 * (blk + 1) * LANES] = jnp.zeros((BLOCK, 2 * LANES), BF16)
    kk = _cached(lambda kv_head: _head_pair_operand(k_all, kv_head))
    vv = _cached(lambda kv_head: _head_pair_operand(v_all, kv_head))

    def operand(full, kv_head, half):
        first = 2 * half * 2 * BLOCK
        return full(kv_head)[first:first + window * 2 * BLOCK]

    units = []
    for kv_head in range(N_KV_HEADS):
        for half in range(2):
            for t in range(kv_head * (GROUP // 2), (kv_head + 1) * (GROUP // 2)):
                n = len(units)
                rows = slice(half * half_rows, (half + 1) * half_rows)
                units.append(dict(
                    q=functools.partial(lambda rows, t: q_ref[rows, LANES * t:LANES * (t + 1)], rows, t),
                    kk=functools.partial(operand, kk, kv_head, half),
                    vv=functools.partial(operand, vv, kv_head, half),
                    t=t, rows=rows, s_ref=s_scr.at[n % 2], p_ref=p_scr.at[2 * half + n % 2],
                    plan=plans[half]))
    _attention_pipeline(units, sink_ref, attn_layer, o_scr, stat_scr)
    out = _bdot(o_scr[...], wo_ref[...])
    z = alpha * x_ref[...] + m[2:3] * out
    y_ref[...] = _layer_norm(z, g_ref[layer:layer + 1, :], b_ref[layer:layer + 1, :])


def _sample_attn_layer(x, mods, cond0, seq, q, kv, ctx_kv, sink, layer, attn_layer, wo, g, b, alpha):
    m_rows = x.shape[0]
    bsz = m_rows // seq
    tq = 4 * BLOCK
    tiles = seq // tq
    blocks = seq // BLOCK
    n_ctx = ctx_kv.shape[1]
    assert n_ctx % BLOCK == 0 and seq % tq == 0
    window = 4 + n_ctx // BLOCK
    kern = functools.partial(_sample_attn_kernel, alpha=alpha, layer=layer, attn_layer=attn_layer)
    row_tile = lambda bi, i: (bi * tiles + i, 0)
    prev_blk = lambda bi, i: (bi * blocks + jnp.maximum(4 * i - 1, 0), 0)
    next_blk = lambda bi, i: (bi * blocks + jnp.minimum(4 * i + 4, blocks - 1), 0)
    ctx_blk = lambda bi, i: (bi, 0, 0)
    half_rows = 2 * BLOCK
    width = 2 * window * LANES
    return pl.pallas_call(
        kern,
        grid=(bsz, tiles),
        in_specs=[
            pl.BlockSpec(memory_space=pltpu.SMEM),
            pl.BlockSpec((tq, D_MODEL), row_tile),
            _mod_spec(layer, lambda bi, i: cond0 + bi),
            pl.BlockSpec((tq, D_MODEL), row_tile),
            pl.BlockSpec((BLOCK, 2 * KV_WIDTH), prev_blk),
            pl.BlockSpec((tq, 2 * KV_WIDTH), row_tile),
            pl.BlockSpec((BLOCK, 2 * KV_WIDTH), next_blk),
            pl.BlockSpec((1, n_ctx, 2 * KV_WIDTH), ctx_blk),
            _layer_spec(wo.shape, attn_layer),
            _whole_spec(g.shape),
            _whole_spec(b.shape),
        ],
        out_specs=pl.BlockSpec((tq, D_MODEL), row_tile),
        out_shape=jax.ShapeDtypeStruct((m_rows, D_MODEL), F32),
        scratch_shapes=[pltpu.VMEM((tq, D_MODEL), BF16),
                        pltpu.VMEM((2, half_rows, width), F32),
                        pltpu.VMEM((4, half_rows, width), BF16),
                        pltpu.VMEM((2, 2, half_rows, 2 * LANES), F32),
                        pltpu.VMEM((4, BLOCK, BLOCK), F32)],
        compiler_params=_params(2),
        name="sample_attn",
    )(sink, x, mods, q, kv, kv, kv, ctx_kv, wo, g, b)


def _glu_weight_kernel(w_ref, out_ref):
    for c in range(D_MODEL // LANES):
        out_ref[c, :, :LANES] = w_ref[:, c * LANES:(c + 1) * LANES].astype(BF16)
        out_ref[c, :, LANES:] = w_ref[:, D_MODEL + c * LANES:D_MODEL + (c + 1) * LANES].astype(BF16)


def _glu_weight_tiles(pw1):
    n_layers = pw1.shape[0]
    lane_tiles = D_MODEL // LANES
    tk = D_MODEL // 2
    return pl.pallas_call(
        _glu_weight_kernel,
        grid=(n_layers, D_MODEL // tk),
        in_specs=[pl.BlockSpec((None, tk, 2 * D_MODEL), lambda l, r: (l, r, 0))],
        out_specs=pl.BlockSpec((None, lane_tiles, tk, 2 * LANES), lambda l, r: (l, 0, r, 0)),
        out_shape=jax.ShapeDtypeStruct((n_layers, lane_tiles, D_MODEL, 2 * LANES), BF16),
        compiler_params=_params(2),
        name="glu_weight_tiles",
    )(pw1)


def _conv_kernel(xp_ref, x_ref, xn_ref, mod_ref, pw1_ref, pb1_ref, dw_ref, db_ref,
                 cg_ref, cb_ref, pw2_ref, pb2_ref, g_ref, b_ref, y_ref, *scratch,
                 alpha, layer, conv_layer, tm, sub, tiles_per_seq, seq_per_sub):
    i = pl.program_id(0)
    t = i % tiles_per_seq
    m = mod_ref[...]
    lane_tiles = D_MODEL // LANES
    n_sub = tm // sub
    h_scrs, u_scrs, c_scrs = scratch[:n_sub], scratch[n_sub:2 * n_sub], scratch[2 * n_sub:]
    recompute_halo = not seq_per_sub
    conv_row = slice(conv_layer, conv_layer + 1)

    def modulated_input(s):
        r0 = s * sub
        rows = x_ref[r0:r0 + sub, :]
        if recompute_halo:
            top = xp_ref[...] if s == 0 else x_ref[r0 - HALO:r0, :]
            bot = xn_ref[...] if s == n_sub - 1 else x_ref[r0 + sub:r0 + sub + HALO, :]
            rows = jnp.concatenate([top, rows, bot], axis=0)
        h_scrs[s][...] = (rows * (1.0 + m[1:2]) + m[0:1]).astype(BF16)

    def glu_chunk(s, c):
        a = _bdot(h_scrs[s][...], pw1_ref[c])
        value = a[:, :LANES] + pb1_ref[conv_row, c * LANES:(c + 1) * LANES]
        gate = a[:, LANES:] + pb1_ref[conv_row, D_MODEL + c * LANES:D_MODEL + (c + 1) * LANES]
        u = value * jax.nn.sigmoid(gate)
        zeros = jnp.zeros((HALO, LANES), F32)
        if recompute_halo:
            u_scrs[s][c, :, :] = u
            if s == 0:
                u_scrs[s][c, 0:HALO, :] = jnp.where(t > 0, u[:HALO], 0.0)
            if s == n_sub - 1:
                u_scrs[s][c, HALO + sub:, :] = jnp.where(t < tiles_per_seq - 1, u[HALO + sub:], 0.0)
        else:
            u_scrs[s][c, 0:HALO, :] = zeros
            u_scrs[s][c, HALO:HALO + sub, :] = u
            u_scrs[s][c, HALO + sub:, :] = zeros

    def conv_taps(s, c):
        span_rows = SUBLANES * CONV_ROW_STRIDE
        for base in range(0, sub, CONV_ROW_BLOCK):
            starts = [base + sp + j for sp in range(0, CONV_ROW_BLOCK, span_rows)
                      for j in range(CONV_ROW_STRIDE)]
            bias = db_ref[conv_row, c * LANES:(c + 1) * LANES]
            accs = [jnp.zeros((SUBLANES, LANES), F32) + bias for _ in starts]
            for tap in range(CONV_WIDTH):
                w = dw_ref[tap:tap + 1, c * LANES:(c + 1) * LANES]
                off = HALO - CONV_PAD + tap
                for n, start in enumerate(starts):
                    rows = pl.ds(start + off, SUBLANES, stride=CONV_ROW_STRIDE)
                    accs[n] = accs[n] + w * u_scrs[s][c, rows, :]
            for n, start in enumerate(starts):
                c_scrs[s][c, pl.ds(start, SUBLANES, stride=CONV_ROW_STRIDE), :] = accs[n]

    def finish(s):
        r0 = s * sub
        x = x_ref[r0:r0 + sub, :]
        conv = jnp.concatenate([c_scrs[s][c] for c in range(lane_tiles)], axis=1)
        un = _layer_norm(conv, cg_ref[conv_row, :], cb_ref[conv_row, :])
        un = un * jax.nn.sigmoid(un)
        out = _bdot(un.astype(BF16), pw2_ref[...]) + pb2_ref[conv_row, :]
        z = alpha * x + m[2:3] * out
        y_ref[r0:r0 + sub, :] = _layer_norm(z, g_ref[layer:layer + 1, :], b_ref[layer:layer + 1, :])

    for s in range(n_sub):
        modulated_input(s)
    for c in range(lane_tiles):
        glu_chunk(0, c)
    for s in range(n_sub):
        for c in range(lane_tiles):
            conv_taps(s, c)
            if s + 1 < n_sub:
                glu_chunk(s + 1, c)
        finish(s)


def _conv_layer(x, mods, cond0, rows_per_group, seq, layer, conv_layer, pw1, pb1, dw, db, cg, cb, pw2, pb2,
                g, b, alpha):
    m_rows = x.shape[0]
    tm = CONV_TILE_ROWS
    sub = CONV_SUB_ROWS
    seq_per_sub = seq == sub
    assert seq_per_sub or seq % tm == 0
    tiles_per_seq = max(seq // tm, 1)
    in_rows = sub if seq_per_sub else sub + 2 * HALO
    halo_per_tile = tm // HALO
    n_halo_blocks = m_rows // HALO
    kern = functools.partial(_conv_kernel, alpha=alpha, layer=layer, conv_layer=conv_layer, tm=tm, sub=sub,
                             tiles_per_seq=tiles_per_seq, seq_per_sub=seq_per_sub)
    prev_blk = lambda i: (jnp.maximum(i * halo_per_tile - 1, 0), 0)
    next_blk = lambda i: (jnp.minimum((i + 1) * halo_per_tile, n_halo_blocks - 1), 0)
    return pl.pallas_call(
        kern,
        grid=(m_rows // tm,),
        in_specs=[
            pl.BlockSpec((HALO, D_MODEL), prev_blk),
            pl.BlockSpec((tm, D_MODEL), lambda i: (i, 0)),
            pl.BlockSpec((HALO, D_MODEL), next_blk),
            _mod_spec(layer, lambda i: cond0 + i * tm // rows_per_group),
            _layer_spec(pw1.shape, conv_layer),
            _whole_spec(pb1.shape),
            _layer_spec(dw.shape, conv_layer),
            _whole_spec(db.shape),
            _whole_spec(cg.shape),
            _whole_spec(cb.shape),
            _layer_spec(pw2.shape, conv_layer),
            _whole_spec(pb2.shape),
            _whole_spec(g.shape),
            _whole_spec(b.shape),
        ],
        out_specs=pl.BlockSpec((tm, D_MODEL), lambda i: (i, 0)),
        out_shape=jax.ShapeDtypeStruct((m_rows, D_MODEL), F32),
        scratch_shapes=([pltpu.VMEM((in_rows, D_MODEL), BF16)] * (tm // sub)
                        + [pltpu.VMEM((D_MODEL // LANES, sub + 2 * HALO, LANES), F32)] * (tm // sub)
                        + [pltpu.VMEM((D_MODEL // LANES, sub, LANES), F32)] * (tm // sub)),
        compiler_params=_params(1),
        name="conv",
    )(x, x, x, mods, pw1, pb1, dw, db, cg, cb, pw2, pb2, g, b)


def kernel(x_prompt, x_sample, cache_k, cache_v, c, c_ctx, ada_w, ada_b, attn_w_qkv, attn_w_o, attn_sink, conv_pw1_w, conv_pw1_b, conv_dw_w, conv_dw_b, conv_norm_g, conv_norm_b, conv_pw2_w, conv_pw2_b, ln1_g, ln1_b, mlp_w1, mlp_b1, mlp_w2, mlp_b2, ln2_g, ln2_b):
    depth = ada_w.shape[0]
    bsz, seq, _ = x_prompt.shape
    dec_bsz, dec_seq, _ = x_sample.shape
    n_ctx = cache_k.shape[2]
    alpha = (2.0 * depth) ** 0.25
    assert 1 + dec_bsz <= COND_ROWS

    cond = jnp.concatenate([c_ctx[None], c, jnp.zeros((COND_ROWS - 1 - dec_bsz, D_MODEL), F32)], axis=0)
    mods = _modulation(cond, 1 + dec_bsz, ada_w, ada_b).reshape(depth, COND_ROWS, N_MOD, D_MODEL)
    cond_p, cond_s = 0, 1

    cos, sin = _rope_tables(dec_seq)

    wqkv, wo = attn_w_qkv.astype(BF16), attn_w_o.astype(BF16)
    pw1, pw2 = _glu_weight_tiles(conv_pw1_w), conv_pw2_w.astype(BF16)
    w1, w2 = mlp_w1.astype(BF16), mlp_w2.astype(BF16)
    ln1 = (ln1_g, ln1_b)
    conv_w = (pw1, conv_pw1_b, conv_dw_w, conv_dw_b, conv_norm_g, conv_norm_b, pw2, conv_pw2_b)
    mlp_w = (w1, mlp_b1, w2, mlp_b2, ln2_g, ln2_b)

    yp = x_prompt
    ys = x_sample.reshape(dec_bsz * dec_seq, D_MODEL)
    new_k, new_v = [], []
    for i in range(depth):
        j = i // N_MIXERS
        if i % N_MIXERS == 0:
            yp, kp, vp = _prompt_attn_layer(yp, mods, cond_p, attn_sink, i, j, wqkv, wo, *ln1, alpha)
            new_k.append(kp.reshape(bsz, seq, N_KV_HEADS, HEAD_DIM))
            new_v.append(vp.reshape(bsz, seq, N_KV_HEADS, HEAD_DIM))
            q, kv = _sample_qkv(ys, mods, cond_s, dec_seq, cos, sin, i, j, wqkv)
            ctx_kv = jnp.concatenate([cache_k[:, j].reshape(dec_bsz, n_ctx, KV_WIDTH),
                                      cache_v[:, j].reshape(dec_bsz, n_ctx, KV_WIDTH)], axis=-1)
            ys = _sample_attn_layer(ys, mods, cond_s, dec_seq, q, kv, ctx_kv, attn_sink, i, j, wo, *ln1, alpha)
        else:
            yp = _conv_layer(yp.reshape(bsz * seq, D_MODEL), mods, cond_p, bsz * seq, seq, i, j, *conv_w, *ln1,
                             alpha).reshape(bsz, seq, D_MODEL)
            ys = _conv_layer(ys, mods, cond_s, dec_seq, dec_seq, i, j, *conv_w, *ln1, alpha)
        yp = _mlp(yp.reshape(bsz * seq, D_MODEL), mods, cond_p, bsz * seq, i, *mlp_w, alpha).reshape(bsz, seq, D_MODEL)
        ys = _mlp(ys, mods, cond_s, dec_seq, i, *mlp_w, alpha)
    new_cache_k = jnp.stack(new_k, axis=1)
    new_cache_v = jnp.stack(new_v, axis=1)
    return (yp, ys.reshape(dec_bsz, dec_seq, D_MODEL), new_cache_k, new_cache_v)
```

```python
import functools
import math

import jax
import jax.numpy as jnp
import numpy as np
from jax import lax
from jax.experimental import pallas as pl
from jax.experimental.pallas import tpu as pltpu

D_MODEL = 1024
GRID_W = 64
N_MIXERS = 2
HEAD_DIM = 64
N_HEADS = D_MODEL // HEAD_DIM
N_KV_HEADS = 4
GROUP = N_HEADS // N_KV_HEADS
KV_WIDTH = N_KV_HEADS * HEAD_DIM
QK_WIDTH = D_MODEL + KV_WIDTH
QKV_WIDTH = D_MODEL + 2 * KV_WIDTH
BLOCK = 128
ROPE_THETA = 10000.0
CONV_WIDTH = 31
CONV_PAD = (CONV_WIDTH - 1) // 2
D_FF = 4 * D_MODEL
N_MOD = 6
LN_EPS = 1e-5
NEG_INF = -1e30
ATTN_SCALE = HEAD_DIM ** -0.5
LOG2E = math.log2(math.e)
Q_SCALE = ATTN_SCALE * LOG2E

LANES = 128
SUBLANES = 8
VMEM_LIMIT_BYTES = 56 * 1024 * 1024

COND_ROWS = 8
HALO = 16
CONV_ROW_STRIDE = 4
CONV_ROW_BLOCK = 128
CONV_TILE_ROWS = 512
CONV_SUB_ROWS = 256
SOFTMAX_ROWS = 64
QKV_TILE_ROWS = 1024
QKV_SUB_ROWS = 256
PROMPT_SEQS_PER_STEP = 4
MOD_TILE_COLS = 1536
MLP_TILE_ROWS = 512
MLP_SUB_ROWS = 256

BF16 = jnp.bfloat16
F32 = jnp.float32


def _layer_spec(shape, layer):
    n = len(shape)
    return pl.BlockSpec((None,) + tuple(shape[1:]), lambda *_: (layer,) + (0,) * (n - 1),
                        pipeline_mode=pl.Buffered(1))


def _whole_spec(shape):
    n = len(shape)
    return pl.BlockSpec(tuple(shape), lambda *_: (0,) * n, pipeline_mode=pl.Buffered(1))


def _mod_spec(layer, cond_of_step):
    return pl.BlockSpec((None, None, N_MOD, D_MODEL), lambda *ids: (layer, cond_of_step(*ids), 0, 0))


def _params(n_axes):
    return pltpu.CompilerParams(dimension_semantics=("arbitrary",) * n_axes,
                                vmem_limit_bytes=VMEM_LIMIT_BYTES)


def _layer_norm(z, g, b):
    mu = jnp.mean(z, axis=-1, keepdims=True)
    zc = z - mu
    var = jnp.mean(zc * zc, axis=-1, keepdims=True)
    return zc * lax.rsqrt(var + LN_EPS) * g + b


def _bdot(a, b):
    return jnp.dot(a, b, preferred_element_type=F32)


def _bdot_nt(a, b):
    return lax.dot_general(a, b, (((1,), (1,)), ((), ())), preferred_element_type=F32)


def _mod_kernel(cond_ref, w_ref, b_ref, out_ref, s_scr, *, n_cond):
    @pl.when((pl.program_id(0) == 0) & (pl.program_id(1) == 0))
    def _():
        c = cond_ref[...]
        s_t = (c * jax.nn.sigmoid(c)).T
        for m in range(n_cond):
            s_scr[m] = jnp.broadcast_to(s_t[:, m:m + 1], (D_MODEL, LANES))

    zero_rows = [jnp.zeros((1, LANES), F32)] * (COND_ROWS - n_cond)
    for n in range(w_ref.shape[2] // LANES):
        cols = slice(n * LANES, (n + 1) * LANES)
        accs = [jnp.zeros((SUBLANES, LANES), F32) for _ in range(n_cond)]
        for k in range(0, D_MODEL, SUBLANES):
            w = w_ref[0, k:k + SUBLANES, cols]
            for m in range(n_cond):
                accs[m] = accs[m] + w * s_scr[m, k:k + SUBLANES, :]
        out_rows = [jnp.sum(a, axis=0, keepdims=True) for a in accs] + zero_rows
        out_ref[0, :, cols] = jnp.concatenate(out_rows, axis=0) + b_ref[0, :, cols]


def _modulation(cond, n_cond, ada_w, ada_b):
    depth = ada_w.shape[0]
    tn = MOD_TILE_COLS
    width = N_MOD * D_MODEL
    return pl.pallas_call(
        functools.partial(_mod_kernel, n_cond=n_cond),
        grid=(depth, width // tn),
        in_specs=[
            pl.BlockSpec((COND_ROWS, D_MODEL), lambda l, n: (0, 0)),
            pl.BlockSpec((1, D_MODEL, tn), lambda l, n: (l, 0, n)),
            pl.BlockSpec((1, 1, tn), lambda l, n: (l, 0, n)),
        ],
        out_specs=pl.BlockSpec((1, COND_ROWS, tn), lambda l, n: (l, 0, n)),
        out_shape=jax.ShapeDtypeStruct((depth, COND_ROWS, width), F32),
        scratch_shapes=[pltpu.VMEM((n_cond, D_MODEL, LANES), F32)],
        compiler_params=_params(2),
        name="modulation",
    )(cond, ada_w, ada_b.reshape(depth, 1, width))


def _mlp_kernel(y_ref, mod_ref, w1_ref, b1_ref, w2_ref, b2_ref, g_ref, b_ref, out_ref, *,
                alpha, ff_chunk, layer):
    m = mod_ref[...]
    for r0 in range(0, y_ref.shape[0], MLP_SUB_ROWS):
        rows = slice(r0, r0 + MLP_SUB_ROWS)
        y = y_ref[rows, :]
        h = (y * (1.0 + m[4:5]) + m[3:4]).astype(BF16)
        acc = jnp.zeros(y.shape, F32)
        for c in range(D_FF // ff_chunk):
            cols = slice(c * ff_chunk, (c + 1) * ff_chunk)
            a = _bdot(h, w1_ref[:, cols]) + b1_ref[layer:layer + 1, cols]
            a = jnp.maximum(a, 0.0)
            acc = acc + _bdot((a * a).astype(BF16), w2_ref[cols, :])
        z = alpha * y + m[5:6] * (acc + b2_ref[layer:layer + 1, :])
        out_ref[rows, :] = _layer_norm(z, g_ref[layer:layer + 1, :], b_ref[layer:layer + 1, :])


def _mlp(y, mods, cond0, rows_per_group, layer, w1, b1, w2, b2, g, b, alpha):
    m_rows = y.shape[0]
    tm = MLP_TILE_ROWS
    kern = functools.partial(_mlp_kernel, alpha=alpha, ff_chunk=1024, layer=layer)
    return pl.pallas_call(
        kern,
        grid=(m_rows // tm,),
        in_specs=[
            pl.BlockSpec((tm, D_MODEL), lambda i: (i, 0)),
            _mod_spec(layer, lambda i: cond0 + i * tm // rows_per_group),
            _layer_spec(w1.shape, layer),
            _whole_spec(b1.shape),
            _layer_spec(w2.shape, layer),
            _whole_spec(b2.shape),
            _whole_spec(g.shape),
            _whole_spec(b.shape),
        ],
        out_specs=pl.BlockSpec((tm, D_MODEL), lambda i: (i, 0)),
        out_shape=jax.ShapeDtypeStruct((m_rows, D_MODEL), F32),
        compiler_params=_params(1),
        name="mlp",
    )(y, mods, w1, b1, w2, b2, g, b)


def _head_pair_operand(x, kv_head):
    tile = x[:, LANES * (kv_head // 2):LANES * (kv_head // 2 + 1)]
    swapped = pltpu.roll(tile, HEAD_DIM, axis=1)
    lane = lax.broadcasted_iota(jnp.int32, (1, LANES), 1)
    low = lane < HEAD_DIM
    if kv_head % 2 == 0:
        lo_src, hi_src = tile, swapped
    else:
        lo_src, hi_src = swapped, tile
    lo = jnp.where(low, lo_src, 0.0).astype(BF16)
    hi = jnp.where(low, 0.0, hi_src).astype(BF16)
    pieces = []
    for blk in range(x.shape[0] // BLOCK):
        pieces += [lo[blk * BLOCK:(blk + 1) * BLOCK], hi[blk * BLOCK:(blk + 1) * BLOCK]]
    return jnp.concatenate(pieces, axis=0)


def _pair_softmax(s_ref, p_ref, stat_ref, sinks, row_plan):
    def load(rows, block, head, plan_row, biases):
        tile = 2 * block + head
        v = s_ref[rows, tile * LANES:(tile + 1) * LANES]
        if block in biases:
            v = v + biases[block][plan_row:plan_row + SOFTMAX_ROWS, :]
        return v

    for row0, n_rows, blocks, biases in row_plan:
        for r in range(0, n_rows, SOFTMAX_ROWS):
            rows = slice(row0 + r, row0 + r + SOFTMAX_ROWS)
            for head in range(2):
                acc = None
                for block in blocks:
                    v = load(rows, block, head, r, biases)
                    acc = v if acc is None else jnp.maximum(acc, v)
                stat_ref[0, rows, head * LANES:(head + 1) * LANES] = acc
    mx = [jnp.maximum(jnp.max(stat_ref[0, :, head * LANES:(head + 1) * LANES], axis=-1, keepdims=True),
                      sinks[head]) for head in range(2)]
    for row0, n_rows, blocks, biases in row_plan:
        for r in range(0, n_rows, SOFTMAX_ROWS):
            rows = slice(row0 + r, row0 + r + SOFTMAX_ROWS)
            for head in range(2):
                m_rows = mx[head][row0 + r:row0 + r + SOFTMAX_ROWS]
                acc = None
                for block in blocks:
                    p = jnp.exp2(load(rows, block, head, r, biases) - m_rows)
                    acc = p if acc is None else acc + p
                    tile = 2 * block + head
                    p_ref[rows, tile * LANES:(tile + 1) * LANES] = p.astype(BF16)
                stat_ref[1, rows, head * LANES:(head + 1) * LANES] = acc
    recip = [1.0 / (jnp.sum(stat_ref[1, :, head * LANES:(head + 1) * LANES], axis=-1, keepdims=True)
                    + jnp.exp2(sinks[head] - mx[head])) for head in range(2)]
    lane = lax.broadcasted_iota(jnp.int32, (1, LANES), 1)
    return jnp.where(lane < HEAD_DIM, recip[0], recip[1])


def _attention_pipeline(units, sink_ref, attn_layer, o_scr, stat_scr):
    def scores(u):
        kk = u["kk"]()
        u["s_ref"][:, :kk.shape[0]] = _bdot_nt(u["q"](), kk)

    def values(u, recip):
        vv = u["vv"]()
        o = _bdot(u["p_ref"][:, :vv.shape[0]], vv) * recip
        o_scr[u["rows"], LANES * u["t"]:LANES * (u["t"] + 1)] = o.astype(BF16)

    scores(units[0])
    recip_prev = None
    for n, u in enumerate(units):
        if n + 1 < len(units):
            scores(units[n + 1])
        if n > 0:
            values(units[n - 1], recip_prev)
        sinks = (sink_ref[attn_layer, 2 * u["t"]] * LOG2E, sink_ref[attn_layer, 2 * u["t"] + 1] * LOG2E)
        recip_prev = _pair_softmax(u["s_ref"], u["p_ref"], stat_scr.at[n % 2], sinks, u["plan"])
    values(units[-1], recip_prev)


def _cached(fn):
    memo = {}

    def get(*key):
        if key not in memo:
            memo[key] = fn(*key)
        return memo[key]
    return get


def _prompt_attn_kernel(sink_ref, x_ref, mod_ref, wqkv_ref, wo_ref, g_ref, b_ref,
                        y_ref, k_ref, v_ref, o_scr, s_scr, p_scr, stat_scr, *, alpha, layer, attn_layer):
    n_seq, seq, _ = x_ref.shape
    m = mod_ref[...]
    x = x_ref[...].reshape(n_seq * seq, D_MODEL)
    h = (x * (1.0 + m[1:2]) + m[0:1]).astype(BF16)
    qkv = _bdot(h, wqkv_ref[...])
    q = (qkv[:, :D_MODEL] * Q_SCALE).astype(BF16)
    plan = [(0, seq, list(range(seq // BLOCK)), {})]
    units = []
    for sq in range(n_seq):
        rows = slice(sq * seq, (sq + 1) * seq)
        k = qkv[rows, D_MODEL:QK_WIDTH]
        v = qkv[rows, QK_WIDTH:]
        k_ref[sq] = k
        v_ref[sq] = v
        kk = _cached(functools.partial(_head_pair_operand, k))
        vv = _cached(functools.partial(_head_pair_operand, v))
        for t in range(D_MODEL // LANES):
            n = len(units)
            units.append(dict(q=functools.partial(lambda rows, t: q[rows, LANES * t:LANES * (t + 1)], rows, t),
                              kk=functools.partial(kk, t // (GROUP // 2)),
                              vv=functools.partial(vv, t // (GROUP // 2)),
                              t=t, rows=rows, s_ref=s_scr.at[n % 2], p_ref=p_scr.at[n % 2], plan=plan))
    _attention_pipeline(units, sink_ref, attn_layer, o_scr, stat_scr)
    out = _bdot(o_scr[...], wo_ref[...])
    z = alpha * x + m[2:3] * out
    y_ref[...] = _layer_norm(z, g_ref[layer:layer + 1, :], b_ref[layer:layer + 1, :]).reshape(n_seq, seq, D_MODEL)


def _prompt_attn_layer(x, mods, cond0, sink, layer, attn_layer, wqkv, wo, g, b, alpha):
    bsz, seq, _ = x.shape
    n_seq = PROMPT_SEQS_PER_STEP
    assert bsz % n_seq == 0
    width = 2 * (seq // BLOCK) * LANES
    kern = functools.partial(_prompt_attn_kernel, alpha=alpha, layer=layer, attn_layer=attn_layer)
    return pl.pallas_call(
        kern,
        grid=(bsz // n_seq,),
        in_specs=[
            pl.BlockSpec(memory_space=pltpu.SMEM),
            pl.BlockSpec((n_seq, seq, D_MODEL), lambda i: (i, 0, 0)),
            _mod_spec(layer, lambda i: cond0),
            _layer_spec(wqkv.shape, attn_layer),
            _layer_spec(wo.shape, attn_layer),
            _whole_spec(g.shape),
            _whole_spec(b.shape),
        ],
        out_specs=[
            pl.BlockSpec((n_seq, seq, D_MODEL), lambda i: (i, 0, 0)),
            pl.BlockSpec((n_seq, seq, KV_WIDTH), lambda i: (i, 0, 0)),
            pl.BlockSpec((n_seq, seq, KV_WIDTH), lambda i: (i, 0, 0)),
        ],
        out_shape=[
            jax.ShapeDtypeStruct((bsz, seq, D_MODEL), F32),
            jax.ShapeDtypeStruct((bsz, seq, KV_WIDTH), F32),
            jax.ShapeDtypeStruct((bsz, seq, KV_WIDTH), F32),
        ],
        scratch_shapes=[pltpu.VMEM((n_seq * seq, D_MODEL), BF16),
                        pltpu.VMEM((2, seq, width), F32),
                        pltpu.VMEM((2, seq, width), BF16),
                        pltpu.VMEM((2, 2, seq, 2 * LANES), F32)],
        compiler_params=_params(1),
        name="prompt_attn",
    )(sink, x, mods, wqkv, wo, g, b)


def _rope_tables(n_tokens):
    rows = n_tokens // GRID_W
    row = np.repeat(np.arange(rows, dtype=np.float32), GRID_W)
    col = np.tile(np.arange(GRID_W, dtype=np.float32), rows)
    half = HEAD_DIM // 2
    inv = np.float32(ROPE_THETA) ** (-np.arange(0, half, 2, dtype=np.float32) / np.float32(half))
    ang_r = row[:, None] * inv
    ang_c = col[:, None] * inv
    cos = np.concatenate([np.cos(ang_r)] * 2 + [np.cos(ang_c)] * 2, axis=-1)
    sin = np.concatenate([-np.sin(ang_r), np.sin(ang_r), -np.sin(ang_c), np.sin(ang_c)], axis=-1)
    reps = (1, LANES // HEAD_DIM)
    return jnp.asarray(np.tile(cos, reps), F32), jnp.asarray(np.tile(sin, reps), F32)


def _sample_qkv_kernel(x_ref, mod_ref, cos_ref, sin_ref, wqkv_ref, q_ref, kv_ref, h_scr):
    m = mod_ref[...]
    lane = lax.broadcasted_iota(jnp.int32, (1, LANES), 1)
    first = (lane % (HEAD_DIM // 2)) < (HEAD_DIM // 4)
    quarter = HEAD_DIM // 4
    h_scr[...] = (x_ref[...] * (1.0 + m[1:2]) + m[0:1]).astype(BF16)
    for r0 in range(0, x_ref.shape[0], QKV_SUB_ROWS):
        rows = slice(r0, r0 + QKV_SUB_ROWS)
        qkv = _bdot(h_scr[rows, :], wqkv_ref[...])
        cos = cos_ref[rows, :]
        sin = sin_ref[rows, :]
        for t in range(QK_WIDTH // LANES):
            tile = qkv[:, LANES * t:LANES * (t + 1)]
            partner = jnp.where(first, pltpu.roll(tile, LANES - quarter, axis=1),
                                pltpu.roll(tile, quarter, axis=1))
            rot = tile * cos + partner * sin
            if t < D_MODEL // LANES:
                q_ref[rows, LANES * t:LANES * (t + 1)] = (rot * Q_SCALE).astype(BF16)
            else:
                tk = t - D_MODEL // LANES
                kv_ref[rows, LANES * tk:LANES * (tk + 1)] = rot.astype(BF16)
        kv_ref[rows, KV_WIDTH:] = qkv[:, QK_WIDTH:].astype(BF16)


def _sample_qkv(x, mods, cond0, seq, cos, sin, layer, attn_layer, wqkv):
    m_rows = x.shape[0]
    tm = QKV_TILE_ROWS
    tiles_per_seq = seq // tm
    return pl.pallas_call(
        _sample_qkv_kernel,
        grid=(m_rows // tm,),
        in_specs=[
            pl.BlockSpec((tm, D_MODEL), lambda i: (i, 0)),
            _mod_spec(layer, lambda i: cond0 + i // tiles_per_seq),
            pl.BlockSpec((tm, LANES), lambda i: (i % tiles_per_seq, 0)),
            pl.BlockSpec((tm, LANES), lambda i: (i % tiles_per_seq, 0)),
            _layer_spec(wqkv.shape, attn_layer),
        ],
        out_specs=[
            pl.BlockSpec((tm, D_MODEL), lambda i: (i, 0)),
            pl.BlockSpec((tm, 2 * KV_WIDTH), lambda i: (i, 0)),
        ],
        out_shape=[
            jax.ShapeDtypeStruct((m_rows, D_MODEL), BF16),
            jax.ShapeDtypeStruct((m_rows, 2 * KV_WIDTH), BF16),
        ],
        scratch_shapes=[pltpu.VMEM((tm, D_MODEL), BF16)],
        compiler_params=_params(1),
        name="sample_qkv",
    )(x, mods, cos, sin, wqkv)


def _sample_attn_kernel(sink_ref, x_ref, mod_ref, q_ref, kvp_ref, kvm_ref, kvn_ref, ctx_ref,
                        wo_ref, g_ref, b_ref, y_ref, o_scr, s_scr, p_scr, stat_scr, bias_scr, *,
                        alpha, layer, attn_layer):
    i = pl.program_id(1)
    n_tiles = pl.num_programs(1)
    m = mod_ref[...]
    nc = ctx_ref.shape[1] // BLOCK
    window = 4 + nc
    half_rows = 2 * BLOCK
    kv_all = jnp.concatenate([kvp_ref[...].astype(F32), kvm_ref[0:BLOCK, :].astype(F32), ctx_ref[0],
                              kvm_ref[BLOCK:, :].astype(F32), kvn_ref[...].astype(F32)], axis=0)
    k_all, v_all = kv_all[:, :KV_WIDTH], kv_all[:, KV_WIDTH:]
    qi = lax.broadcasted_iota(jnp.int32, (BLOCK, BLOCK), 0)
    ki = lax.broadcasted_iota(jnp.int32, (BLOCK, BLOCK), 1)
    bias_scr[0] = jnp.where(ki >= qi, jnp.where(i > 0, 0.0, NEG_INF), NEG_INF)
    bias_scr[1] = jnp.where(ki <= qi, 0.0, NEG_INF)
    bias_scr[2] = jnp.where(ki >= qi, 0.0, NEG_INF)
    bias_scr[3] = jnp.where(ki <= qi, jnp.where(i < n_tiles - 1, 0.0, NEG_INF), NEG_INF)
    prev_edge, next_plain, prev_plain, next_edge = (bias_scr.at[j] for j in range(4))
    ctx_a = list(range(2, 2 + nc))
    ctx_b = list(range(nc))
    plans = [
        [(0, BLOCK, [0, 1] + ctx_a + [2 + nc], {0: prev_edge, 2 + nc: next_plain}),
         (BLOCK, BLOCK, [1] + ctx_a + [2 + nc, 3 + nc], {1: prev_plain, 3 + nc: next_plain})],
        [(0, BLOCK, ctx_b + [nc, nc + 1, nc + 2], {nc: prev_plain, nc + 2: next_plain}),
         (BLOCK, BLOCK, ctx_b + [nc + 1, nc + 2, nc + 3], {nc + 1: prev_plain, nc + 3: next_edge})],
    ]
    unused = [[3 + nc, 0], [nc + 3, nc]]
    for half in range(2):
        for parity in range(2):
            for qb in range(2):
                blk = unused[half][qb]
                p_scr[2 * half + parity, qb * BLOCK:(qb + 1) * BLOCK,
                      2 * blk * LANES:2 * (blk + 1) * LANES] = jnp.zeros((BLOCK, 2 * LANES), BF16)
    kk = _cached(lambda kv_head: _head_pair_operand(k_all, kv_head))
    vv = _cached(lambda kv_head: _head_pair_operand(v_all, kv_head))

    def operand(full, kv_head, half):
        first = 2 * half * 2 * BLOCK
        return full(kv_head)[first:first + window * 2 * BLOCK]

    units = []
    for kv_head in range(N_KV_HEADS):
        for half in range(2):
            for t in range(kv_head * (GROUP // 2), (kv_head + 1) * (GROUP // 2)):
                n = len(units)
                rows = slice(half * half_rows, (half + 1) * half_rows)
                units.append(dict(
                    q=functools.partial(lambda rows, t: q_ref[rows, LANES * t:LANES * (t + 1)], rows, t),
                    kk=functools.partial(operand, kk, kv_head, half),
                    vv=functools.partial(operand, vv, kv_head, half),
                    t=t, rows=rows, s_ref=s_scr.at[n % 2], p_ref=p_scr.at[2 * half + n % 2],
                    plan=plans[half]))
    _attention_pipeline(units, sink_ref, attn_layer, o_scr, stat_scr)
    out = _bdot(o_scr[...], wo_ref[...])
    z = alpha * x_ref[...] + m[2:3] * out
    y_ref[...] = _layer_norm(z, g_ref[layer:layer + 1, :], b_ref[layer:layer + 1, :])


def _sample_attn_layer(x, mods, cond0, seq, q, kv, ctx_kv, sink, layer, attn_layer, wo, g, b, alpha):
    m_rows = x.shape[0]
    bsz = m_rows // seq
    tq = 4 * BLOCK
    tiles = seq // tq
    blocks = seq // BLOCK
    n_ctx = ctx_kv.shape[1]
    assert n_ctx % BLOCK == 0 and seq % tq == 0
    window = 4 + n_ctx // BLOCK
    kern = functools.partial(_sample_attn_kernel, alpha=alpha, layer=layer, attn_layer=attn_layer)
    row_tile = lambda bi, i: (bi * tiles + i, 0)
    prev_blk = lambda bi, i: (bi * blocks + jnp.maximum(4 * i - 1, 0), 0)
    next_blk = lambda bi, i: (bi * blocks + jnp.minimum(4 * i + 4, blocks - 1), 0)
    ctx_blk = lambda bi, i: (bi, 0, 0)
    half_rows = 2 * BLOCK
    width = 2 * window * LANES + LANES
    return pl.pallas_call(
        kern,
        grid=(bsz, tiles),
        in_specs=[
            pl.BlockSpec(memory_space=pltpu.SMEM),
            pl.BlockSpec((tq, D_MODEL), row_tile),
            _mod_spec(layer, lambda bi, i: cond0 + bi),
            pl.BlockSpec((tq, D_MODEL), row_tile),
            pl.BlockSpec((BLOCK, 2 * KV_WIDTH), prev_blk),
            pl.BlockSpec((tq, 2 * KV_WIDTH), row_tile),
            pl.BlockSpec((BLOCK, 2 * KV_WIDTH), next_blk),
            pl.BlockSpec((1, n_ctx, 2 * KV_WIDTH), ctx_blk),
            _layer_spec(wo.shape, attn_layer),
            _whole_spec(g.shape),
            _whole_spec(b.shape),
        ],
        out_specs=pl.BlockSpec((tq, D_MODEL), row_tile),
        out_shape=jax.ShapeDtypeStruct((m_rows, D_MODEL), F32),
        scratch_shapes=[pltpu.VMEM((tq, D_MODEL), BF16),
                        pltpu.VMEM((2, half_rows, width), F32),
                        pltpu.VMEM((4, half_rows, width), BF16),
                        pltpu.VMEM((2, 2, half_rows, 2 * LANES), F32),
                        pltpu.VMEM((4, BLOCK, BLOCK), F32)],
        compiler_params=_params(2),
        name="sample_attn",
    )(sink, x, mods, q, kv, kv, kv, ctx_kv, wo, g, b)


def _glu_weight_kernel(w_ref, out_ref):
    for c in range(D_MODEL // LANES):
        out_ref[c, :, :LANES] = w_ref[:, c * LANES:(c + 1) * LANES].astype(BF16)
        out_ref[c, :, LANES:] = w_ref[:, D_MODEL + c * LANES:D_MODEL + (c + 1) * LANES].astype(BF16)


def _glu_weight_tiles(pw1):
    n_layers = pw1.shape[0]
    lane_tiles = D_MODEL // LANES
    tk = D_MODEL // 2
    return pl.pallas_call(
        _glu_weight_kernel,
        grid=(n_layers, D_MODEL // tk),
        in_specs=[pl.BlockSpec((None, tk, 2 * D_MODEL), lambda l, r: (l, r, 0))],
        out_specs=pl.BlockSpec((None, lane_tiles, tk, 2 * LANES), lambda l, r: (l, 0, r, 0)),
        out_shape=jax.ShapeDtypeStruct((n_layers, lane_tiles, D_MODEL, 2 * LANES), BF16),
        compiler_params=_params(2),
        name="glu_weight_tiles",
    )(pw1)


def _conv_kernel(xp_ref, x_ref, xn_ref, mod_ref, pw1_ref, pb1_ref, dw_ref, db_ref,
                 cg_ref, cb_ref, pw2_ref, pb2_ref, g_ref, b_ref, y_ref, *scratch,
                 alpha, layer, conv_layer, tm, sub, tiles_per_seq, seq_per_sub):
    i = pl.program_id(0)
    t = i % tiles_per_seq
    m = mod_ref[...]
    lane_tiles = D_MODEL // LANES
    n_sub = tm // sub
    h_scrs, u_scrs, c_scrs = scratch[:n_sub], scratch[n_sub:2 * n_sub], scratch[2 * n_sub:]
    recompute_halo = not seq_per_sub
    conv_row = slice(conv_layer, conv_layer + 1)

    def modulated_input(s):
        r0 = s * sub
        rows = x_ref[r0:r0 + sub, :]
        if recompute_halo:
            top = xp_ref[...] if s == 0 else x_ref[r0 - HALO:r0, :]
            bot = xn_ref[...] if s == n_sub - 1 else x_ref[r0 + sub:r0 + sub + HALO, :]
            rows = jnp.concatenate([top, rows, bot], axis=0)
        h_scrs[s][...] = (rows * (1.0 + m[1:2]) + m[0:1]).astype(BF16)

    def glu_chunk(s, c):
        a = _bdot(h_scrs[s][...], pw1_ref[c])
        value = a[:, :LANES] + pb1_ref[conv_row, c * LANES:(c + 1) * LANES]
        gate = a[:, LANES:] + pb1_ref[conv_row, D_MODEL + c * LANES:D_MODEL + (c + 1) * LANES]
        u = value * jax.nn.sigmoid(gate)
        zeros = jnp.zeros((HALO, LANES), F32)
        if recompute_halo:
            u_scrs[s][c, :, :] = u
            if s == 0:
                u_scrs[s][c, 0:HALO, :] = jnp.where(t > 0, u[:HALO], 0.0)
            if s == n_sub - 1:
                u_scrs[s][c, HALO + sub:, :] = jnp.where(t < tiles_per_seq - 1, u[HALO + sub:], 0.0)
        else:
            u_scrs[s][c, 0:HALO, :] = zeros
            u_scrs[s][c, HALO:HALO + sub, :] = u
            u_scrs[s][c, HALO + sub:, :] = zeros

    def conv_taps(s, c):
        span_rows = SUBLANES * CONV_ROW_STRIDE
        for base in range(0, sub, CONV_ROW_BLOCK):
            starts = [base + sp + j for sp in range(0, CONV_ROW_BLOCK, span_rows)
                      for j in range(CONV_ROW_STRIDE)]
            bias = db_ref[conv_row, c * LANES:(c + 1) * LANES]
            accs = [jnp.zeros((SUBLANES, LANES), F32) + bias for _ in starts]
            for tap in range(CONV_WIDTH):
                w = dw_ref[tap:tap + 1, c * LANES:(c + 1) * LANES]
                off = HALO - CONV_PAD + tap
                for n, start in enumerate(starts):
                    rows = pl.ds(start + off, SUBLANES, stride=CONV_ROW_STRIDE)
                    accs[n] = accs[n] + w * u_scrs[s][c, rows, :]
            for n, start in enumerate(starts):
                c_scrs[s][c, pl.ds(start, SUBLANES, stride=CONV_ROW_STRIDE), :] = accs[n]

    def finish(s):
        r0 = s * sub
        x = x_ref[r0:r0 + sub, :]
        conv = jnp.concatenate([c_scrs[s][c] for c in range(lane_tiles)], axis=1)
        un = _layer_norm(conv, cg_ref[conv_row, :], cb_ref[conv_row, :])
        un = un * jax.nn.sigmoid(un)
        out = _bdot(un.astype(BF16), pw2_ref[...]) + pb2_ref[conv_row, :]
        z = alpha * x + m[2:3] * out
        y_ref[r0:r0 + sub, :] = _layer_norm(z, g_ref[layer:layer + 1, :], b_ref[layer:layer + 1, :])

    for s in range(n_sub):
        modulated_input(s)
    for c in range(lane_tiles):
        glu_chunk(0, c)
    for s in range(n_sub):
        for c in range(lane_tiles):
            conv_taps(s, c)
            if s + 1 < n_sub:
                glu_chunk(s + 1, c)
        finish(s)


def _conv_layer(x, mods, cond0, rows_per_group, seq, layer, conv_layer, pw1, pb1, dw, db, cg, cb, pw2, pb2,
                g, b, alpha):
    m_rows = x.shape[0]
    tm = CONV_TILE_ROWS
    sub = CONV_SUB_ROWS
    seq_per_sub = seq == sub
    assert seq_per_sub or seq % tm == 0
    tiles_per_seq = max(seq // tm, 1)
    in_rows = sub if seq_per_sub else sub + 2 * HALO
    halo_per_tile = tm // HALO
    n_halo_blocks = m_rows // HALO
    kern = functools.partial(_conv_kernel, alpha=alpha, layer=layer, conv_layer=conv_layer, tm=tm, sub=sub,
                             tiles_per_seq=tiles_per_seq, seq_per_sub=seq_per_sub)
    prev_blk = lambda i: (jnp.maximum(i * halo_per_tile - 1, 0), 0)
    next_blk = lambda i: (jnp.minimum((i + 1) * halo_per_tile, n_halo_blocks - 1), 0)
    return pl.pallas_call(
        kern,
        grid=(m_rows // tm,),
        in_specs=[
            pl.BlockSpec((HALO, D_MODEL), prev_blk),
            pl.BlockSpec((tm, D_MODEL), lambda i: (i, 0)),
            pl.BlockSpec((HALO, D_MODEL), next_blk),
            _mod_spec(layer, lambda i: cond0 + i * tm // rows_per_group),
            _layer_spec(pw1.shape, conv_layer),
            _whole_spec(pb1.shape),
            _layer_spec(dw.shape, conv_layer),
            _whole_spec(db.shape),
            _whole_spec(cg.shape),
            _whole_spec(cb.shape),
            _layer_spec(pw2.shape, conv_layer),
            _whole_spec(pb2.shape),
            _whole_spec(g.shape),
            _whole_spec(b.shape),
        ],
        out_specs=pl.BlockSpec((tm, D_MODEL), lambda i: (i, 0)),
        out_shape=jax.ShapeDtypeStruct((m_rows, D_MODEL), F32),
        scratch_shapes=([pltpu.VMEM((in_rows, D_MODEL), BF16)] * (tm // sub)
                        + [pltpu.VMEM((D_MODEL // LANES, sub + 2 * HALO, LANES), F32)] * (tm // sub)
                        + [pltpu.VMEM((D_MODEL // LANES, sub, LANES), F32)] * (tm // sub)),
        compiler_params=_params(1),
        name="conv",
    )(x, x, x, mods, pw1, pb1, dw, db, cg, cb, pw2, pb2, g, b)


def kernel(x_prompt, x_sample, cache_k, cache_v, c, c_ctx, ada_w, ada_b, attn_w_qkv, attn_w_o, attn_sink, conv_pw1_w, conv_pw1_b, conv_dw_w, conv_dw_b, conv_norm_g, conv_norm_b, conv_pw2_w, conv_pw2_b, ln1_g, ln1_b, mlp_w1, mlp_b1, mlp_w2, mlp_b2, ln2_g, ln2_b):
    depth = ada_w.shape[0]
    bsz, seq, _ = x_prompt.shape
    dec_bsz, dec_seq, _ = x_sample.shape
    n_ctx = cache_k.shape[2]
    alpha = (2.0 * depth) ** 0.25
    assert 1 + dec_bsz <= COND_ROWS

    cond = jnp.concatenate([c_ctx[None], c, jnp.zeros((COND_ROWS - 1 - dec_bsz, D_MODEL), F32)], axis=0)
    mods = _modulation(cond, 1 + dec_bsz, ada_w, ada_b).reshape(depth, COND_ROWS, N_MOD, D_MODEL)
    cond_p, cond_s = 0, 1

    cos, sin = _rope_tables(dec_seq)

    wqkv, wo = attn_w_qkv.astype(BF16), attn_w_o.astype(BF16)
    pw1, pw2 = _glu_weight_tiles(conv_pw1_w), conv_pw2_w.astype(BF16)
    w1, w2 = mlp_w1.astype(BF16), mlp_w2.astype(BF16)
    ln1 = (ln1_g, ln1_b)
    conv_w = (pw1, conv_pw1_b, conv_dw_w, conv_dw_b, conv_norm_g, conv_norm_b, pw2, conv_pw2_b)
    mlp_w = (w1, mlp_b1, w2, mlp_b2, ln2_g, ln2_b)

    yp = x_prompt
    ys = x_sample.reshape(dec_bsz * dec_seq, D_MODEL)
    new_k, new_v = [], []
    for i in range(depth):
        j = i // N_MIXERS
        if i % N_MIXERS == 0:
            yp, kp, vp = _prompt_attn_layer(yp, mods, cond_p, attn_sink, i, j, wqkv, wo, *ln1, alpha)
            new_k.append(kp.reshape(bsz, seq, N_KV_HEADS, HEAD_DIM))
            new_v.append(vp.reshape(bsz, seq, N_KV_HEADS, HEAD_DIM))
            q, kv = _sample_qkv(ys, mods, cond_s, dec_seq, cos, sin, i, j, wqkv)
            ctx_kv = jnp.concatenate([cache_k[:, j].reshape(dec_bsz, n_ctx, KV_WIDTH),
                                      cache_v[:, j].reshape(dec_bsz, n_ctx, KV_WIDTH)], axis=-1)
            ys = _sample_attn_layer(ys, mods, cond_s, dec_seq, q, kv, ctx_kv, attn_sink, i, j, wo, *ln1, alpha)
        else:
            yp = _conv_layer(yp.reshape(bsz * seq, D_MODEL), mods, cond_p, bsz * seq, seq, i, j, *conv_w, *ln1,
                             alpha).reshape(bsz, seq, D_MODEL)
            ys = _conv_layer(ys, mods, cond_s, dec_seq, dec_seq, i, j, *conv_w, *ln1, alpha)
        yp = _mlp(yp.reshape(bsz * seq, D_MODEL), mods, cond_p, bsz * seq, i, *mlp_w, alpha).reshape(bsz, seq, D_MODEL)
        ys = _mlp(ys, mods, cond_s, dec_seq, i, *mlp_w, alpha)
    new_cache_k = jnp.stack(new_k, axis=1)
    new_cache_v = jnp.stack(new_v, axis=1)
    return (yp, ys.reshape(dec_bsz, dec_seq, D_MODEL), new_cache_k, new_cache_v)
```
